```python
import math
import jax
import jax.numpy as jnp
from jax import lax
import numpy as np

D_MODEL = 1024
BATCH = 1
SEQ = 16384
DEPTH = 4
DEC_BATCH = 4
DEC_SEQ = 8192
PAST_LEN = 128

N_MIXERS = 3
GRID_W = 64
PLE_DIM = 256
D_FF = 2816
RMS_EPS = 1e-6

HY_EMB = 33
HY_FILTER_ORDER = 64
HY_SHORT = 3
HY_DECAY_TARGET = 1e-2
HY_DECAY_PCT_HI = 0.3
HY_DECAY_PCT_LO = 1.5

NA_HEADS = 32
NA_HEAD_DIM = D_MODEL // NA_HEADS
NA_KH = 8
NA_KW = 16
NA_QW = 16
NA_KCOLS = 2 * NA_QW
NEG_INF = -1e30

LRU_WIDTH = D_MODEL
LRU_BLOCKS = 4
LRU_BLOCK = LRU_WIDTH // LRU_BLOCKS
LRU_CONV = 4
LRU_C = 8.0

kernel_name = 'hybrid_bidir_hyena_na_rglru_trunk'


def rmsnorm(x, g):
    xf = x.astype(jnp.float32)
    y = xf * lax.rsqrt(jnp.mean(xf * xf, axis=-1, keepdims=True) + RMS_EPS)
    return (y * g.astype(jnp.float32)).astype(x.dtype)


def swiglu(x, w_gate, w_up, w_down):
    return (jax.nn.silu(x @ w_gate) * (x @ w_up)) @ w_down


def depthwise_conv(u, w, b, left):
    width = w.shape[0]
    length = u.shape[1]
    up = jnp.pad(u, ((0, 0), (left, width - 1 - left), (0, 0)))
    y = b + w[0] * up[:, 0:length]
    for j in range(1, width):
        y = y + w[j] * up[:, j:j + length]
    return y


def hyena_filters(length, w1, b1, w2, b2, w3, b3, freq, w_out):
    f32 = jnp.float32
    t = jnp.linspace(0.0, 1.0, length, dtype=f32)[:, None]
    bands = (HY_EMB - 1) // 2
    ang = (2.0 * math.pi / length) * jnp.arange(length, dtype=f32)[:, None]
    fb = jnp.linspace(1e-4, bands - 1, bands, dtype=f32)[None, :]
    z = jnp.concatenate([t, jnp.cos(fb * ang), -jnp.sin(fb * ang)], axis=-1)
    fr = freq.astype(f32)
    h = jnp.sin(fr * (z @ w1.astype(f32) + b1.astype(f32)))
    h = jnp.sin(fr * (h @ w2.astype(f32) + b2.astype(f32)))
    h = jnp.sin(fr * (h @ w3.astype(f32) + b3.astype(f32)))
    k = h @ w_out.astype(f32)
    decay_min = math.log(HY_DECAY_TARGET) / HY_DECAY_PCT_LO
    decay_max = math.log(HY_DECAY_TARGET) / HY_DECAY_PCT_HI
    deltas = jnp.abs(jnp.linspace(decay_min, decay_max, D_MODEL, dtype=f32))
    window = jnp.exp(-t * deltas)
    k = k.reshape(length, 2, D_MODEL) * window[:, None, :]
    return k[:, 0], k[:, 1]


def bidir_long_conv(v, k_fwd, k_bwd):
    length = v.shape[1]
    kc = jnp.concatenate([k_fwd, jnp.zeros((1, D_MODEL), jnp.float32), k_bwd[:0:-1]], axis=0)
    kf = jnp.fft.rfft(kc, axis=0)
    vf = jnp.fft.rfft(v, n=2 * length, axis=1)
    return jnp.fft.irfft(vf * kf[None], n=2 * length, axis=1)[:, :length]


def hyena_mixer(x, prm, j):
    length = x.shape[1]
    u = x @ prm['hy_in_w'][j] + prm['hy_in_b'][j]
    u = depthwise_conv(u, prm['hy_conv_w'][j], prm['hy_conv_b'][j], (HY_SHORT - 1) // 2)
    x0, x1, v = jnp.split(u, 3, axis=-1)
    k_fwd, k_bwd = hyena_filters(length, prm['hy_f_w1'][j], prm['hy_f_b1'][j], prm['hy_f_w2'][j],
                                 prm['hy_f_b2'][j], prm['hy_f_w3'][j], prm['hy_f_b3'][j],
                                 prm['hy_f_freq'][j], prm['hy_f_out'][j])
    v = (v * x1).astype(jnp.float32)
    y = bidir_long_conv(v, k_fwd, k_bwd) + v * prm['hy_skip'][j].astype(jnp.float32)
    y = y.astype(x.dtype) * x0
    return y @ prm['hy_out_w'][j] + prm['hy_out_b'][j]


def na_mixer(x, prm, j):
    f32 = jnp.float32
    bsz, length, _ = x.shape
    rows = length // GRID_W
    kh = min(NA_KH, rows)
    ncb = GRID_W // NA_QW
    qkv = (x @ prm['na_qkv_w'][j] + prm['na_qkv_b'][j]).reshape(bsz, rows, GRID_W, 3, NA_HEADS, NA_HEAD_DIM)
    q = qkv[:, :, :, 0] * (NA_HEAD_DIM ** -0.5)
    k = qkv[:, :, :, 1]
    v = qkv[:, :, :, 2]
    row_start = np.clip(np.arange(rows) - kh // 2, 0, rows - kh)
    cb = np.arange(ncb)
    key_col0 = np.clip(cb * NA_QW - NA_KW // 2, 0, GRID_W - NA_KCOLS)
    key_cols = key_col0[:, None] + np.arange(NA_KCOLS)[None, :]
    q_cols = cb[:, None] * NA_QW + np.arange(NA_QW)[None, :]
    q_start = np.clip(q_cols - NA_KW // 2, 0, GRID_W - NA_KW)
    kc = key_cols[:, None, :]
    col_ok = (kc >= q_start[:, :, None]) & (kc < q_start[:, :, None] + NA_KW)
    dc_idx = np.clip(kc - q_cols[:, :, None] + NA_KW - 1, 0, 2 * NA_KW - 2)
    bias_tab = jnp.where(col_ok[None, None], prm['na_rpb'][j].astype(f32)[:, :, dc_idx], NEG_INF)

    def row_block(args):
        q_r, r, rs = args
        k_r = lax.dynamic_slice_in_dim(k, rs, kh, axis=1)[:, :, key_cols]
        v_r = lax.dynamic_slice_in_dim(v, rs, kh, axis=1)[:, :, key_cols]
        qb = q_r.reshape(bsz, ncb, NA_QW, NA_HEADS, NA_HEAD_DIM)
        s = jnp.einsum('bnqhd,binchd->bhnqic', qb, k_r, preferred_element_type=f32)
        dr_idx = rs + jnp.arange(kh, dtype=jnp.int32) - r + NA_KH - 1
        bias = jnp.take(bias_tab, dr_idx, axis=1).transpose(0, 2, 3, 1, 4)
        s = s + bias[None]
        p = jax.nn.softmax(s.reshape(bsz, NA_HEADS, ncb, NA_QW, kh * NA_KCOLS), axis=-1)
        p = p.reshape(s.shape).astype(v.dtype)
        o = jnp.einsum('bhnqic,binchd->bnqhd', p, v_r)
        return o.reshape(bsz, GRID_W, NA_HEADS, NA_HEAD_DIM)

    o = lax.map(row_block, (jnp.moveaxis(q, 1, 0), jnp.arange(rows, dtype=jnp.int32),
                            jnp.asarray(row_start, dtype=jnp.int32)))
    o = jnp.moveaxis(o, 0, 1).reshape(bsz, length, D_MODEL)
    return o @ prm['na_out_w'][j] + prm['na_out_b'][j]


def _lru_combine(e1, e2):
    a1, b1 = e1
    a2, b2 = e2
    return a1 * a2, a2 * b1 + b2


def rglru_mixer(x, prm, j):
    f32 = jnp.float32
    bsz, length, _ = x.shape
    u = x @ prm['lru_in_w'][j] + prm['lru_in_b'][j]
    gate_branch = jax.nn.gelu(u[..., :LRU_WIDTH])
    xb = depthwise_conv(u[..., LRU_WIDTH:], prm['lru_conv_w'][j], prm['lru_conv_b'][j], LRU_CONV // 2)
    xs = jnp.stack([xb, jnp.flip(xb, axis=1)]).astype(f32)
    xblk = xs.reshape(2, bsz, length, LRU_BLOCKS, LRU_BLOCK)

    def block_diag(w, b):
        y = jnp.einsum('dblnk,dnkj->dblnj', xblk, w.astype(f32)).reshape(2, bsz, length, LRU_WIDTH)
        return y + b.astype(f32)[:, None, None, :]

    gate_x = jax.nn.sigmoid(block_diag(prm['lru_wx'][j], prm['lru_bx'][j]))
    gate_a = jax.nn.sigmoid(block_diag(prm['lru_wa'][j], prm['lru_ba'][j]))
    log_a = -LRU_C * gate_a * jax.nn.softplus(-prm['lru_lambda'][j].astype(f32))[:, None, None, :]
    mult = jnp.sqrt(-jnp.expm1(2.0 * log_a)).at[:, :, 0].set(1.0)
    b = mult * gate_x * xs
    _, h = lax.associative_scan(_lru_combine, (jnp.exp(log_a), b), axis=2)
    y = h[0] + jnp.flip(h[1], axis=1)
    return (gate_branch * y.astype(x.dtype)) @ prm['lru_out_w'][j] + prm['lru_out_b'][j]


def trunk(x, p, prm):
    h = x
    for i in range(DEPTH):
        kind = i % N_MIXERS
        j = i // N_MIXERS
        h = h + 0.5 * swiglu(rmsnorm(h, prm['ln_ffn1'][i]), prm['ffn1_wg'][i], prm['ffn1_wu'][i], prm['ffn1_wd'][i])
        xn = rmsnorm(h, prm['ln_mix'][i])
        if kind == 0:
            m = hyena_mixer(xn, prm, j)
        elif kind == 1:
            m = na_mixer(xn, prm, j)
        else:
            m = rglru_mixer(xn, prm, j)
        h = h + m
        h = h + 0.5 * swiglu(rmsnorm(h, prm['ln_ffn2'][i]), prm['ffn2_wg'][i], prm['ffn2_wu'][i], prm['ffn2_wd'][i])
        gate = jax.nn.sigmoid(rmsnorm(h, prm['ln_ple'][i]) @ prm['ple_gate'][i])
        h = h + gate * (p[i] @ prm['ple_proj'][i])
    return rmsnorm(h, prm['ln_final'])


def setup_inputs(seed: int = 0) -> dict:
    key = jax.random.key(seed)
    keys = jax.random.split(key, 64)
    counter = [0]

    def nxt():
        kk = keys[counter[0]]
        counter[0] += 1
        return kk

    def nrm(shape, scale):
        return scale * jax.random.normal(nxt(), shape, jnp.float32)

    def gain(shape):
        return 1.0 + 0.01 * jax.random.normal(nxt(), shape, jnp.float32)

    n_a = len(range(0, DEPTH, N_MIXERS))
    n_b = len(range(1, DEPTH, N_MIXERS))
    n_c = len(range(2, DEPTH, N_MIXERS))
    d = D_MODEL
    r = LRU_WIDTH
    lam_u = jax.random.uniform(nxt(), (n_c, 2, r), jnp.float32, 0.9, 0.999)
    s = lam_u ** (1.0 / LRU_C)
    lru_lambda = jnp.log(s) - jnp.log1p(-s)
    return {
        'x_prompt': nrm((BATCH, SEQ, d), 1.0),
        'x_sample': nrm((DEC_BATCH, DEC_SEQ, d), 1.0),
        'p_prompt': nrm((DEPTH, BATCH, SEQ, PLE_DIM), 1.0),
        'p_sample': nrm((DEPTH, DEC_BATCH, DEC_SEQ, PLE_DIM), 1.0),
        'ln_ffn1': gain((DEPTH, d)),
        'ffn1_wg': nrm((DEPTH, d, D_FF), d ** -0.5),
        'ffn1_wu': nrm((DEPTH, d, D_FF), d ** -0.5),
        'ffn1_wd': nrm((DEPTH, D_FF, d), D_FF ** -0.5),
        'ln_mix': gain((DEPTH, d)),
        'ln_ffn2': gain((DEPTH, d)),
        'ffn2_wg': nrm((DEPTH, d, D_FF), d ** -0.5),
        'ffn2_wu': nrm((DEPTH, d, D_FF), d ** -0.5),
        'ffn2_wd': nrm((DEPTH, D_FF, d), D_FF ** -0.5),
        'ln_ple': gain((DEPTH, d)),
        'ple_gate': nrm((DEPTH, d, d), d ** -0.5),
        'ple_proj': nrm((DEPTH, PLE_DIM, d), PLE_DIM ** -0.5),
        'ln_final': gain((d,)),
        'hy_in_w': nrm((n_a, d, 3 * d), d ** -0.5),
        'hy_in_b': nrm((n_a, 3 * d), 0.01),
        'hy_conv_w': nrm((n_a, HY_SHORT, 3 * d), HY_SHORT ** -0.5),
        'hy_conv_b': nrm((n_a, 3 * d), 0.01),
        'hy_f_w1': nrm((n_a, HY_EMB, HY_FILTER_ORDER), HY_EMB ** -0.5),
        'hy_f_b1': nrm((n_a, HY_FILTER_ORDER), 0.01),
        'hy_f_w2': nrm((n_a, HY_FILTER_ORDER, HY_FILTER_ORDER), HY_FILTER_ORDER ** -0.5),
        'hy_f_b2': nrm((n_a, HY_FILTER_ORDER), 0.01),
        'hy_f_w3': nrm((n_a, HY_FILTER_ORDER, HY_FILTER_ORDER), HY_FILTER_ORDER ** -0.5),
        'hy_f_b3': nrm((n_a, HY_FILTER_ORDER), 0.01),
        'hy_f_freq': gain((n_a, HY_FILTER_ORDER)),
        'hy_f_out': nrm((n_a, HY_FILTER_ORDER, 2 * d), 0.02),
        'hy_skip': nrm((n_a, d), 1.0),
        'hy_out_w': nrm((n_a, d, d), d ** -0.5),
        'hy_out_b': nrm((n_a, d), 0.01),
        'na_qkv_w': nrm((n_b, d, 3 * d), d ** -0.5),
        'na_qkv_b': nrm((n_b, 3 * d), 0.01),
        'na_rpb': nrm((n_b, NA_HEADS, 2 * NA_KH - 1, 2 * NA_KW - 1), 0.02),
        'na_out_w': nrm((n_b, d, d), d ** -0.5),
        'na_out_b': nrm((n_b, d), 0.01),
        'lru_in_w': nrm((n_c, d, 2 * r), d ** -0.5),
        'lru_in_b': nrm((n_c, 2 * r), 0.01),
        'lru_conv_w': nrm((n_c, LRU_CONV, r), LRU_CONV ** -0.5),
        'lru_conv_b': nrm((n_c, r), 0.01),
        'lru_wa': nrm((n_c, 2, LRU_BLOCKS, LRU_BLOCK, LRU_BLOCK), LRU_BLOCK ** -0.5),
        'lru_ba': nrm((n_c, 2, r), 0.01),
        'lru_wx': nrm((n_c, 2, LRU_BLOCKS, LRU_BLOCK, LRU_BLOCK), LRU_BLOCK ** -0.5),
        'lru_bx': nrm((n_c, 2, r), 0.01),
        'lru_lambda': lru_lambda,
        'lru_out_w': nrm((n_c, r, d), r ** -0.5),
        'lru_out_b': nrm((n_c, d), 0.01),
    }


def reference(x_prompt, x_sample, p_prompt, p_sample,
              ln_ffn1, ffn1_wg, ffn1_wu, ffn1_wd, ln_mix, ln_ffn2, ffn2_wg, ffn2_wu, ffn2_wd,
              ln_ple, ple_gate, ple_proj, ln_final,
              hy_in_w, hy_in_b, hy_conv_w, hy_conv_b, hy_f_w1, hy_f_b1, hy_f_w2, hy_f_b2,
              hy_f_w3, hy_f_b3, hy_f_freq, hy_f_out, hy_skip, hy_out_w, hy_out_b,
              na_qkv_w, na_qkv_b, na_rpb, na_out_w, na_out_b,
              lru_in_w, lru_in_b, lru_conv_w, lru_conv_b, lru_wa, lru_ba, lru_wx, lru_bx,
              lru_lambda, lru_out_w, lru_out_b):
    prm = {
        'ln_ffn1': ln_ffn1, 'ffn1_wg': ffn1_wg, 'ffn1_wu': ffn1_wu, 'ffn1_wd': ffn1_wd,
        'ln_mix': ln_mix, 'ln_ffn2': ln_ffn2, 'ffn2_wg': ffn2_wg, 'ffn2_wu': ffn2_wu, 'ffn2_wd': ffn2_wd,
        'ln_ple': ln_ple, 'ple_gate': ple_gate, 'ple_proj': ple_proj, 'ln_final': ln_final,
        'hy_in_w': hy_in_w, 'hy_in_b': hy_in_b, 'hy_conv_w': hy_conv_w, 'hy_conv_b': hy_conv_b,
        'hy_f_w1': hy_f_w1, 'hy_f_b1': hy_f_b1, 'hy_f_w2': hy_f_w2, 'hy_f_b2': hy_f_b2,
        'hy_f_w3': hy_f_w3, 'hy_f_b3': hy_f_b3, 'hy_f_freq': hy_f_freq, 'hy_f_out': hy_f_out,
        'hy_skip': hy_skip, 'hy_out_w': hy_out_w, 'hy_out_b': hy_out_b,
        'na_qkv_w': na_qkv_w, 'na_qkv_b': na_qkv_b, 'na_rpb': na_rpb, 'na_out_w': na_out_w, 'na_out_b': na_out_b,
        'lru_in_w': lru_in_w, 'lru_in_b': lru_in_b, 'lru_conv_w': lru_conv_w, 'lru_conv_b': lru_conv_b,
        'lru_wa': lru_wa, 'lru_ba': lru_ba, 'lru_wx': lru_wx, 'lru_bx': lru_bx,
        'lru_lambda': lru_lambda, 'lru_out_w': lru_out_w, 'lru_out_b': lru_out_b,
    }
    y_prompt = trunk(x_prompt, p_prompt, prm)
    y_sample = trunk(x_sample, p_sample, prm)
    return (y_prompt, y_sample)
```

```python
import functools
import math

import numpy as np
import jax
import jax.numpy as jnp
from jax import lax
from jax.experimental import pallas as pl
from jax.experimental.pallas import tpu as pltpu

F32 = jnp.float32
BF16 = jnp.bfloat16

RMS_EPS = 1e-6
NEG_INF = -1e30
LANES = 128
SUBLANES = 8
VMEM_LIMIT = 56 * 1024 * 1024

GRID_W = 64
NA_HEADS = 32
NA_KH = 8
NA_KW = 16
LRU_BLOCKS = 4
LRU_CONV = 4
LRU_C = 8.0
HY_EMB = 33
HY_SHORT = 3
HY_DECAY_TARGET = 1e-2
HY_DECAY_PCT_HI = 0.3
HY_DECAY_PCT_LO = 1.5
N_MIXERS = 3

TOKEN_TILE = 512
HALO = SUBLANES


def _params(*sem):
    return pltpu.CompilerParams(dimension_semantics=sem, vmem_limit_bytes=VMEM_LIMIT)


def _resident(shape):
    nd = len(shape)
    return pl.BlockSpec(shape, lambda *_: (0,) * nd, pipeline_mode=pl.Buffered(1))


def _rms(x, g):
    ms = jnp.mean(x * x, axis=-1, keepdims=True)
    return x * lax.rsqrt(ms + RMS_EPS) * g


def _dot(a, b):
    return jnp.dot(a, b, preferred_element_type=F32)


def _row(v):
    return v.reshape(1, -1).astype(F32)


def _ffn_body(*refs, ple, final, n_chunks):
    h_ref, g_ref, wg_ref, wu_ref, wd_ref = refs[:5]
    o_ref = refs[-1]
    h = h_ref[...]
    xn = _rms(h, g_ref[...]).astype(BF16)
    d_ff = wg_ref.shape[1]
    fc = d_ff // n_chunks
    acc = jnp.zeros_like(h)
    for c in range(n_chunks):
        gt = _dot(xn, wg_ref[:, c * fc:(c + 1) * fc])
        up = _dot(xn, wu_ref[:, c * fc:(c + 1) * fc])
        act = (gt * jax.nn.sigmoid(gt) * up).astype(BF16)
        acc = acc + _dot(act, wd_ref[c * fc:(c + 1) * fc, :])
    h = h + 0.5 * acc
    k = 5
    if ple:
        gp_ref, wpg_ref, p_ref, wpp_ref = refs[k:k + 4]
        k += 4
        gate = jax.nn.sigmoid(_dot(_rms(h, gp_ref[...]).astype(BF16), wpg_ref[...]))
        h = h + gate * _dot(p_ref[...].astype(BF16), wpp_ref[...])
    if final:
        h = _rms(h, refs[k][...])
    o_ref[...] = h


def _ffn(h, g, wg, wu, wd, ple=None, final=None, tm=TOKEN_TILE):
    t, d = h.shape
    d_ff = wg.shape[1]
    tile = pl.BlockSpec((tm, d), lambda i: (i, 0))
    args = [h, _row(g), wg, wu, wd]
    specs = [tile, _resident((1, d)), _resident((d, d_ff)), _resident((d, d_ff)), _resident((d_ff, d))]
    if ple is not None:
        gp, wpg, p, wpp = ple
        args += [_row(gp), wpg, p, wpp]
        specs += [_resident((1, d)), _resident((d, d)), pl.BlockSpec((tm, p.shape[1]), lambda i: (i, 0)),
                  _resident(wpp.shape)]
    if final is not None:
        args.append(_row(final))
        specs.append(_resident((1, d)))
    return pl.pallas_call(
        functools.partial(_ffn_body, ple=ple is not None, final=final is not None, n_chunks=2),
        grid=(t // tm,), in_specs=specs, out_specs=tile,
        out_shape=jax.ShapeDtypeStruct((t, d), F32),
        compiler_params=_params("parallel"), name="ffn")(*args)


def _halo_specs(tm, d, n_rows):
    per = tm // HALO
    last = n_rows // HALO - 1
    return [pl.BlockSpec((HALO, d), lambda i: (jnp.maximum(i * per - 1, 0), 0)),
            pl.BlockSpec((tm, d), lambda i: (i, 0)),
            pl.BlockSpec((HALO, d), lambda i: (jnp.minimum((i + 1) * per, last), 0))]


def _normed_with_halo(hp_ref, hc_ref, hn_ref, g_ref):
    x = jnp.concatenate([hp_ref[...], hc_ref[...], hn_ref[...]], axis=0)
    return _rms(x, g_ref[...]).astype(BF16)


def _zero_outside_sequence(u, tiles_per_seq, tm):
    i = pl.program_id(0) % tiles_per_seq
    r = lax.broadcasted_iota(jnp.int32, (u.shape[0], 1), 0)
    outside = ((r < HALO) & (i == 0)) | ((r >= tm + HALO) & (i == tiles_per_seq - 1))
    return jnp.where(outside, 0.0, u)


def _proj_res_body(x_ref, w_ref, b_ref, h_ref, o_ref):
    o_ref[...] = h_ref[...] + _dot(x_ref[...], w_ref[...]) + b_ref[...]


def _proj_res(x, w, b, h, tm=TOKEN_TILE):
    t, d = h.shape
    tile = pl.BlockSpec((tm, d), lambda i: (i, 0))
    return pl.pallas_call(
        _proj_res_body, grid=(t // tm,),
        in_specs=[pl.BlockSpec((tm, x.shape[1]), lambda i: (i, 0)), _resident(w.shape), _resident((1, d)), tile],
        out_specs=tile, out_shape=jax.ShapeDtypeStruct((t, d), F32),
        compiler_params=_params("parallel"), name="proj_res")(x, w, _row(b), h)


def _lru_in_body(hp_ref, hc_ref, hn_ref, g_ref, w_ref, b_ref, cw_ref, cb_ref, gb_ref, xb_ref, u_s,
                 *, tiles_per_seq, tm):
    r_w = gb_ref.shape[1]
    xn = _normed_with_halo(hp_ref, hc_ref, hn_ref, g_ref)
    gb_ref[...] = jax.nn.gelu(_dot(xn[HALO:HALO + tm], w_ref[:, :r_w]) + b_ref[:, :r_w]).astype(BF16)
    u = _dot(xn, w_ref[:, r_w:]) + b_ref[:, r_w:]
    u_s[...] = _zero_outside_sequence(u, tiles_per_seq, tm)
    left = LRU_CONV // 2
    y = cb_ref[...]
    for j in range(LRU_CONV):
        y = y + cw_ref[j:j + 1, :] * u_s[pl.ds(HALO + j - left, tm), :]
    xb_ref[...] = y


def _lru_in(h, g, w, b, cw, cb, seq_len, tm=TOKEN_TILE):
    t, d = h.shape
    r_w = cw.shape[1]
    tile = pl.BlockSpec((tm, r_w), lambda i: (i, 0))
    return pl.pallas_call(
        functools.partial(_lru_in_body, tiles_per_seq=seq_len // tm, tm=tm),
        grid=(t // tm,),
        in_specs=_halo_specs(tm, d, t) + [_resident((1, d)), _resident(w.shape), _resident((1, 2 * r_w)),
                                          _resident(cw.shape), _resident((1, r_w))],
        out_specs=[tile, tile],
        out_shape=[jax.ShapeDtypeStruct((t, r_w), BF16), jax.ShapeDtypeStruct((t, r_w), F32)],
        scratch_shapes=[pltpu.VMEM((tm + 2 * HALO, r_w), F32)],
        compiler_params=_params("parallel"), name="lru_in")(h, h, h, _row(g), w, _row(b), cw, _row(cb))


def _block_diag(x, w_ref):
    nb, bw, _ = w_ref.shape
    return jnp.concatenate([_dot(x[:, n * bw:(n + 1) * bw], w_ref[n]) for n in range(nb)], axis=1)


def _lru_scan_tile(a, b, a_s, b_s, h_s, carry_ref, reverse):
    tm = a.shape[0]
    row = lax.broadcasted_iota(jnp.int32, (tm, 1), 0) % SUBLANES
    for dist in (1, 2, 4):
        if reverse:
            shift, ok = tm - dist, row < SUBLANES - dist
        else:
            shift, ok = dist, row >= dist
        a_prev = jnp.where(ok, pltpu.roll(a, shift, 0), 1.0)
        b_prev = jnp.where(ok, pltpu.roll(b, shift, 0), 0.0)
        b = a * b_prev + b
        a = a * a_prev
    a_s[...] = a
    b_s[...] = b
    n_groups = tm // SUBLANES
    edge = 0 if reverse else SUBLANES - 1

    def group(k, carry):
        gi = n_groups - 1 - k if reverse else k
        r0 = pl.multiple_of(gi * SUBLANES, SUBLANES)
        hg = a_s[pl.ds(r0, SUBLANES), :] * carry + b_s[pl.ds(r0, SUBLANES), :]
        h_s[pl.ds(r0, SUBLANES), :] = hg
        return hg[edge:edge + 1, :]

    carry_ref[...] = lax.fori_loop(0, n_groups, group, carry_ref[...])


def _lru_gates(xb, wa_ref, ba_ref, wx_ref, bx_ref, c8_ref, first_row):
    xb16 = xb.astype(BF16)
    gate_x = jax.nn.sigmoid(_block_diag(xb16, wx_ref) + bx_ref[...])
    gate_a = jax.nn.sigmoid(_block_diag(xb16, wa_ref) + ba_ref[...])
    log_a = gate_a * c8_ref[...]
    a = jnp.exp(log_a)
    mult = jnp.sqrt(1.0 - jnp.exp(2.0 * log_a))
    r = lax.broadcasted_iota(jnp.int32, (xb.shape[0], 1), 0)
    mult = jnp.where(r == first_row, 1.0, mult)
    return a, mult * gate_x * xb


def _lru_fwd_body(xb_ref, wa_ref, ba_ref, wx_ref, bx_ref, c8_ref, h0_ref, a_s, b_s, carry, *, tm):
    ti = pl.program_id(1)

    @pl.when(ti == 0)
    def _():
        carry[...] = jnp.zeros_like(carry)

    first_row = jnp.where(ti == 0, 0, -1)
    a, b = _lru_gates(xb_ref[...], wa_ref, ba_ref, wx_ref, bx_ref, c8_ref, first_row)
    _lru_scan_tile(a, b, a_s, b_s, h0_ref, carry, reverse=False)


def _lru_bwd_body(xb_ref, wa_ref, ba_ref, wx_ref, bx_ref, c8_ref, h0_ref, gb_ref, h_ref, wo_ref, bo_ref,
                  o_ref, a_s, b_s, h1_s, carry, *, tm):
    ti = pl.program_id(1)

    @pl.when(ti == 0)
    def _():
        carry[...] = jnp.zeros_like(carry)

    first_row = jnp.where(ti == 0, tm - 1, -1)
    a, b = _lru_gates(xb_ref[...], wa_ref, ba_ref, wx_ref, bx_ref, c8_ref, first_row)
    _lru_scan_tile(a, b, a_s, b_s, h1_s, carry, reverse=True)
    y = h0_ref[...] + h1_s[...]
    z = (gb_ref[...].astype(F32) * y).astype(BF16)
    o_ref[...] = h_ref[...] + _dot(z, wo_ref[...]) + bo_ref[...]


def _lru_mixer(h, g, prm, seq_len, tm=TOKEN_TILE):
    t, d = h.shape
    bsz = t // seq_len
    nt = seq_len // tm
    gb, xb = _lru_in(h, g, prm['in_w'], prm['in_b'], prm['conv_w'], prm['conv_b'], seq_len, tm)
    r_w = xb.shape[1]
    bw = r_w // LRU_BLOCKS
    gate_specs = lambda dr: [_resident((LRU_BLOCKS, bw, bw)), _resident((1, r_w)),
                             _resident((LRU_BLOCKS, bw, bw)), _resident((1, r_w)), _resident((1, r_w))]
    gate_args = lambda dr: [prm['wa'][dr], _row(prm['ba'][dr]), prm['wx'][dr], _row(prm['bx'][dr]),
                            _row(prm['c8'][dr])]
    fwd = lambda b, i: (b * nt + i, 0)
    bwd = lambda b, i: (b * nt + nt - 1 - i, 0)
    scan_scratch = [pltpu.VMEM((tm, r_w), F32), pltpu.VMEM((tm, r_w), F32)]
    h0 = pl.pallas_call(
        functools.partial(_lru_fwd_body, tm=tm), grid=(bsz, nt),
        in_specs=[pl.BlockSpec((tm, r_w), fwd)] + gate_specs(0),
        out_specs=pl.BlockSpec((tm, r_w), fwd), out_shape=jax.ShapeDtypeStruct((t, r_w), F32),
        scratch_shapes=scan_scratch + [pltpu.VMEM((1, r_w), F32)],
        compiler_params=_params("arbitrary", "arbitrary"), name="lru_fwd")(xb, *gate_args(0))
    return pl.pallas_call(
        functools.partial(_lru_bwd_body, tm=tm), grid=(bsz, nt),
        in_specs=[pl.BlockSpec((tm, r_w), bwd)] + gate_specs(1)
        + [pl.BlockSpec((tm, r_w), bwd), pl.BlockSpec((tm, r_w), bwd), pl.BlockSpec((tm, d), bwd),
           _resident((r_w, d)), _resident((1, d))],
        out_specs=pl.BlockSpec((tm, d), bwd), out_shape=jax.ShapeDtypeStruct((t, d), F32),
        scratch_shapes=scan_scratch + [pltpu.VMEM((tm, r_w), F32), pltpu.VMEM((1, r_w), F32)],
        compiler_params=_params("arbitrary", "arbitrary"), name="lru_bwd")(
            xb, *gate_args(1), h0, gb, h, prm['out_w'], _row(prm['out_b']))


NA_PAIR = 2 * GRID_W
NA_WIN = 5
NA_PB = 4
NA_HG = LANES // (1024 // NA_HEADS)


def _na_qkv_body(h_ref, g_ref, wq_ref, wkt_ref, wv_ref, bq_ref, bk_ref, bv_ref, q_ref, kt_ref, v_ref, *, scale):
    xn = _rms(h_ref[...], g_ref[...]).astype(BF16)
    q_ref[...] = ((_dot(xn, wq_ref[...]) + bq_ref[...]) * scale).astype(BF16)
    kt = lax.dot_general(wkt_ref[...], xn, (((1,), (1,)), ((), ())), preferred_element_type=F32)
    kt_ref[...] = (kt + bk_ref[...]).astype(BF16)
    v_ref[...] = (_dot(xn, wv_ref[...]) + bv_ref[...]).astype(BF16)


def _na_qkv(h, g, wq, wkt, wv, bq, bk, bv, seq_len, scale, tm=TOKEN_TILE):
    t, d = h.shape
    bsz, nt = t // seq_len, seq_len // tm
    tile = pl.BlockSpec((tm, d), lambda i: (i, 0))
    return pl.pallas_call(
        functools.partial(_na_qkv_body, scale=scale), grid=(t // tm,),
        in_specs=[tile, _resident((1, d)), _resident((d, d)), _resident((d, d)), _resident((d, d)),
                  _resident((1, d)), _resident((d, 1)), _resident((1, d))],
        out_specs=[tile, pl.BlockSpec((None, d, tm), lambda i: (i // nt, 0, i % nt)), tile],
        out_shape=[jax.ShapeDtypeStruct((t, d), BF16), jax.ShapeDtypeStruct((bsz, d, seq_len), BF16),
                   jax.ShapeDtypeStruct((t, d), BF16)],
        compiler_params=_params("parallel"), name="na_qkv")(
            h, _row(g), wq, wkt, wv, _row(bq), bk.reshape(d, 1).astype(F32), _row(bv))


def _na_attn_body(q_ref, kp_ref, kc_ref, kn_ref, vp_ref, vc_ref, vn_ref, sb_ref, o_ref, kwin, vwin,
                  *, n_pairs, rows):
    pb = pl.program_id(2)
    blk = NA_PB * NA_PAIR
    for bi, (k_ref, v_ref) in enumerate(((kp_ref, vp_ref), (kc_ref, vc_ref), (kn_ref, vn_ref))):
        for t in range(NA_PB):
            kwin[bi * NA_PB + t] = k_ref[:, t * NA_PAIR:(t + 1) * NA_PAIR]
        vwin[bi * blk:(bi + 1) * blk, :] = v_ref[...]
    dh = LANES // NA_HG
    lane = lax.broadcasted_iota(jnp.int32, (1, LANES), 1)
    q_row_in_pair = lax.broadcasted_iota(jnp.int32, (NA_PAIR, 1), 0) // GRID_W
    k_row_in_win = lax.broadcasted_iota(jnp.int32, (1, NA_WIN * NA_PAIR), 1) // GRID_W

    def pair(i, carry):
        p = pb * NA_PB + i
        kp0 = jnp.clip(p - 2, 0, n_pairs - NA_WIN)
        off = kp0 - (pb - 1) * NA_PB
        m0 = kp0 - p + (NA_WIN - 1)
        qi = q_ref[pl.ds(pl.multiple_of(i * NA_PAIR, NA_PAIR), NA_PAIR), :]
        ksl = jnp.concatenate([kwin[off + jp] for jp in range(NA_WIN)], axis=1)
        vsl = vwin[pl.ds(pl.multiple_of(off * NA_PAIR, NA_PAIR), NA_WIN * NA_PAIR), :]
        q_row = 2 * p + q_row_in_pair
        r_start = jnp.clip(q_row - NA_KH // 2, 0, rows - NA_KH)
        k_row = 2 * kp0 + k_row_in_win
        row_mask = jnp.where((k_row >= r_start) & (k_row < r_start + NA_KH), 0.0, NEG_INF)
        acc = jnp.zeros((NA_PAIR, LANES), F32)
        for j in range(NA_HG):
            in_head = (lane // dh) == j
            s = _dot(jnp.where(in_head, qi, jnp.zeros_like(qi)), ksl)
            bias = jnp.concatenate([sb_ref[j, m0 + jp] for jp in range(NA_WIN)], axis=1)
            s = s + bias + row_mask
            e = jnp.exp(s - jnp.max(s, axis=-1, keepdims=True))
            denom = jnp.sum(e, axis=-1, keepdims=True)
            acc = acc + jnp.where(in_head, _dot(e.astype(BF16), vsl) / denom, 0.0)
        o_ref[pl.ds(pl.multiple_of(i * NA_PAIR, NA_PAIR), NA_PAIR), :] = acc.astype(BF16)
        return carry

    lax.fori_loop(0, NA_PB, pair, 0)


def _na_bias_table(rpb):
    m = np.arange(2 * (NA_WIN - 1) + 1)[:, None, None]
    qa, qc = np.divmod(np.arange(NA_PAIR), GRID_W)
    ka, kc = np.divmod(np.arange(NA_PAIR), GRID_W)
    dr = 2 * (m - (NA_WIN - 1)) + ka[None, None, :] - qa[None, :, None]
    q_start = np.clip(qc - NA_KW // 2, 0, GRID_W - NA_KW)
    col_ok = (kc[None, :] >= q_start[:, None]) & (kc[None, :] < q_start[:, None] + NA_KW)
    dc = kc[None, :] - qc[:, None]
    ok = (np.abs(dr) <= NA_KH - 1) & col_ok[None]
    dr_idx = np.clip(dr + NA_KH - 1, 0, 2 * NA_KH - 2)
    dc_idx = np.broadcast_to(np.clip(dc + NA_KW - 1, 0, 2 * NA_KW - 2)[None], dr_idx.shape)
    tab = jnp.where(ok[None], rpb.astype(F32)[:, dr_idx, dc_idx], NEG_INF)
    return tab.reshape(NA_HEADS // NA_HG, NA_HG, *tab.shape[1:])


def _na_mixer(h, g, prm, seq_len):
    t, d = h.shape
    bsz = t // seq_len
    rows = seq_len // GRID_W
    n_pairs = rows // 2
    nblk = n_pairs // NA_PB
    blk = NA_PB * NA_PAIR
    q, kt, v = _na_qkv(h, g, prm['wq'], prm['wkt'], prm['wv'], prm['bq'], prm['bk'], prm['bv'], seq_len,
                       (d // NA_HEADS) ** -0.5)
    q = q.reshape(bsz, seq_len, d)
    v = v.reshape(bsz, seq_len, d)
    prev = lambda pb: jnp.maximum(pb - 1, 0)
    nxt = lambda pb: jnp.minimum(pb + 1, nblk - 1)
    k_spec = lambda f: pl.BlockSpec((None, LANES, blk), lambda gi, b, pb: (b, gi, f(pb)))
    v_spec = lambda f: pl.BlockSpec((None, blk, LANES), lambda gi, b, pb: (b, f(pb), gi))
    same = lambda pb: pb
    sb = prm['bias']
    o = pl.pallas_call(
        functools.partial(_na_attn_body, n_pairs=n_pairs, rows=rows),
        grid=(d // LANES, bsz, nblk),
        in_specs=[v_spec(same), k_spec(prev), k_spec(same), k_spec(nxt), v_spec(prev), v_spec(same), v_spec(nxt),
                  pl.BlockSpec((None,) + sb.shape[1:], lambda gi, b, pb: (gi, 0, 0, 0, 0))],
        out_specs=v_spec(same), out_shape=jax.ShapeDtypeStruct((bsz, seq_len, d), BF16),
        scratch_shapes=[pltpu.VMEM((3 * NA_PB, LANES, NA_PAIR), BF16), pltpu.VMEM((3 * blk, LANES), BF16)],
        compiler_params=_params("arbitrary", "arbitrary", "arbitrary"), name="na_attn")(
            q, kt, kt, kt, v, v, v, sb)
    return _proj_res(o.reshape(t, d), prm['out_w'], prm['out_b'], h)


def _split(x):
    hi = x.astype(BF16)
    return hi, (x - hi.astype(F32)).astype(BF16)


def _mm3(t_hi, t_lo, x):
    x_hi, x_lo = _split(x)
    return _dot(t_hi, x_hi) + (_dot(t_hi, x_lo) + _dot(t_lo, x_hi))


def _hy_in_body(hp_ref, hc_ref, hn_ref, g_ref, w_ref, b_ref, cw_ref, cb_ref, x0_ref, vv_ref, u_s,
                *, tiles_per_seq, tm):
    d = x0_ref.shape[1]
    xn = _normed_with_halo(hp_ref, hc_ref, hn_ref, g_ref)
    left = (HY_SHORT - 1) // 2

    def conv(c0):
        u = _dot(xn, w_ref[:, c0:c0 + d]) + b_ref[:, c0:c0 + d]
        u_s[...] = _zero_outside_sequence(u, tiles_per_seq, tm)
        y = cb_ref[:, c0:c0 + d]
        for j in range(HY_SHORT):
            y = y + cw_ref[j:j + 1, c0:c0 + d] * u_s[pl.ds(HALO + j - left, tm), :]
        return y

    x0_ref[...] = conv(0)
    x1 = conv(d)
    vv_ref[...] = conv(2 * d) * x1


def _hy_in(h, g, w, b, cw, cb, seq_len, tm=TOKEN_TILE):
    t, d = h.shape
    tile = pl.BlockSpec((tm, d), lambda i: (i, 0))
    return pl.pallas_call(
        functools.partial(_hy_in_body, tiles_per_seq=seq_len // tm, tm=tm), grid=(t // tm,),
        in_specs=_halo_specs(tm, d, t) + [_resident((1, d)), _resident(w.shape), _resident((1, 3 * d)),
                                          _resident(cw.shape), _resident((1, 3 * d))],
        out_specs=[tile, tile],
        out_shape=[jax.ShapeDtypeStruct((t, d), F32), jax.ShapeDtypeStruct((t, d), F32)],
        scratch_shapes=[pltpu.VMEM((tm + 2 * HALO, d), F32)],
        compiler_params=_params("parallel"), name="hy_in")(h, h, h, _row(g), w, _row(b), cw, _row(cb))


def _hy_filter_body(z_ref, w1_ref, b1_ref, w2_ref, b2_ref, w3_ref, b3_ref, fr_ref, wo_ref, dl_ref,
                    kf_ref, kb_ref):
    hp = lax.Precision.HIGHEST
    d = kf_ref.shape[1]
    z = z_ref[...]
    fr = fr_ref[...]
    a = jnp.sin(fr * (jnp.dot(z, w1_ref[...], precision=hp, preferred_element_type=F32) + b1_ref[...]))
    a = jnp.sin(fr * (jnp.dot(a, w2_ref[...], precision=hp, preferred_element_type=F32) + b2_ref[...]))
    a = jnp.sin(fr * (jnp.dot(a, w3_ref[...], precision=hp, preferred_element_type=F32) + b3_ref[...]))
    k = jnp.dot(a, wo_ref[...], precision=hp, preferred_element_type=F32)
    window = jnp.exp(-z[:, 0:1] * dl_ref[...])
    kf_ref[...] = k[:, :d] * window
    r = lax.broadcasted_iota(jnp.int32, (z.shape[0], 1), 0)
    kb_ref[...] = jnp.where((r == 0) & (pl.program_id(0) == 0), 0.0, k[:, d:] * window)


def _pad_to(x, shape):
    return jnp.pad(x.astype(F32), [(0, s - n) for s, n in zip(shape, x.shape)])


def _hy_filters(length, prm, d, tl=TOKEN_TILE):
    bands = (HY_EMB - 1) // 2
    t = jnp.linspace(0.0, 1.0, length, dtype=F32)[:, None]
    ang = (2.0 * math.pi / length) * jnp.arange(length, dtype=F32)[:, None]
    fb = jnp.linspace(1e-4, bands - 1, bands, dtype=F32)[None, :]
    z = _pad_to(jnp.concatenate([t, jnp.cos(fb * ang), -jnp.sin(fb * ang)], axis=-1), (length, LANES))
    decay_min = math.log(HY_DECAY_TARGET) / HY_DECAY_PCT_LO
    decay_max = math.log(HY_DECAY_TARGET) / HY_DECAY_PCT_HI
    deltas = jnp.abs(jnp.linspace(decay_min, decay_max, d, dtype=F32))[None, :]
    sq = (LANES, LANES)
    args = [z, _pad_to(prm['f_w1'], sq), _pad_to(prm['f_b1'][None], (1, LANES)),
            _pad_to(prm['f_w2'], sq), _pad_to(prm['f_b2'][None], (1, LANES)),
            _pad_to(prm['f_w3'], sq), _pad_to(prm['f_b3'][None], (1, LANES)),
            _pad_to(prm['f_freq'][None], (1, LANES)), _pad_to(prm['f_out'], (LANES, 2 * d)), deltas]
    tile = pl.BlockSpec((tl, d), lambda i: (i, 0))
    return pl.pallas_call(
        _hy_filter_body, grid=(length // tl,),
        in_specs=[pl.BlockSpec((tl, LANES), lambda i: (i, 0))] + [_resident(a.shape) for a in args[1:]],
        out_specs=[tile, tile],
        out_shape=[jax.ShapeDtypeStruct((length, d), F32)] * 2,
        compiler_params=_params("parallel"), name="hy_filter")(*args)


def _dft_plan(length):
    n = 2 * length
    n1 = min(128, n // 16)
    return n, n1, n // n1


@functools.lru_cache(maxsize=None)
def _dft_tables(length):
    n, n1, n2 = _dft_plan(length)
    n1h = n1 // 2
    k1 = np.arange(n1, dtype=np.int64)
    tt = np.arange(n2, dtype=np.int64)[:, None, None] + n2 * np.arange(n1h, dtype=np.int64)[None, None, :]
    th = 2.0 * np.pi * ((k1[None, :, None] * tt) % n) / n
    fwd1 = np.concatenate([np.cos(th), -np.sin(th)], axis=1)
    inv1 = np.concatenate([np.cos(th), -np.sin(th)], axis=1).transpose(0, 2, 1) / n
    idx = np.arange(n2, dtype=np.int64)
    th2 = 2.0 * np.pi * ((idx[:, None] * idx[None, :]) % n2) / n2
    c2, s2 = np.cos(th2), np.sin(th2)
    fwd2 = np.block([[c2, s2], [-s2, c2]])
    inv2 = np.block([[c2, -s2], [s2, c2]])

    def split(x):
        hi = x.astype(np.float32).astype(BF16)
        lo = (x - hi.astype(np.float64)).astype(np.float32).astype(BF16)
        return hi, lo

    return {name: split(tab) for name, tab in (('fwd1', fwd1), ('inv1', inv1), ('fwd2', fwd2), ('inv2', inv2))}


def _dft1_body(x_ref, th_ref, tl_ref, o_ref):
    n1 = o_ref.shape[1]
    for s in range(SUBLANES):
        r = _mm3(th_ref[s], tl_ref[s], x_ref[:, s, :])
        o_ref[0, :, s, :] = r[:n1]
        o_ref[1, :, s, :] = r[n1:]


def _dft1(x, length, tabs):
    bx, _, d = x.shape
    n, n1, n2 = _dft_plan(length)
    th, tl = tabs['fwd1']
    tab_spec = pl.BlockSpec((SUBLANES, 2 * n1, n1 // 2), lambda j, b: (j, 0, 0))
    a = pl.pallas_call(
        _dft1_body, grid=(n2 // SUBLANES, bx),
        in_specs=[pl.BlockSpec((None, n1 // 2, None, SUBLANES, d), lambda j, b: (b, 0, j, 0, 0)), tab_spec, tab_spec],
        out_specs=pl.BlockSpec((None, 2, n1, None, SUBLANES, d), lambda j, b: (b, 0, 0, j, 0, 0)),
        out_shape=jax.ShapeDtypeStruct((bx, 2, n1, n2 // SUBLANES, SUBLANES, d), F32),
        compiler_params=_params("parallel", "arbitrary"), name="hy_dft1")(
            x.reshape(bx, n1 // 2, n2 // SUBLANES, SUBLANES, d), th, tl)
    return a.reshape(bx, 2, n1, n2, d)


def _filter_spectrum_body(a_ref, th_ref, tl_ref, o_ref):
    n2 = o_ref.shape[1]
    d = o_ref.shape[2]
    xf = _mm3(th_ref[...], tl_ref[...], a_ref[0].reshape(2 * n2, d))
    xb = _mm3(th_ref[...], tl_ref[...], a_ref[1].reshape(2 * n2, d))
    o_ref[0] = xf[:n2] + xb[:n2]
    o_ref[1] = xf[n2:] - xb[n2:]


def _filter_spectrum(kf, kb, length, tabs):
    d = kf.shape[1]
    n, n1, n2 = _dft_plan(length)
    a = _dft1(jnp.stack([kf, kb]), length, tabs)
    th, tl = tabs['fwd2']
    return pl.pallas_call(
        _filter_spectrum_body, grid=(n1,),
        in_specs=[pl.BlockSpec((2, 2, None, n2, d), lambda k: (0, 0, k, 0, 0)), _resident(th.shape),
                  _resident(tl.shape)],
        out_specs=pl.BlockSpec((2, None, n2, d), lambda k: (0, k, 0, 0)),
        out_shape=jax.ShapeDtypeStruct((2, n1, n2, d), F32),
        compiler_params=_params("parallel"), name="hy_filter_spectrum")(a, th, tl)


def _spectral_body(a_ref, k_ref, fh_ref, fl_ref, ih_ref, il_ref, o_ref):
    _, n2, d = o_ref.shape
    x = _mm3(fh_ref[...], fl_ref[...], a_ref[...].reshape(2 * n2, d))
    xr, xi = x[:n2], x[n2:]
    kr, ki = k_ref[0], k_ref[1]
    y = jnp.concatenate([xr * kr - xi * ki, xr * ki + xi * kr], axis=0)
    o_ref[...] = _mm3(ih_ref[...], il_ref[...], y).reshape(2, n2, d)


def _spectral_multiply(a, kc, length, tabs):
    bx = a.shape[0]
    _, n1, n2, d = kc.shape
    fh, fl = tabs['fwd2']
    ih, il = tabs['inv2']
    blk = pl.BlockSpec((None, 2, None, n2, d), lambda k, b: (b, 0, k, 0, 0))
    return pl.pallas_call(
        _spectral_body, grid=(n1, bx),
        in_specs=[blk, pl.BlockSpec((2, None, n2, d), lambda k, b: (0, k, 0, 0))]
        + [_resident(fh.shape)] * 4,
        out_specs=blk, out_shape=jax.ShapeDtypeStruct((bx, 2, n1, n2, d), F32),
        compiler_params=_params("parallel", "arbitrary"), name="hy_spectral")(
            a, kc, fh, fl, ih, il)


def _idft1_body(z_ref, th_ref, tl_ref, o_ref):
    for s in range(SUBLANES):
        z = jnp.concatenate([z_ref[0, :, s, :], z_ref[1, :, s, :]], axis=0)
        o_ref[:, s, :] = _mm3(th_ref[s], tl_ref[s], z)


def _idft1(z, length, tabs):
    bx, _, n1, n2, d = z.shape
    th, tl = tabs['inv1']
    tab_spec = pl.BlockSpec((SUBLANES, n1 // 2, 2 * n1), lambda j, b: (j, 0, 0))
    y = pl.pallas_call(
        _idft1_body, grid=(n2 // SUBLANES, bx),
        in_specs=[pl.BlockSpec((None, 2, n1, None, SUBLANES, d), lambda j, b: (b, 0, 0, j, 0, 0)), tab_spec, tab_spec],
        out_specs=pl.BlockSpec((None, n1 // 2, None, SUBLANES, d), lambda j, b: (b, 0, j, 0, 0)),
        out_shape=jax.ShapeDtypeStruct((bx, n1 // 2, n2 // SUBLANES, SUBLANES, d), F32),
        compiler_params=_params("parallel", "arbitrary"), name="hy_idft1")(
            z.reshape(bx, 2, n1, n2 // SUBLANES, SUBLANES, d), th, tl)
    return y.reshape(bx, length, d)


def _hy_out_body(y_ref, vv_ref, x0_ref, h_ref, sk_ref, w_ref, b_ref, o_ref):
    y = (y_ref[...] + vv_ref[...] * sk_ref[...]) * x0_ref[...]
    o_ref[...] = h_ref[...] + _dot(y.astype(BF16), w_ref[...]) + b_ref[...]


def _hy_out(yc, vv, x0, h, skip, w, b, tm=TOKEN_TILE):
    t, d = h.shape
    tile = pl.BlockSpec((tm, d), lambda i: (i, 0))
    return pl.pallas_call(
        _hy_out_body, grid=(t // tm,),
        in_specs=[tile, tile, tile, tile, _resident((1, d)), _resident((d, d)), _resident((1, d))],
        out_specs=tile, out_shape=jax.ShapeDtypeStruct((t, d), F32),
        compiler_params=_params("parallel"), name="hy_out")(yc, vv, x0, h, _row(skip), w, _row(b))


def _hyena_mixer(h, g, prm, seq_len):
    t, d = h.shape
    bsz = t // seq_len
    tabs = {k: (jnp.asarray(v[0]), jnp.asarray(v[1])) for k, v in _dft_tables(seq_len).items()}
    x0, vv = _hy_in(h, g, prm['in_w'], prm['in_b'], prm['conv_w'], prm['conv_b'], seq_len)
    kf, kb = _hy_filters(seq_len, prm, d)
    kc = _filter_spectrum(kf, kb, seq_len, tabs)
    a = _dft1(vv.reshape(bsz, seq_len, d), seq_len, tabs)
    z = _spectral_multiply(a, kc, seq_len, tabs)
    yc = _idft1(z, seq_len, tabs).reshape(t, d)
    return _hy_out(yc, vv, x0, h, prm['skip'], prm['out_w'], prm['out_b'])


def _trunk(x, p, w, depth):
    bsz, seq_len, d = x.shape
    h = x.reshape(bsz * seq_len, d)
    p = p.reshape(depth, bsz * seq_len, -1)
    for i in range(depth):
        kind, j = i % N_MIXERS, i // N_MIXERS
        h = _ffn(h, w['ln_ffn1'][i], w['ffn1_wg'][i], w['ffn1_wu'][i], w['ffn1_wd'][i])
        mix = (_hyena_mixer, _na_mixer, _lru_mixer)[kind]
        h = mix(h, w['ln_mix'][i], w[('hy', 'na', 'lru')[kind]][j], seq_len)
        h = _ffn(h, w['ln_ffn2'][i], w['ffn2_wg'][i], w['ffn2_wu'][i], w['ffn2_wd'][i],
                 ple=(w['ln_ple'][i], w['ple_gate'][i], p[i], w['ple_proj'][i]),
                 final=w['ln_final'] if i == depth - 1 else None)
    return h.reshape(bsz, seq_len, d)


def kernel(x_prompt, x_sample, p_prompt, p_sample, ln_ffn1, ffn1_wg, ffn1_wu, ffn1_wd, ln_mix, ln_ffn2, ffn2_wg, ffn2_wu, ffn2_wd, ln_ple, ple_gate, ple_proj, ln_final, hy_in_w, hy_in_b, hy_conv_w, hy_conv_b, hy_f_w1, hy_f_b1, hy_f_w2, hy_f_b2, hy_f_w3, hy_f_b3, hy_f_freq, hy_f_out, hy_skip, hy_out_w, hy_out_b, na_qkv_w, na_qkv_b, na_rpb, na_out_w, na_out_b, lru_in_w, lru_in_b, lru_conv_w, lru_conv_b, lru_wa, lru_ba, lru_wx, lru_bx, lru_lambda, lru_out_w, lru_out_b):
    depth = ln_ffn1.shape[0]
    d = x_prompt.shape[-1]
    b16 = lambda a: a.astype(BF16)
    w = {
        'ln_ffn1': ln_ffn1, 'ffn1_wg': b16(ffn1_wg), 'ffn1_wu': b16(ffn1_wu), 'ffn1_wd': b16(ffn1_wd),
        'ln_mix': ln_mix, 'ln_ffn2': ln_ffn2, 'ffn2_wg': b16(ffn2_wg), 'ffn2_wu': b16(ffn2_wu),
        'ffn2_wd': b16(ffn2_wd), 'ln_ple': ln_ple, 'ple_gate': b16(ple_gate), 'ple_proj': b16(ple_proj),
        'ln_final': ln_final,
        'hy': [dict(in_w=b16(hy_in_w[j]), in_b=hy_in_b[j], conv_w=hy_conv_w[j], conv_b=hy_conv_b[j],
                    f_w1=hy_f_w1[j], f_b1=hy_f_b1[j], f_w2=hy_f_w2[j], f_b2=hy_f_b2[j], f_w3=hy_f_w3[j],
                    f_b3=hy_f_b3[j], f_freq=hy_f_freq[j], f_out=hy_f_out[j], skip=hy_skip[j],
                    out_w=b16(hy_out_w[j]), out_b=hy_out_b[j]) for j in range(hy_in_w.shape[0])],
        'na': [dict(wq=b16(na_qkv_w[j][:, :d]), wkt=b16(na_qkv_w[j][:, d:2 * d].T), wv=b16(na_qkv_w[j][:, 2 * d:]),
                    bq=na_qkv_b[j][:d], bk=na_qkv_b[j][d:2 * d], bv=na_qkv_b[j][2 * d:],
                    bias=_na_bias_table(na_rpb[j]), out_w=b16(na_out_w[j]), out_b=na_out_b[j])
               for j in range(na_qkv_w.shape[0])],
        'lru': [dict(in_w=b16(lru_in_w[j]), in_b=lru_in_b[j], conv_w=lru_conv_w[j], conv_b=lru_conv_b[j],
                     wa=b16(lru_wa[j]), ba=lru_ba[j], wx=b16(lru_wx[j]), bx=lru_bx[j],
                     c8=-LRU_C * jax.nn.softplus(-lru_lambda[j].astype(F32)),
                     out_w=b16(lru_out_w[j]), out_b=lru_out_b[j]) for j in range(lru_in_w.shape[0])],
    }
    return _trunk(x_prompt, p_prompt, w, depth), _trunk(x_sample, p_sample, w, depth)
```

```python
import functools
import math

import numpy as np
import jax
import jax.numpy as jnp
from jax import lax
from jax.experimental import pallas as pl
from jax.experimental.pallas import tpu as pltpu

F32 = jnp.float32
BF16 = jnp.bfloat16

RMS_EPS = 1e-6
NEG_INF = -1e30
LANES = 128
SUBLANES = 8
VMEM_LIMIT = 56 * 1024 * 1024

GRID_W = 64
NA_HEADS = 32
NA_KH = 8
NA_KW = 16
LRU_BLOCKS = 4
LRU_CONV = 4
LRU_C = 8.0
HY_EMB = 33
HY_SHORT = 3
HY_DECAY_TARGET = 1e-2
HY_DECAY_PCT_HI = 0.3
HY_DECAY_PCT_LO = 1.5
N_MIXERS = 3

TOKEN_TILE = 512
FFN_TILE = 1024
HALO = SUBLANES


def _params(*sem):
    return pltpu.CompilerParams(dimension_semantics=sem, vmem_limit_bytes=VMEM_LIMIT)


def _resident(shape):
    nd = len(shape)
    return pl.BlockSpec(shape, lambda *_: (0,) * nd, pipeline_mode=pl.Buffered(1))


def _rms(x, g):
    ms = jnp.mean(x * x, axis=-1, keepdims=True)
    return x * lax.rsqrt(ms + RMS_EPS) * g


def _dot(a, b):
    return jnp.dot(a, b, preferred_element_type=F32)


def _row(v):
    return v.reshape(1, -1).astype(F32)


def _ffn_body(*refs, ple, final, n_chunks):
    h_ref, g_ref, wg_ref, wu_ref, wd_ref = refs[:5]
    o_ref = refs[-1]
    h = h_ref[...]
    xn = _rms(h, g_ref[...]).astype(BF16)
    d_ff = wg_ref.shape[1]
    fc = d_ff // n_chunks
    acc = jnp.zeros_like(h)
    for c in range(n_chunks):
        gt = _dot(xn, wg_ref[:, c * fc:(c + 1) * fc])
        up = _dot(xn, wu_ref[:, c * fc:(c + 1) * fc])
        act = (gt * jax.nn.sigmoid(gt) * up).astype(BF16)
        acc = acc + _dot(act, wd_ref[c * fc:(c + 1) * fc, :])
    h = h + 0.5 * acc
    k = 5
    if ple:
        gp_ref, wpg_ref, p_ref, wpp_ref = refs[k:k + 4]
        k += 4
        gate = jax.nn.sigmoid(_dot(_rms(h, gp_ref[...]).astype(BF16), wpg_ref[...]))
        h = h + gate * _dot(p_ref[...].astype(BF16), wpp_ref[...])
    if final:
        h = _rms(h, refs[k][...])
    o_ref[...] = h


def _ffn(h, g, wg, wu, wd, ple=None, final=None, tm=FFN_TILE):
    t, d = h.shape
    d_ff = wg.shape[1]
    tile = pl.BlockSpec((tm, d), lambda i: (i, 0))
    args = [h, _row(g), wg, wu, wd]
    specs = [tile, _resident((1, d)), _resident((d, d_ff)), _resident((d, d_ff)), _resident((d_ff, d))]
    if ple is not None:
        gp, wpg, p, wpp = ple
        args += [_row(gp), wpg, p, wpp]
        specs += [_resident((1, d)), _resident((d, d)), pl.BlockSpec((tm, p.shape[1]), lambda i: (i, 0)),
                  _resident(wpp.shape)]
    if final is not None:
        args.append(_row(final))
        specs.append(_resident((1, d)))
    return pl.pallas_call(
        functools.partial(_ffn_body, ple=ple is not None, final=final is not None, n_chunks=2),
        grid=(t // tm,), in_specs=specs, out_specs=tile,
        out_shape=jax.ShapeDtypeStruct((t, d), F32),
        compiler_params=_params("parallel"), name="ffn")(*args)


def _halo_specs(tm, d, n_rows):
    per = tm // HALO
    last = n_rows // HALO - 1
    return [pl.BlockSpec((HALO, d), lambda i: (jnp.maximum(i * per - 1, 0), 0)),
            pl.BlockSpec((tm, d), lambda i: (i, 0)),
            pl.BlockSpec((HALO, d), lambda i: (jnp.minimum((i + 1) * per, last), 0))]


def _normed_with_halo(hp_ref, hc_ref, hn_ref, g_ref):
    x = jnp.concatenate([hp_ref[...], hc_ref[...], hn_ref[...]], axis=0)
    return _rms(x, g_ref[...]).astype(BF16)


def _store_zero_padded(u_s, u, tiles_per_seq, tm):
    i = pl.program_id(0) % tiles_per_seq
    u_s[HALO:tm + HALO, :] = u[HALO:tm + HALO]
    u_s[0:HALO, :] = jnp.where(i == 0, 0.0, u[0:HALO])
    u_s[tm + HALO:tm + 2 * HALO, :] = jnp.where(i == tiles_per_seq - 1, 0.0, u[tm + HALO:tm + 2 * HALO])


def _proj_res_body(x_ref, w_ref, b_ref, h_ref, o_ref):
    o_ref[...] = h_ref[...] + _dot(x_ref[...], w_ref[...]) + b_ref[...]


def _proj_res(x, w, b, h, tm=TOKEN_TILE):
    t, d = h.shape
    tile = pl.BlockSpec((tm, d), lambda i: (i, 0))
    return pl.pallas_call(
        _proj_res_body, grid=(t // tm,),
        in_specs=[pl.BlockSpec((tm, x.shape[1]), lambda i: (i, 0)), _resident(w.shape), _resident((1, d)), tile],
        out_specs=tile, out_shape=jax.ShapeDtypeStruct((t, d), F32),
        compiler_params=_params("parallel"), name="proj_res")(x, w, _row(b), h)


def _lru_in_body(hp_ref, hc_ref, hn_ref, g_ref, w_ref, b_ref, cw_ref, cb_ref, gb_ref, xb_ref, u_s,
                 *, tiles_per_seq, tm):
    r_w = gb_ref.shape[1]
    xn = _normed_with_halo(hp_ref, hc_ref, hn_ref, g_ref)
    gb_ref[...] = jax.nn.gelu(_dot(xn[HALO:HALO + tm], w_ref[:, :r_w]) + b_ref[:, :r_w]).astype(BF16)
    u = _dot(xn, w_ref[:, r_w:]) + b_ref[:, r_w:]
    _store_zero_padded(u_s, u, tiles_per_seq, tm)
    left = LRU_CONV // 2
    y = cb_ref[...]
    for j in range(LRU_CONV):
        y = y + cw_ref[j:j + 1, :] * u_s[pl.ds(HALO + j - left, tm), :]
    xb_ref[...] = y


def _lru_in(h, g, w, b, cw, cb, seq_len, tm=TOKEN_TILE):
    t, d = h.shape
    r_w = cw.shape[1]
    tile = pl.BlockSpec((tm, r_w), lambda i: (i, 0))
    return pl.pallas_call(
        functools.partial(_lru_in_body, tiles_per_seq=seq_len // tm, tm=tm),
        grid=(t // tm,),
        in_specs=_halo_specs(tm, d, t) + [_resident((1, d)), _resident(w.shape), _resident((1, 2 * r_w)),
                                          _resident(cw.shape), _resident((1, r_w))],
        out_specs=[tile, tile],
        out_shape=[jax.ShapeDtypeStruct((t, r_w), BF16), jax.ShapeDtypeStruct((t, r_w), F32)],
        scratch_shapes=[pltpu.VMEM((tm + 2 * HALO, r_w), F32)],
        compiler_params=_params("parallel"), name="lru_in")(h, h, h, _row(g), w, _row(b), cw, _row(cb))


def _block_diag(x, w_ref):
    nb, bw, _ = w_ref.shape
    return jnp.concatenate([_dot(x[:, n * bw:(n + 1) * bw], w_ref[n]) for n in range(nb)], axis=1)


def _lru_scan_tile(a, b, a_s, b_s, h_s, carry_ref, reverse):
    tm = a.shape[0]
    row = lax.broadcasted_iota(jnp.int32, (tm, 1), 0) % SUBLANES
    for dist in (1, 2, 4):
        if reverse:
            shift, ok = tm - dist, row < SUBLANES - dist
        else:
            shift, ok = dist, row >= dist
        a_prev = jnp.where(ok, pltpu.roll(a, shift, 0), 1.0)
        b_prev = jnp.where(ok, pltpu.roll(b, shift, 0), 0.0)
        b = a * b_prev + b
        a = a * a_prev
    a_s[...] = a
    b_s[...] = b
    n_groups = tm // SUBLANES
    edge = 0 if reverse else SUBLANES - 1

    def group(k, carry):
        gi = n_groups - 1 - k if reverse else k
        r0 = pl.multiple_of(gi * SUBLANES, SUBLANES)
        hg = a_s[pl.ds(r0, SUBLANES), :] * carry + b_s[pl.ds(r0, SUBLANES), :]
        h_s[pl.ds(r0, SUBLANES), :] = hg
        return hg[edge:edge + 1, :]

    carry_ref[...] = lax.fori_loop(0, n_groups, group, carry_ref[...])


def _lru_gates(xb, wa_ref, ba_ref, wx_ref, bx_ref, c8_ref, first_row):
    xb16 = xb.astype(BF16)
    gate_x = jax.nn.sigmoid(_block_diag(xb16, wx_ref) + bx_ref[...])
    gate_a = jax.nn.sigmoid(_block_diag(xb16, wa_ref) + ba_ref[...])
    log_a = gate_a * c8_ref[...]
    a = jnp.exp(log_a)
    mult = jnp.sqrt(1.0 - jnp.exp(2.0 * log_a))
    r = lax.broadcasted_iota(jnp.int32, (xb.shape[0], 1), 0)
    mult = jnp.where(r == first_row, 1.0, mult)
    return a, mult * gate_x * xb


def _lru_fwd_body(xb_ref, wa_ref, ba_ref, wx_ref, bx_ref, c8_ref, h0_ref, a_s, b_s, carry, *, tm):
    ti = pl.program_id(1)

    @pl.when(ti == 0)
    def _():
        carry[...] = jnp.zeros_like(carry)

    first_row = jnp.where(ti == 0, 0, -1)
    a, b = _lru_gates(xb_ref[...], wa_ref, ba_ref, wx_ref, bx_ref, c8_ref, first_row)
    _lru_scan_tile(a, b, a_s, b_s, h0_ref, carry, reverse=False)


def _lru_bwd_body(xb_ref, wa_ref, ba_ref, wx_ref, bx_ref, c8_ref, h0_ref, gb_ref, h_ref, wo_ref, bo_ref,
                  o_ref, a_s, b_s, h1_s, carry, *, tm):
    ti = pl.program_id(1)

    @pl.when(ti == 0)
    def _():
        carry[...] = jnp.zeros_like(carry)

    first_row = jnp.where(ti == 0, tm - 1, -1)
    a, b = _lru_gates(xb_ref[...], wa_ref, ba_ref, wx_ref, bx_ref, c8_ref, first_row)
    _lru_scan_tile(a, b, a_s, b_s, h1_s, carry, reverse=True)
    y = h0_ref[...] + h1_s[...]
    z = (gb_ref[...].astype(F32) * y).astype(BF16)
    o_ref[...] = h_ref[...] + _dot(z, wo_ref[...]) + bo_ref[...]


def _lru_mixer(h, g, prm, seq_len, tm=TOKEN_TILE):
    t, d = h.shape
    bsz = t // seq_len
    nt = seq_len // tm
    gb, xb = _lru_in(h, g, prm['in_w'], prm['in_b'], prm['conv_w'], prm['conv_b'], seq_len, tm)
    r_w = xb.shape[1]
    bw = r_w // LRU_BLOCKS
    gate_specs = lambda dr: [_resident((LRU_BLOCKS, bw, bw)), _resident((1, r_w)),
                             _resident((LRU_BLOCKS, bw, bw)), _resident((1, r_w)), _resident((1, r_w))]
    gate_args = lambda dr: [prm['wa'][dr], _row(prm['ba'][dr]), prm['wx'][dr], _row(prm['bx'][dr]),
                            _row(prm['c8'][dr])]
    fwd = lambda b, i: (b * nt + i, 0)
    bwd = lambda b, i: (b * nt + nt - 1 - i, 0)
    scan_scratch = [pltpu.VMEM((tm, r_w), F32), pltpu.VMEM((tm, r_w), F32)]
    h0 = pl.pallas_call(
        functools.partial(_lru_fwd_body, tm=tm), grid=(bsz, nt),
        in_specs=[pl.BlockSpec((tm, r_w), fwd)] + gate_specs(0),
        out_specs=pl.BlockSpec((tm, r_w), fwd), out_shape=jax.ShapeDtypeStruct((t, r_w), F32),
        scratch_shapes=scan_scratch + [pltpu.VMEM((1, r_w), F32)],
        compiler_params=_params("arbitrary", "arbitrary"), name="lru_fwd")(xb, *gate_args(0))
    return pl.pallas_call(
        functools.partial(_lru_bwd_body, tm=tm), grid=(bsz, nt),
        in_specs=[pl.BlockSpec((tm, r_w), bwd)] + gate_specs(1)
        + [pl.BlockSpec((tm, r_w), bwd), pl.BlockSpec((tm, r_w), bwd), pl.BlockSpec((tm, d), bwd),
           _resident((r_w, d)), _resident((1, d))],
        out_specs=pl.BlockSpec((tm, d), bwd), out_shape=jax.ShapeDtypeStruct((t, d), F32),
        scratch_shapes=scan_scratch + [pltpu.VMEM((tm, r_w), F32), pltpu.VMEM((1, r_w), F32)],
        compiler_params=_params("arbitrary", "arbitrary"), name="lru_bwd")(
            xb, *gate_args(1), h0, gb, h, prm['out_w'], _row(prm['out_b']))


NA_PAIR = 2 * GRID_W
NA_WIN = 5
NA_PB = 4
NA_HG = LANES // (1024 // NA_HEADS)


def _na_qkv_body(h_ref, g_ref, wq_ref, wkt_ref, wv_ref, bq_ref, bk_ref, bv_ref, q_ref, kt_ref, v_ref, *, scale):
    xn = _rms(h_ref[...], g_ref[...]).astype(BF16)
    q_ref[...] = ((_dot(xn, wq_ref[...]) + bq_ref[...]) * scale).astype(BF16)
    kt = lax.dot_general(wkt_ref[...], xn, (((1,), (1,)), ((), ())), preferred_element_type=F32)
    kt_ref[...] = (kt + bk_ref[...]).astype(BF16)
    v_ref[...] = (_dot(xn, wv_ref[...]) + bv_ref[...]).astype(BF16)


def _na_qkv(h, g, wq, wkt, wv, bq, bk, bv, seq_len, scale, tm=TOKEN_TILE):
    t, d = h.shape
    bsz, nt = t // seq_len, seq_len // tm
    tile = pl.BlockSpec((tm, d), lambda i: (i, 0))
    return pl.pallas_call(
        functools.partial(_na_qkv_body, scale=scale), grid=(t // tm,),
        in_specs=[tile, _resident((1, d)), _resident((d, d)), _resident((d, d)), _resident((d, d)),
                  _resident((1, d)), _resident((d, 1)), _resident((1, d))],
        out_specs=[tile, pl.BlockSpec((None, d, tm), lambda i: (i // nt, 0, i % nt)), tile],
        out_shape=[jax.ShapeDtypeStruct((t, d), BF16), jax.ShapeDtypeStruct((bsz, d, seq_len), BF16),
                   jax.ShapeDtypeStruct((t, d), BF16)],
        compiler_params=_params("parallel"), name="na_qkv")(
            h, _row(g), wq, wkt, wv, _row(bq), bk.reshape(d, 1).astype(F32), _row(bv))


def _na_attn_body(q_ref, kp_ref, kc_ref, kn_ref, vp_ref, vc_ref, vn_ref, sb_ref, o_ref, kwin, vwin,
                  *, n_pairs, rows):
    pb = pl.program_id(2)
    blk = NA_PB * NA_PAIR
    for bi, (k_ref, v_ref) in enumerate(((kp_ref, vp_ref), (kc_ref, vc_ref), (kn_ref, vn_ref))):
        for t in range(NA_PB):
            kwin[bi * NA_PB + t] = k_ref[:, t * NA_PAIR:(t + 1) * NA_PAIR]
        vwin[bi * blk:(bi + 1) * blk, :] = v_ref[...]
    dh = LANES // NA_HG
    lane = lax.broadcasted_iota(jnp.int32, (1, LANES), 1)
    q_row_in_pair = lax.broadcasted_iota(jnp.int32, (NA_PAIR, 1), 0) // GRID_W
    k_row_in_win = lax.broadcasted_iota(jnp.int32, (1, NA_WIN * NA_PAIR), 1) // GRID_W

    def pair(i, carry):
        p = pb * NA_PB + i
        kp0 = jnp.clip(p - 2, 0, n_pairs - NA_WIN)
        off = kp0 - (pb - 1) * NA_PB
        m0 = kp0 - p + (NA_WIN - 1)
        qi = q_ref[pl.ds(pl.multiple_of(i * NA_PAIR, NA_PAIR), NA_PAIR), :]
        ksl = jnp.concatenate([kwin[off + jp] for jp in range(NA_WIN)], axis=1)
        vsl = vwin[pl.ds(pl.multiple_of(off * NA_PAIR, NA_PAIR), NA_WIN * NA_PAIR), :]
        q_row = 2 * p + q_row_in_pair
        r_start = jnp.clip(q_row - NA_KH // 2, 0, rows - NA_KH)
        k_row = 2 * kp0 + k_row_in_win
        row_mask = jnp.where((k_row >= r_start) & (k_row < r_start + NA_KH), 0.0, NEG_INF)
        in_head = [(lane // dh) == j for j in range(NA_HG)]
        q_heads = jnp.concatenate([jnp.where(m, qi, jnp.zeros_like(qi)) for m in in_head], axis=0)
        bias = jnp.concatenate(
            [jnp.concatenate([sb_ref[j, m0 + jp] for jp in range(NA_WIN)], axis=1) + row_mask
             for j in range(NA_HG)], axis=0)
        s = _dot(q_heads, ksl) + bias
        e = jnp.exp(s - jnp.max(s, axis=-1, keepdims=True))
        denom = jnp.sum(e, axis=-1, keepdims=True)
        e = e.astype(BF16)
        acc = jnp.zeros((NA_PAIR, LANES), F32)
        for j, m in enumerate(in_head):
            rows_j = slice(j * NA_PAIR, (j + 1) * NA_PAIR)
            acc = acc + jnp.where(m, _dot(e[rows_j], vsl) / denom[rows_j], 0.0)
        o_ref[pl.ds(pl.multiple_of(i * NA_PAIR, NA_PAIR), NA_PAIR), :] = acc.astype(BF16)
        return carry

    lax.fori_loop(0, NA_PB, pair, 0)


def _na_bias_table(rpb):
    n_h, n_dr, n_dc = rpb.shape
    col = np.arange(GRID_W)
    q_start = np.clip(col - NA_KW // 2, 0, GRID_W - NA_KW)
    col_ok = (col[None, :] >= q_start[:, None]) & (col[None, :] < q_start[:, None] + NA_KW)
    dc_idx = col[None, :] - col[:, None] + NA_KW - 1
    onehot = ((np.arange(n_dc)[:, None, None] == dc_idx[None]) & col_ok[None]).reshape(n_dc, -1)
    tab = jnp.dot(rpb.astype(F32).reshape(n_h * n_dr, n_dc), jnp.asarray(onehot, F32),
                  precision=lax.Precision.HIGHEST)
    tab = tab.reshape(n_h, n_dr, GRID_W, GRID_W) + jnp.where(col_ok, 0.0, NEG_INF).astype(F32)
    masked = jnp.full((n_h, GRID_W, GRID_W), NEG_INF, F32)

    def quadrant(dr):
        return tab[:, dr + NA_KH - 1] if abs(dr) <= NA_KH - 1 else masked

    tiles = [jnp.concatenate([jnp.concatenate([quadrant(2 * (m - (NA_WIN - 1)) + c - a) for c in range(2)], axis=2)
                              for a in range(2)], axis=1) for m in range(2 * (NA_WIN - 1) + 1)]
    sb = jnp.stack(tiles, axis=1)
    return sb.reshape(n_h // NA_HG, NA_HG, *sb.shape[1:])


def _na_mixer(h, g, prm, seq_len):
    t, d = h.shape
    bsz = t // seq_len
    rows = seq_len // GRID_W
    n_pairs = rows // 2
    nblk = n_pairs // NA_PB
    blk = NA_PB * NA_PAIR
    q, kt, v = _na_qkv(h, g, prm['wq'], prm['wkt'], prm['wv'], prm['bq'], prm['bk'], prm['bv'], seq_len,
                       (d // NA_HEADS) ** -0.5)
    q = q.reshape(bsz, seq_len, d)
    v = v.reshape(bsz, seq_len, d)
    prev = lambda pb: jnp.maximum(pb - 1, 0)
    nxt = lambda pb: jnp.minimum(pb + 1, nblk - 1)
    k_spec = lambda f: pl.BlockSpec((None, LANES, blk), lambda gi, b, pb: (b, gi, f(pb)))
    v_spec = lambda f: pl.BlockSpec((None, blk, LANES), lambda gi, b, pb: (b, f(pb), gi))
    same = lambda pb: pb
    sb = prm['bias']
    o = pl.pallas_call(
        functools.partial(_na_attn_body, n_pairs=n_pairs, rows=rows),
        grid=(d // LANES, bsz, nblk),
        in_specs=[v_spec(same), k_spec(prev), k_spec(same), k_spec(nxt), v_spec(prev), v_spec(same), v_spec(nxt),
                  pl.BlockSpec((None,) + sb.shape[1:], lambda gi, b, pb: (gi, 0, 0, 0, 0))],
        out_specs=v_spec(same), out_shape=jax.ShapeDtypeStruct((bsz, seq_len, d), BF16),
        scratch_shapes=[pltpu.VMEM((3 * NA_PB, LANES, NA_PAIR), BF16), pltpu.VMEM((3 * blk, LANES), BF16)],
        compiler_params=_params("arbitrary", "arbitrary", "arbitrary"), name="na_attn")(
            q, kt, kt, kt, v, v, v, sb)
    return _proj_res(o.reshape(t, d), prm['out_w'], prm['out_b'], h)


def _hy_in_body(hp_ref, hc_ref, hn_ref, g_ref, w_ref, b_ref, cw_ref, cb_ref, x0_ref, vv_ref, u_s,
                *, tiles_per_seq, tm):
    d = x0_ref.shape[1]
    xn = _normed_with_halo(hp_ref, hc_ref, hn_ref, g_ref)
    left = (HY_SHORT - 1) // 2

    def conv(c0):
        u = _dot(xn, w_ref[:, c0:c0 + d]) + b_ref[:, c0:c0 + d]
        _store_zero_padded(u_s, u, tiles_per_seq, tm)
        y = cb_ref[:, c0:c0 + d]
        for j in range(HY_SHORT):
            y = y + cw_ref[j:j + 1, c0:c0 + d] * u_s[pl.ds(HALO + j - left, tm), :]
        return y

    x0_ref[...] = conv(0).astype(x0_ref.dtype)
    x1 = conv(d)
    vv_ref[...] = conv(2 * d) * x1


def _hy_in(h, g, w, b, cw, cb, seq_len, tm=TOKEN_TILE):
    t, d = h.shape
    tile = pl.BlockSpec((tm, d), lambda i: (i, 0))
    return pl.pallas_call(
        functools.partial(_hy_in_body, tiles_per_seq=seq_len // tm, tm=tm), grid=(t // tm,),
        in_specs=_halo_specs(tm, d, t) + [_resident((1, d)), _resident(w.shape), _resident((1, 3 * d)),
                                          _resident(cw.shape), _resident((1, 3 * d))],
        out_specs=[tile, tile],
        out_shape=[jax.ShapeDtypeStruct((t, d), BF16), jax.ShapeDtypeStruct((t, d), F32)],
        scratch_shapes=[pltpu.VMEM((tm + 2 * HALO, d), F32)],
        compiler_params=_params("parallel"), name="hy_in")(h, h, h, _row(g), w, _row(b), cw, _row(cb))


def _hy_filter_body(z_ref, w1_ref, b1_ref, w2_ref, b2_ref, w3_ref, b3_ref, fr_ref, wo_ref, dl_ref,
                    kf_ref, kb_ref):
    hp = lax.Precision.HIGHEST
    d = kf_ref.shape[1]
    z = z_ref[...]
    fr = fr_ref[...]
    a = jnp.sin(fr * (jnp.dot(z, w1_ref[...], precision=hp, preferred_element_type=F32) + b1_ref[...]))
    a = jnp.sin(fr * (jnp.dot(a, w2_ref[...], precision=hp, preferred_element_type=F32) + b2_ref[...]))
    a = jnp.sin(fr * (jnp.dot(a, w3_ref[...], precision=hp, preferred_element_type=F32) + b3_ref[...]))
    k = jnp.dot(a, wo_ref[...], precision=hp, preferred_element_type=F32)
    window = jnp.exp(-z[:, 0:1] * dl_ref[...])
    kf_ref[...] = k[:, :d] * window
    r = lax.broadcasted_iota(jnp.int32, (z.shape[0], 1), 0)
    kb_ref[...] = jnp.where((r == 0) & (pl.program_id(0) == 0), 0.0, k[:, d:] * window)


def _pad_to(x, shape):
    return jnp.pad(x.astype(F32), [(0, s - n) for s, n in zip(shape, x.shape)])


def _hy_filters(length, prm, d, tl=TOKEN_TILE):
    bands = (HY_EMB - 1) // 2
    t = jnp.linspace(0.0, 1.0, length, dtype=F32)[:, None]
    ang = (2.0 * math.pi / length) * jnp.arange(length, dtype=F32)[:, None]
    fb = jnp.linspace(1e-4, bands - 1, bands, dtype=F32)[None, :]
    z = _pad_to(jnp.concatenate([t, jnp.cos(fb * ang), -jnp.sin(fb * ang)], axis=-1), (length, LANES))
    decay_min = math.log(HY_DECAY_TARGET) / HY_DECAY_PCT_LO
    decay_max = math.log(HY_DECAY_TARGET) / HY_DECAY_PCT_HI
    deltas = jnp.abs(jnp.linspace(decay_min, decay_max, d, dtype=F32))[None, :]
    sq = (LANES, LANES)
    args = [z, _pad_to(prm['f_w1'], sq), _pad_to(prm['f_b1'][None], (1, LANES)),
            _pad_to(prm['f_w2'], sq), _pad_to(prm['f_b2'][None], (1, LANES)),
            _pad_to(prm['f_w3'], sq), _pad_to(prm['f_b3'][None], (1, LANES)),
            _pad_to(prm['f_freq'][None], (1, LANES)), _pad_to(prm['f_out'], (LANES, 2 * d)), deltas]
    tile = pl.BlockSpec((tl, d), lambda i: (i, 0))
    return pl.pallas_call(
        _hy_filter_body, grid=(length // tl,),
        in_specs=[pl.BlockSpec((tl, LANES), lambda i: (i, 0))] + [_resident(a.shape) for a in args[1:]],
        out_specs=[tile, tile],
        out_shape=[jax.ShapeDtypeStruct((length, d), F32)] * 2,
        compiler_params=_params("parallel"), name="hy_filter")(*args)


def _dft_plan(length):
    n = 2 * length
    n1 = min(128, n // 16)
    return n, n1, n // n1, n1 // 2 + 1


@functools.lru_cache(maxsize=None)
def _dft_tables(length):
    n, n1, n2, n1e = _dft_plan(length)
    n1h, sub = n1 // 2, SUBLANES
    groups = n2 // sub
    k1 = np.arange(n1e)
    t_idx = (n2 * np.arange(n1h)[None, None, None, :] + sub * np.arange(groups)[:, None, None, None]
             + np.arange(sub)[None, None, :, None])
    th = 2.0 * np.pi * ((k1[None, :, None, None] * t_idx) % n) / n

    def expand(m):
        return np.einsum('gksi,st->gksit', m, np.eye(sub)).reshape(groups, n1e * sub, n1h * sub)

    fwd1 = np.concatenate([expand(np.cos(th)), expand(-np.sin(th))], axis=1)
    weight = np.where((k1 == 0) | (k1 == n1 // 2), 1.0, 2.0)[None, :, None, None] / n
    inv1 = np.concatenate([expand(weight * np.cos(th)), expand(-weight * np.sin(th))], axis=1).transpose(0, 2, 1)
    inv1 = np.pad(inv1, ((0, 0), (0, 0), (0, -inv1.shape[2] % LANES)))
    idx = np.arange(n2, dtype=np.int64)
    th2 = 2.0 * np.pi * ((idx[:, None] * idx[None, :]) % n2) / n2
    c2, s2 = np.cos(th2), np.sin(th2)
    fwd2 = np.block([[c2, s2], [-s2, c2]])
    inv2 = np.block([[c2, -s2], [s2, c2]])
    return {name: tab.astype(np.float32).astype(BF16)
            for name, tab in (('fwd1', fwd1), ('inv1', inv1), ('fwd2', fwd2), ('inv2', inv2))}


def _dft1_body(x_ref, t_ref, o_ref):
    n1h, sub, d = x_ref.shape
    x = x_ref[...].reshape(n1h * sub, d).astype(BF16)
    o_ref[...] = _dot(t_ref[...], x).reshape(o_ref.shape)


def _dft1(x, length, tabs):
    bx, _, d = x.shape
    n, n1, n2, n1e = _dft_plan(length)
    tab = tabs['fwd1']
    a = pl.pallas_call(
        _dft1_body, grid=(n2 // SUBLANES, bx),
        in_specs=[pl.BlockSpec((None, n1 // 2, None, SUBLANES, d), lambda j, b: (b, 0, j, 0, 0)),
                  pl.BlockSpec((None,) + tab.shape[1:], lambda j, b: (j, 0, 0))],
        out_specs=pl.BlockSpec((None, 2, n1e, None, SUBLANES, d), lambda j, b: (b, 0, 0, j, 0, 0)),
        out_shape=jax.ShapeDtypeStruct((bx, 2, n1e, n2 // SUBLANES, SUBLANES, d), F32),
        compiler_params=_params("parallel", "arbitrary"), name="hy_dft1")(
            x.reshape(bx, n1 // 2, n2 // SUBLANES, SUBLANES, d), tab)
    return a.reshape(bx, 2, n1e, n2, d)


def _filter_spectrum_body(a_ref, f_ref, o_ref):
    n2 = o_ref.shape[1]
    d = o_ref.shape[2]
    xf = _dot(f_ref[...], a_ref[0].reshape(2 * n2, d).astype(BF16))
    xb = _dot(f_ref[...], a_ref[1].reshape(2 * n2, d).astype(BF16))
    o_ref[0] = xf[:n2] + xb[:n2]
    o_ref[1] = xf[n2:] - xb[n2:]


def _filter_spectrum(kf, kb, length, tabs):
    d = kf.shape[1]
    n, n1, n2, n1e = _dft_plan(length)
    a = _dft1(jnp.stack([kf, kb]), length, tabs)
    return pl.pallas_call(
        _filter_spectrum_body, grid=(n1e,),
        in_specs=[pl.BlockSpec((2, 2, None, n2, d), lambda k: (0, 0, k, 0, 0)), _resident(tabs['fwd2'].shape)],
        out_specs=pl.BlockSpec((2, None, n2, d), lambda k: (0, k, 0, 0)),
        out_shape=jax.ShapeDtypeStruct((2, n1e, n2, d), F32),
        compiler_params=_params("parallel"), name="hy_filter_spectrum")(a, tabs['fwd2'])


def _spectral_body(a_ref, k_ref, f_ref, i_ref, o_ref):
    _, n2, d = o_ref.shape
    x = _dot(f_ref[...], a_ref[...].reshape(2 * n2, d).astype(BF16))
    xr, xi = x[:n2], x[n2:]
    kr, ki = k_ref[0], k_ref[1]
    y = jnp.concatenate([xr * kr - xi * ki, xr * ki + xi * kr], axis=0).astype(BF16)
    o_ref[...] = _dot(i_ref[...], y).reshape(2, n2, d)


def _spectral_multiply(a, kc, length, tabs):
    bx = a.shape[0]
    _, n1e, n2, d = kc.shape
    blk = pl.BlockSpec((None, 2, None, n2, d), lambda k, b: (b, 0, k, 0, 0))
    return pl.pallas_call(
        _spectral_body, grid=(n1e, bx),
        in_specs=[blk, pl.BlockSpec((2, None, n2, d), lambda k, b: (0, k, 0, 0)),
                  _resident(tabs['fwd2'].shape), _resident(tabs['inv2'].shape)],
        out_specs=blk, out_shape=jax.ShapeDtypeStruct((bx, 2, n1e, n2, d), F32),
        compiler_params=_params("parallel", "arbitrary"), name="hy_spectral")(
            a, kc, tabs['fwd2'], tabs['inv2'])


def _idft1_body(z_ref, t_ref, o_ref):
    _, n1e, sub, d = z_ref.shape
    rows = 2 * n1e * sub
    z = z_ref[...].reshape(rows, d).astype(BF16)
    z = jnp.concatenate([z, jnp.zeros((t_ref.shape[1] - rows, d), BF16)], axis=0)
    o_ref[...] = _dot(t_ref[...], z).reshape(o_ref.shape)


def _idft1(z, length, tabs):
    bx, _, n1e, n2, d = z.shape
    n, n1, _, _ = _dft_plan(length)
    tab = tabs['inv1']
    y = pl.pallas_call(
        _idft1_body, grid=(n2 // SUBLANES, bx),
        in_specs=[pl.BlockSpec((None, 2, n1e, None, SUBLANES, d), lambda j, b: (b, 0, 0, j, 0, 0)),
                  pl.BlockSpec((None,) + tab.shape[1:], lambda j, b: (j, 0, 0))],
        out_specs=pl.BlockSpec((None, n1 // 2, None, SUBLANES, d), lambda j, b: (b, 0, j, 0, 0)),
        out_shape=jax.ShapeDtypeStruct((bx, n1 // 2, n2 // SUBLANES, SUBLANES, d), F32),
        compiler_params=_params("parallel", "arbitrary"), name="hy_idft1")(
            z.reshape(bx, 2, n1e, n2 // SUBLANES, SUBLANES, d), tab)
    return y.reshape(bx, length, d)


def _hy_out_body(y_ref, vv_ref, x0_ref, h_ref, sk_ref, w_ref, b_ref, o_ref):
    y = (y_ref[...] + vv_ref[...] * sk_ref[...]) * x0_ref[...]
    o_ref[...] = h_ref[...] + _dot(y.astype(BF16), w_ref[...]) + b_ref[...]


def _hy_out(yc, vv, x0, h, skip, w, b, tm=TOKEN_TILE):
    t, d = h.shape
    tile = pl.BlockSpec((tm, d), lambda i: (i, 0))
    return pl.pallas_call(
        _hy_out_body, grid=(t // tm,),
        in_specs=[tile, tile, tile, tile, _resident((1, d)), _resident((d, d)), _resident((1, d))],
        out_specs=tile, out_shape=jax.ShapeDtypeStruct((t, d), F32),
        compiler_params=_params("parallel"), name="hy_out")(yc, vv, x0, h, _row(skip), w, _row(b))


def _hyena_mixer(h, g, prm, seq_len):
    t, d = h.shape
    bsz = t // seq_len
    tabs = _dft_tables(seq_len)
    x0, vv = _hy_in(h, g, prm['in_w'], prm['in_b'], prm['conv_w'], prm['conv_b'], seq_len)
    kf, kb = _hy_filters(seq_len, prm, d)
    kc = _filter_spectrum(kf, kb, seq_len, tabs)
    a = _dft1(vv.reshape(bsz, seq_len, d), seq_len, tabs)
    z = _spectral_multiply(a, kc, seq_len, tabs)
    yc = _idft1(z, seq_len, tabs).reshape(t, d)
    return _hy_out(yc, vv, x0, h, prm['skip'], prm['out_w'], prm['out_b'])


def _trunk(x, p, w, depth):
    bsz, seq_len, d = x.shape
    h = x.reshape(bsz * seq_len, d)
    p = p.reshape(depth, bsz * seq_len, -1)
    for i in range(depth):
        kind, j = i % N_MIXERS, i // N_MIXERS
        h = _ffn(h, w['ln_ffn1'][i], w['ffn1_wg'][i], w['ffn1_wu'][i], w['ffn1_wd'][i])
        mix = (_hyena_mixer, _na_mixer, _lru_mixer)[kind]
        h = mix(h, w['ln_mix'][i], w[('hy', 'na', 'lru')[kind]][j], seq_len)
        h = _ffn(h, w['ln_ffn2'][i], w['ffn2_wg'][i], w['ffn2_wu'][i], w['ffn2_wd'][i],
                 ple=(w['ln_ple'][i], w['ple_gate'][i], p[i], w['ple_proj'][i]),
                 final=w['ln_final'] if i == depth - 1 else None)
    return h.reshape(bsz, seq_len, d)


def kernel(x_prompt, x_sample, p_prompt, p_sample, ln_ffn1, ffn1_wg, ffn1_wu, ffn1_wd, ln_mix, ln_ffn2, ffn2_wg, ffn2_wu, ffn2_wd, ln_ple, ple_gate, ple_proj, ln_final, hy_in_w, hy_in_b, hy_conv_w, hy_conv_b, hy_f_w1, hy_f_b1, hy_f_w2, hy_f_b2, hy_f_w3, hy_f_b3, hy_f_freq, hy_f_out, hy_skip, hy_out_w, hy_out_b, na_qkv_w, na_qkv_b, na_rpb, na_out_w, na_out_b, lru_in_w, lru_in_b, lru_conv_w, lru_conv_b, lru_wa, lru_ba, lru_wx, lru_bx, lru_lambda, lru_out_w, lru_out_b):
    depth = ln_ffn1.shape[0]
    d = x_prompt.shape[-1]
    b16 = lambda a: a.astype(BF16)
    w = {
        'ln_ffn1': ln_ffn1, 'ffn1_wg': b16(ffn1_wg), 'ffn1_wu': b16(ffn1_wu), 'ffn1_wd': b16(ffn1_wd),
        'ln_mix': ln_mix, 'ln_ffn2': ln_ffn2, 'ffn2_wg': b16(ffn2_wg), 'ffn2_wu': b16(ffn2_wu),
        'ffn2_wd': b16(ffn2_wd), 'ln_ple': ln_ple, 'ple_gate': b16(ple_gate), 'ple_proj': b16(ple_proj),
        'ln_final': ln_final,
        'hy': [dict(in_w=b16(hy_in_w[j]), in_b=hy_in_b[j], conv_w=hy_conv_w[j], conv_b=hy_conv_b[j],
                    f_w1=hy_f_w1[j], f_b1=hy_f_b1[j], f_w2=hy_f_w2[j], f_b2=hy_f_b2[j], f_w3=hy_f_w3[j],
                    f_b3=hy_f_b3[j], f_freq=hy_f_freq[j], f_out=hy_f_out[j], skip=hy_skip[j],
                    out_w=b16(hy_out_w[j]), out_b=hy_out_b[j]) for j in range(hy_in_w.shape[0])],
        'na': [dict(wq=b16(na_qkv_w[j][:, :d]), wkt=b16(na_qkv_w[j][:, d:2 * d].T), wv=b16(na_qkv_w[j][:, 2 * d:]),
                    bq=na_qkv_b[j][:d], bk=na_qkv_b[j][d:2 * d], bv=na_qkv_b[j][2 * d:],
                    bias=_na_bias_table(na_rpb[j]), out_w=b16(na_out_w[j]), out_b=na_out_b[j])
               for j in range(na_qkv_w.shape[0])],
        'lru': [dict(in_w=b16(lru_in_w[j]), in_b=lru_in_b[j], conv_w=lru_conv_w[j], conv_b=lru_conv_b[j],
                     wa=b16(lru_wa[j]), ba=lru_ba[j], wx=b16(lru_wx[j]), bx=lru_bx[j],
                     c8=-LRU_C * jax.nn.softplus(-lru_lambda[j].astype(F32)),
                     out_w=b16(lru_out_w[j]), out_b=lru_out_b[j]) for j in range(lru_in_w.shape[0])],
    }
    return _trunk(x_prompt, p_prompt, w, depth), _trunk(x_sample, p_sample, w, depth)
```

```python
import functools
import math

import numpy as np
import jax
import jax.numpy as jnp
from jax import lax
from jax.experimental import pallas as pl
from jax.experimental.pallas import tpu as pltpu

F32 = jnp.float32
BF16 = jnp.bfloat16

RMS_EPS = 1e-6
NEG_INF = -1e30
LANES = 128
SUBLANES = 8
MXU_DIM = 256
VMEM_LIMIT = 56 * 1024 * 1024

GRID_W = 64
NA_HEADS = 32
NA_KH = 8
NA_KW = 16
LRU_BLOCKS = 4
LRU_CONV = 4
LRU_C = 8.0
HY_EMB = 33
HY_SHORT = 3
HY_DECAY_TARGET = 1e-2
HY_DECAY_PCT_HI = 0.3
HY_DECAY_PCT_LO = 1.5
N_MIXERS = 3

TOKEN_TILE = 512
FFN_TILE = 1024
HALO = SUBLANES


def _params(*sem):
    return pltpu.CompilerParams(dimension_semantics=sem, vmem_limit_bytes=VMEM_LIMIT)


def _resident(shape):
    nd = len(shape)
    return pl.BlockSpec(shape, lambda *_: (0,) * nd, pipeline_mode=pl.Buffered(1))


def _rms(x, g):
    ms = jnp.mean(x * x, axis=-1, keepdims=True)
    return x * lax.rsqrt(ms + RMS_EPS) * g


def _dot(a, b):
    return jnp.dot(a, b, preferred_element_type=F32)


def _row(v):
    return v.reshape(1, -1).astype(F32)


def _ffn_body(*refs, ple, final, n_chunks):
    h_ref, g_ref, wg_ref, wu_ref, wd_ref = refs[:5]
    o_ref = refs[-1]
    h = h_ref[...]
    xn = _rms(h, g_ref[...]).astype(BF16)
    n_tiles = pl.cdiv(wg_ref.shape[1], MXU_DIM)
    bounds = [min(-(-n_tiles * c // n_chunks) * MXU_DIM, wg_ref.shape[1]) for c in range(n_chunks + 1)]
    acc = jnp.zeros_like(h)
    for c0, c1 in zip(bounds[:-1], bounds[1:]):
        gt = _dot(xn, wg_ref[:, c0:c1])
        up = _dot(xn, wu_ref[:, c0:c1])
        act = (gt * jax.nn.sigmoid(gt) * up).astype(BF16)
        acc = acc + _dot(act, wd_ref[c0:c1, :])
    h = h + 0.5 * acc
    k = 5
    if ple:
        gp_ref, wpg_ref, p_ref, wpp_ref = refs[k:k + 4]
        k += 4
        gate = jax.nn.sigmoid(_dot(_rms(h, gp_ref[...]).astype(BF16), wpg_ref[...]))
        h = h + gate * _dot(p_ref[...].astype(BF16), wpp_ref[...])
    if final:
        h = _rms(h, refs[k][...])
    o_ref[...] = h


def _ffn(h, g, wg, wu, wd, ple=None, final=None, tm=FFN_TILE):
    t, d = h.shape
    d_ff = wg.shape[1]
    tile = pl.BlockSpec((tm, d), lambda i: (i, 0))
    args = [h, _row(g), wg, wu, wd]
    specs = [tile, _resident((1, d)), _resident((d, d_ff)), _resident((d, d_ff)), _resident((d_ff, d))]
    if ple is not None:
        gp, wpg, p, wpp = ple
        args += [_row(gp), wpg, p, wpp]
        specs += [_resident((1, d)), _resident((d, d)), pl.BlockSpec((tm, p.shape[1]), lambda i: (i, 0)),
                  _resident(wpp.shape)]
    if final is not None:
        args.append(_row(final))
        specs.append(_resident((1, d)))
    return pl.pallas_call(
        functools.partial(_ffn_body, ple=ple is not None, final=final is not None, n_chunks=2),
        grid=(t // tm,), in_specs=specs, out_specs=tile,
        out_shape=jax.ShapeDtypeStruct((t, d), F32),
        compiler_params=_params("parallel"), name="ffn")(*args)


def _halo_specs(tm, d, n_rows):
    per = tm // HALO
    last = n_rows // HALO - 1
    return [pl.BlockSpec((HALO, d), lambda i: (jnp.maximum(i * per - 1, 0), 0)),
            pl.BlockSpec((tm, d), lambda i: (i, 0)),
            pl.BlockSpec((HALO, d), lambda i: (jnp.minimum((i + 1) * per, last), 0))]


def _normed_with_halo(hp_ref, hc_ref, hn_ref, g_ref):
    x = jnp.concatenate([hp_ref[...], hc_ref[...], hn_ref[...]], axis=0)
    return _rms(x, g_ref[...]).astype(BF16)


def _store_zero_padded(u_s, u, tiles_per_seq, tm):
    i = pl.program_id(0) % tiles_per_seq
    u_s[HALO:tm + HALO, :] = u[HALO:tm + HALO]
    u_s[0:HALO, :] = jnp.where(i == 0, 0.0, u[0:HALO])
    u_s[tm + HALO:tm + 2 * HALO, :] = jnp.where(i == tiles_per_seq - 1, 0.0, u[tm + HALO:tm + 2 * HALO])


def _proj_res_body(x_ref, w_ref, b_ref, h_ref, o_ref):
    o_ref[...] = h_ref[...] + _dot(x_ref[...], w_ref[...]) + b_ref[...]


def _proj_res(x, w, b, h, tm=TOKEN_TILE):
    t, d = h.shape
    tile = pl.BlockSpec((tm, d), lambda i: (i, 0))
    return pl.pallas_call(
        _proj_res_body, grid=(t // tm,),
        in_specs=[pl.BlockSpec((tm, x.shape[1]), lambda i: (i, 0)), _resident(w.shape), _resident((1, d)), tile],
        out_specs=tile, out_shape=jax.ShapeDtypeStruct((t, d), F32),
        compiler_params=_params("parallel"), name="proj_res")(x, w, _row(b), h)


def _lru_in_body(hp_ref, hc_ref, hn_ref, g_ref, w_ref, b_ref, cw_ref, cb_ref, gb_ref, xb_ref, u_s,
                 *, tiles_per_seq, tm):
    r_w = gb_ref.shape[1]
    xn = _normed_with_halo(hp_ref, hc_ref, hn_ref, g_ref)
    gb_ref[...] = jax.nn.gelu(_dot(xn[HALO:HALO + tm], w_ref[:, :r_w]) + b_ref[:, :r_w]).astype(BF16)
    u = _dot(xn, w_ref[:, r_w:]) + b_ref[:, r_w:]
    _store_zero_padded(u_s, u, tiles_per_seq, tm)
    left = LRU_CONV // 2
    y = cb_ref[...]
    for j in range(LRU_CONV):
        y = y + cw_ref[j:j + 1, :] * u_s[pl.ds(HALO + j - left, tm), :]
    xb_ref[...] = y


def _lru_in(h, g, w, b, cw, cb, seq_len, tm=TOKEN_TILE):
    t, d = h.shape
    r_w = cw.shape[1]
    tile = pl.BlockSpec((tm, r_w), lambda i: (i, 0))
    return pl.pallas_call(
        functools.partial(_lru_in_body, tiles_per_seq=seq_len // tm, tm=tm),
        grid=(t // tm,),
        in_specs=_halo_specs(tm, d, t) + [_resident((1, d)), _resident(w.shape), _resident((1, 2 * r_w)),
                                          _resident(cw.shape), _resident((1, r_w))],
        out_specs=[tile, tile],
        out_shape=[jax.ShapeDtypeStruct((t, r_w), BF16), jax.ShapeDtypeStruct((t, r_w), F32)],
        scratch_shapes=[pltpu.VMEM((tm + 2 * HALO, r_w), F32)],
        compiler_params=_params("parallel"), name="lru_in")(h, h, h, _row(g), w, _row(b), cw, _row(cb))


def _block_diag(x, w_ref):
    nb, bw, _ = w_ref.shape
    return jnp.concatenate([_dot(x[:, n * bw:(n + 1) * bw], w_ref[n]) for n in range(nb)], axis=1)


def _lru_scan_tile(a, b, a_s, b_s, h_s, carry_ref, reverse):
    tm = a.shape[0]
    row = lax.broadcasted_iota(jnp.int32, (tm, 1), 0) % SUBLANES
    for dist in (1, 2, 4):
        if reverse:
            shift, ok = tm - dist, row < SUBLANES - dist
        else:
            shift, ok = dist, row >= dist
        a_prev = jnp.where(ok, pltpu.roll(a, shift, 0), 1.0)
        b_prev = jnp.where(ok, pltpu.roll(b, shift, 0), 0.0)
        b = a * b_prev + b
        a = a * a_prev
    a_s[...] = a
    b_s[...] = b
    n_groups = tm // SUBLANES
    edge = 0 if reverse else SUBLANES - 1

    def group(k, carry):
        gi = n_groups - 1 - k if reverse else k
        r0 = pl.multiple_of(gi * SUBLANES, SUBLANES)
        hg = a_s[pl.ds(r0, SUBLANES), :] * carry + b_s[pl.ds(r0, SUBLANES), :]
        h_s[pl.ds(r0, SUBLANES), :] = hg
        return hg[edge:edge + 1, :]

    carry_ref[...] = lax.fori_loop(0, n_groups, group, carry_ref[...])


def _lru_gates(xb, wa_ref, ba_ref, wx_ref, bx_ref, c8_ref, first_row):
    xb16 = xb.astype(BF16)
    gate_x = jax.nn.sigmoid(_block_diag(xb16, wx_ref) + bx_ref[...])
    gate_a = jax.nn.sigmoid(_block_diag(xb16, wa_ref) + ba_ref[...])
    log_a = gate_a * c8_ref[...]
    a = jnp.exp(log_a)
    mult = jnp.sqrt(1.0 - jnp.exp(2.0 * log_a))
    r = lax.broadcasted_iota(jnp.int32, (xb.shape[0], 1), 0)
    mult = jnp.where(r == first_row, 1.0, mult)
    return a, mult * gate_x * xb


def _lru_fwd_body(xb_ref, wa_ref, ba_ref, wx_ref, bx_ref, c8_ref, h0_ref, a_s, b_s, carry, *, tm):
    ti = pl.program_id(1)

    @pl.when(ti == 0)
    def _():
        carry[...] = jnp.zeros_like(carry)

    first_row = jnp.where(ti == 0, 0, -1)
    a, b = _lru_gates(xb_ref[...], wa_ref, ba_ref, wx_ref, bx_ref, c8_ref, first_row)
    _lru_scan_tile(a, b, a_s, b_s, h0_ref, carry, reverse=False)


def _lru_bwd_body(xb_ref, wa_ref, ba_ref, wx_ref, bx_ref, c8_ref, h0_ref, gb_ref, h_ref, wo_ref, bo_ref,
                  o_ref, a_s, b_s, h1_s, carry, *, tm):
    ti = pl.program_id(1)

    @pl.when(ti == 0)
    def _():
        carry[...] = jnp.zeros_like(carry)

    first_row = jnp.where(ti == 0, tm - 1, -1)
    a, b = _lru_gates(xb_ref[...], wa_ref, ba_ref, wx_ref, bx_ref, c8_ref, first_row)
    _lru_scan_tile(a, b, a_s, b_s, h1_s, carry, reverse=True)
    y = h0_ref[...] + h1_s[...]
    z = (gb_ref[...].astype(F32) * y).astype(BF16)
    o_ref[...] = h_ref[...] + _dot(z, wo_ref[...]) + bo_ref[...]


def _lru_mixer(h, g, prm, seq_len, tm=TOKEN_TILE):
    t, d = h.shape
    bsz = t // seq_len
    nt = seq_len // tm
    gb, xb = _lru_in(h, g, prm['in_w'], prm['in_b'], prm['conv_w'], prm['conv_b'], seq_len, tm)
    r_w = xb.shape[1]
    bw = r_w // LRU_BLOCKS
    gate_specs = lambda dr: [_resident((LRU_BLOCKS, bw, bw)), _resident((1, r_w)),
                             _resident((LRU_BLOCKS, bw, bw)), _resident((1, r_w)), _resident((1, r_w))]
    gate_args = lambda dr: [prm['wa'][dr], _row(prm['ba'][dr]), prm['wx'][dr], _row(prm['bx'][dr]),
                            _row(prm['c8'][dr])]
    fwd = lambda b, i: (b * nt + i, 0)
    bwd = lambda b, i: (b * nt + nt - 1 - i, 0)
    scan_scratch = [pltpu.VMEM((tm, r_w), F32), pltpu.VMEM((tm, r_w), F32)]
    h0 = pl.pallas_call(
        functools.partial(_lru_fwd_body, tm=tm), grid=(bsz, nt),
        in_specs=[pl.BlockSpec((tm, r_w), fwd)] + gate_specs(0),
        out_specs=pl.BlockSpec((tm, r_w), fwd), out_shape=jax.ShapeDtypeStruct((t, r_w), F32),
        scratch_shapes=scan_scratch + [pltpu.VMEM((1, r_w), F32)],
        compiler_params=_params("arbitrary", "arbitrary"), name="lru_fwd")(xb, *gate_args(0))
    return pl.pallas_call(
        functools.partial(_lru_bwd_body, tm=tm), grid=(bsz, nt),
        in_specs=[pl.BlockSpec((tm, r_w), bwd)] + gate_specs(1)
        + [pl.BlockSpec((tm, r_w), bwd), pl.BlockSpec((tm, r_w), bwd), pl.BlockSpec((tm, d), bwd),
           _resident((r_w, d)), _resident((1, d))],
        out_specs=pl.BlockSpec((tm, d), bwd), out_shape=jax.ShapeDtypeStruct((t, d), F32),
        scratch_shapes=scan_scratch + [pltpu.VMEM((tm, r_w), F32), pltpu.VMEM((1, r_w), F32)],
        compiler_params=_params("arbitrary", "arbitrary"), name="lru_bwd")(
            xb, *gate_args(1), h0, gb, h, prm['out_w'], _row(prm['out_b']))


NA_PAIR = 2 * GRID_W
NA_WIN = 5
NA_PB = 4
NA_HG = LANES // (1024 // NA_HEADS)


def _na_qkv_body(h_ref, g_ref, wq_ref, wkt_ref, wv_ref, bq_ref, bk_ref, bv_ref, q_ref, kt_ref, v_ref, *, scale):
    xn = _rms(h_ref[...], g_ref[...]).astype(BF16)
    q_ref[...] = ((_dot(xn, wq_ref[...]) + bq_ref[...]) * scale).astype(BF16)
    kt = lax.dot_general(wkt_ref[...], xn, (((1,), (1,)), ((), ())), preferred_element_type=F32)
    kt_ref[...] = (kt + bk_ref[...]).astype(BF16)
    v_ref[...] = (_dot(xn, wv_ref[...]) + bv_ref[...]).astype(BF16)


def _na_qkv(h, g, wq, wkt, wv, bq, bk, bv, seq_len, scale, tm=TOKEN_TILE):
    t, d = h.shape
    bsz, nt = t // seq_len, seq_len // tm
    tile = pl.BlockSpec((tm, d), lambda i: (i, 0))
    return pl.pallas_call(
        functools.partial(_na_qkv_body, scale=scale), grid=(t // tm,),
        in_specs=[tile, _resident((1, d)), _resident((d, d)), _resident((d, d)), _resident((d, d)),
                  _resident((1, d)), _resident((d, 1)), _resident((1, d))],
        out_specs=[tile, pl.BlockSpec((None, d, tm), lambda i: (i // nt, 0, i % nt)), tile],
        out_shape=[jax.ShapeDtypeStruct((t, d), BF16), jax.ShapeDtypeStruct((bsz, d, seq_len), BF16),
                   jax.ShapeDtypeStruct((t, d), BF16)],
        compiler_params=_params("parallel"), name="na_qkv")(
            h, _row(g), wq, wkt, wv, _row(bq), bk.reshape(d, 1).astype(F32), _row(bv))


def _na_attn_body(q_ref, kp_ref, kc_ref, kn_ref, vp_ref, vc_ref, vn_ref, sb_ref, o_ref, kwin, vwin,
                  *, n_pairs, rows):
    pb = pl.program_id(2)
    blk = NA_PB * NA_PAIR
    for bi, (k_ref, v_ref) in enumerate(((kp_ref, vp_ref), (kc_ref, vc_ref), (kn_ref, vn_ref))):
        for t in range(NA_PB):
            kwin[bi * NA_PB + t] = k_ref[:, t * NA_PAIR:(t + 1) * NA_PAIR]
        vwin[bi * blk:(bi + 1) * blk, :] = v_ref[...]
    dh = LANES // NA_HG
    pad = 2 * SUBLANES
    lane = lax.broadcasted_iota(jnp.int32, (1, LANES), 1)
    in_head = [(lane // dh) == j for j in range(NA_HG)]
    q_row_onehot = (lax.broadcasted_iota(jnp.int32, (NA_PAIR, pad), 0) // GRID_W
                    == lax.broadcasted_iota(jnp.int32, (NA_PAIR, pad), 1)).astype(BF16)
    mask_row = lax.broadcasted_iota(jnp.int32, (pad, 1), 0)
    k_row_in_win = lax.broadcasted_iota(jnp.int32, (1, NA_WIN * NA_PAIR), 1) // GRID_W

    def pair(i, carry):
        p = pb * NA_PB + i
        kp0 = jnp.clip(p - 2, 0, n_pairs - NA_WIN)
        off = kp0 - (pb - 1) * NA_PB
        m0 = kp0 - p + (NA_WIN - 1)
        qi = q_ref[pl.ds(pl.multiple_of(i * NA_PAIR, NA_PAIR), NA_PAIR), :]
        ksl = jnp.concatenate([kwin[off + jp] for jp in range(NA_WIN)], axis=1)
        vsl = vwin[pl.ds(pl.multiple_of(off * NA_PAIR, NA_PAIR), NA_WIN * NA_PAIR), :]
        r_start = jnp.clip(2 * p + mask_row - NA_KH // 2, 0, rows - NA_KH)
        k_row = 2 * kp0 + k_row_in_win
        in_window = (k_row >= r_start) & (k_row < r_start + NA_KH)
        row_mask = jnp.where(in_window | (mask_row >= 2), 0.0, NEG_INF).astype(BF16)
        q_heads = jnp.concatenate(
            [jnp.concatenate([jnp.where(m, qi, jnp.zeros_like(qi)), q_row_onehot], axis=1) for m in in_head], axis=0)
        bias = jnp.concatenate(
            [jnp.concatenate([sb_ref[j, m0 + jp] for jp in range(NA_WIN)], axis=1) for j in range(NA_HG)], axis=0)
        s = _dot(q_heads, jnp.concatenate([ksl, row_mask], axis=0)) + bias
        e = jnp.exp(s - jnp.max(s, axis=-1, keepdims=True))
        denom = jnp.sum(e, axis=-1, keepdims=True)
        e = e.astype(BF16)
        acc = jnp.zeros((NA_PAIR, LANES), F32)
        for j, m in enumerate(in_head):
            rows_j = slice(j * NA_PAIR, (j + 1) * NA_PAIR)
            acc = acc + jnp.where(m, _dot(e[rows_j], vsl) / denom[rows_j], 0.0)
        o_ref[pl.ds(pl.multiple_of(i * NA_PAIR, NA_PAIR), NA_PAIR), :] = acc.astype(BF16)
        return carry

    lax.fori_loop(0, NA_PB, pair, 0)


def _na_bias_table(rpb):
    n_h, n_dr, n_dc = rpb.shape
    col = np.arange(GRID_W)
    q_start = np.clip(col - NA_KW // 2, 0, GRID_W - NA_KW)
    col_ok = (col[None, :] >= q_start[:, None]) & (col[None, :] < q_start[:, None] + NA_KW)
    dc_idx = col[None, :] - col[:, None] + NA_KW - 1
    onehot = ((np.arange(n_dc)[:, None, None] == dc_idx[None]) & col_ok[None]).reshape(n_dc, -1)
    tab = jnp.dot(rpb.astype(F32).reshape(n_h * n_dr, n_dc), jnp.asarray(onehot, F32),
                  precision=lax.Precision.HIGHEST)
    tab = tab.reshape(n_h, n_dr, GRID_W, GRID_W) + jnp.where(col_ok, 0.0, NEG_INF).astype(F32)
    masked = jnp.full((n_h, GRID_W, GRID_W), NEG_INF, F32)

    def quadrant(dr):
        return tab[:, dr + NA_KH - 1] if abs(dr) <= NA_KH - 1 else masked

    tiles = [jnp.concatenate([jnp.concatenate([quadrant(2 * (m - (NA_WIN - 1)) + c - a) for c in range(2)], axis=2)
                              for a in range(2)], axis=1) for m in range(2 * (NA_WIN - 1) + 1)]
    sb = jnp.stack(tiles, axis=1)
    return sb.reshape(n_h // NA_HG, NA_HG, *sb.shape[1:])


def _na_mixer(h, g, prm, seq_len):
    t, d = h.shape
    bsz = t // seq_len
    rows = seq_len // GRID_W
    n_pairs = rows // 2
    nblk = n_pairs // NA_PB
    blk = NA_PB * NA_PAIR
    q, kt, v = _na_qkv(h, g, prm['wq'], prm['wkt'], prm['wv'], prm['bq'], prm['bk'], prm['bv'], seq_len,
                       (d // NA_HEADS) ** -0.5)
    q = q.reshape(bsz, seq_len, d)
    v = v.reshape(bsz, seq_len, d)
    prev = lambda pb: jnp.maximum(pb - 1, 0)
    nxt = lambda pb: jnp.minimum(pb + 1, nblk - 1)
    k_spec = lambda f: pl.BlockSpec((None, LANES, blk), lambda gi, b, pb: (b, gi, f(pb)))
    v_spec = lambda f: pl.BlockSpec((None, blk, LANES), lambda gi, b, pb: (b, f(pb), gi))
    same = lambda pb: pb
    sb = prm['bias']
    o = pl.pallas_call(
        functools.partial(_na_attn_body, n_pairs=n_pairs, rows=rows),
        grid=(d // LANES, bsz, nblk),
        in_specs=[v_spec(same), k_spec(prev), k_spec(same), k_spec(nxt), v_spec(prev), v_spec(same), v_spec(nxt),
                  pl.BlockSpec((None,) + sb.shape[1:], lambda gi, b, pb: (gi, 0, 0, 0, 0))],
        out_specs=v_spec(same), out_shape=jax.ShapeDtypeStruct((bsz, seq_len, d), BF16),
        scratch_shapes=[pltpu.VMEM((3 * NA_PB, LANES, NA_PAIR), BF16), pltpu.VMEM((3 * blk, LANES), BF16)],
        compiler_params=_params("arbitrary", "arbitrary", "arbitrary"), name="na_attn")(
            q, kt, kt, kt, v, v, v, sb)
    return _proj_res(o.reshape(t, d), prm['out_w'], prm['out_b'], h)


def _hy_in_body(hp_ref, hc_ref, hn_ref, g_ref, w_ref, b_ref, cw_ref, cb_ref, x0_ref, vv_ref, u_s,
                *, tiles_per_seq, tm):
    d = x0_ref.shape[1]
    xn = _normed_with_halo(hp_ref, hc_ref, hn_ref, g_ref)
    left = (HY_SHORT - 1) // 2

    def conv(c0):
        u = _dot(xn, w_ref[:, c0:c0 + d]) + b_ref[:, c0:c0 + d]
        _store_zero_padded(u_s, u, tiles_per_seq, tm)
        y = cb_ref[:, c0:c0 + d]
        for j in range(HY_SHORT):
            y = y + cw_ref[j:j + 1, c0:c0 + d] * u_s[pl.ds(HALO + j - left, tm), :]
        return y

    x0_ref[...] = conv(0).astype(x0_ref.dtype)
    x1 = conv(d)
    vv_ref[...] = conv(2 * d) * x1


def _hy_in(h, g, w, b, cw, cb, seq_len, tm=TOKEN_TILE):
    t, d = h.shape
    tile = pl.BlockSpec((tm, d), lambda i: (i, 0))
    return pl.pallas_call(
        functools.partial(_hy_in_body, tiles_per_seq=seq_len // tm, tm=tm), grid=(t // tm,),
        in_specs=_halo_specs(tm, d, t) + [_resident((1, d)), _resident(w.shape), _resident((1, 3 * d)),
                                          _resident(cw.shape), _resident((1, 3 * d))],
        out_specs=[tile, tile],
        out_shape=[jax.ShapeDtypeStruct((t, d), BF16), jax.ShapeDtypeStruct((t, d), F32)],
        scratch_shapes=[pltpu.VMEM((tm + 2 * HALO, d), F32)],
        compiler_params=_params("parallel"), name="hy_in")(h, h, h, _row(g), w, _row(b), cw, _row(cb))


def _hy_filter_body(z_ref, w1_ref, b1_ref, w2_ref, b2_ref, w3_ref, b3_ref, fr_ref, wo_ref, dl_ref,
                    kf_ref, kb_ref):
    hp = lax.Precision.HIGHEST
    d = kf_ref.shape[1]
    z = z_ref[...]
    fr = fr_ref[...]
    a = jnp.sin(fr * (jnp.dot(z, w1_ref[...], precision=hp, preferred_element_type=F32) + b1_ref[...]))
    a = jnp.sin(fr * (jnp.dot(a, w2_ref[...], precision=hp, preferred_element_type=F32) + b2_ref[...]))
    a = jnp.sin(fr * (jnp.dot(a, w3_ref[...], precision=hp, preferred_element_type=F32) + b3_ref[...]))
    k = _dot(a.astype(BF16), wo_ref[...])
    window = jnp.exp(-z[:, 0:1] * dl_ref[...])
    kf_ref[...] = k[:, :d] * window
    r = lax.broadcasted_iota(jnp.int32, (z.shape[0], 1), 0)
    kb_ref[...] = jnp.where((r == 0) & (pl.program_id(0) == 0), 0.0, k[:, d:] * window)


def _pad_to(x, shape):
    return jnp.pad(x.astype(F32), [(0, s - n) for s, n in zip(shape, x.shape)])


@functools.lru_cache(maxsize=None)
def _hy_positions(length, d):
    bands = (HY_EMB - 1) // 2
    t = np.linspace(0.0, 1.0, length)[:, None]
    ang = (2.0 * math.pi / length) * np.arange(length)[:, None]
    fb = np.linspace(1e-4, bands - 1, bands)[None, :]
    z = np.concatenate([t, np.cos(fb * ang), -np.sin(fb * ang)], axis=-1)
    z = np.pad(z, ((0, 0), (0, LANES - z.shape[1]))).astype(np.float32)
    decay_min = math.log(HY_DECAY_TARGET) / HY_DECAY_PCT_LO
    decay_max = math.log(HY_DECAY_TARGET) / HY_DECAY_PCT_HI
    deltas = np.abs(np.linspace(decay_min, decay_max, d))[None, :].astype(np.float32)
    return z, deltas


def _hy_filters(length, prm, d, tl=TOKEN_TILE):
    z, deltas = _hy_positions(length, d)
    sq = (LANES, LANES)
    args = [z, _pad_to(prm['f_w1'], sq), _pad_to(prm['f_b1'][None], (1, LANES)),
            _pad_to(prm['f_w2'], sq), _pad_to(prm['f_b2'][None], (1, LANES)),
            _pad_to(prm['f_w3'], sq), _pad_to(prm['f_b3'][None], (1, LANES)),
            _pad_to(prm['f_freq'][None], (1, LANES)), _pad_to(prm['f_out'], (LANES, 2 * d)).astype(BF16), deltas]
    tile = pl.BlockSpec((tl, d), lambda i: (i, 0))
    return pl.pallas_call(
        _hy_filter_body, grid=(length // tl,),
        in_specs=[pl.BlockSpec((tl, LANES), lambda i: (i, 0))] + [_resident(a.shape) for a in args[1:]],
        out_specs=[tile, tile],
        out_shape=[jax.ShapeDtypeStruct((length, d), F32)] * 2,
        compiler_params=_params("parallel"), name="hy_filter")(*args)


def _dft_plan(length):
    n = 2 * length
    n1 = min(128, n // 16)
    return n, n1, n // n1, n1 // 2 + 1


@functools.lru_cache(maxsize=None)
def _dft_tables(length):
    n, n1, n2, n1e = _dft_plan(length)
    n1h, sub = n1 // 2, SUBLANES
    groups = n2 // sub
    k1 = np.arange(n1e)
    t_idx = (n2 * np.arange(n1h)[None, None, None, :] + sub * np.arange(groups)[:, None, None, None]
             + np.arange(sub)[None, None, :, None])
    th = 2.0 * np.pi * ((k1[None, :, None, None] * t_idx) % n) / n

    def expand(m):
        return np.einsum('gksi,st->gksit', m, np.eye(sub)).reshape(groups, n1e * sub, n1h * sub)

    fwd1 = np.concatenate([expand(np.cos(th)), expand(-np.sin(th))], axis=1)
    weight = np.where((k1 == 0) | (k1 == n1 // 2), 1.0, 2.0)[None, :, None, None] / n
    inv1 = np.concatenate([expand(weight * np.cos(th)), expand(-weight * np.sin(th))], axis=1).transpose(0, 2, 1)
    inv1 = np.pad(inv1, ((0, 0), (0, 0), (0, -inv1.shape[2] % LANES)))
    idx = np.arange(n2, dtype=np.int64)
    th2 = 2.0 * np.pi * ((idx[:, None] * idx[None, :]) % n2) / n2
    c2, s2 = np.cos(th2), np.sin(th2)
    fwd2 = np.block([[c2, s2], [-s2, c2]])
    inv2 = np.block([[c2, -s2], [s2, c2]])
    return {name: tab.astype(np.float32).astype(BF16)
            for name, tab in (('fwd1', fwd1), ('inv1', inv1), ('fwd2', fwd2), ('inv2', inv2))}


def _dft1_body(x_ref, t_ref, o_ref):
    n1h, sub, d = x_ref.shape
    x = x_ref[...].reshape(n1h * sub, d).astype(BF16)
    o_ref[...] = _dot(t_ref[...], x).reshape(o_ref.shape)


def _dft1(x, length, tabs):
    bx, _, d = x.shape
    n, n1, n2, n1e = _dft_plan(length)
    tab = tabs['fwd1']
    a = pl.pallas_call(
        _dft1_body, grid=(n2 // SUBLANES, bx),
        in_specs=[pl.BlockSpec((None, n1 // 2, None, SUBLANES, d), lambda j, b: (b, 0, j, 0, 0)),
                  pl.BlockSpec((None,) + tab.shape[1:], lambda j, b: (j, 0, 0))],
        out_specs=pl.BlockSpec((None, 2, n1e, None, SUBLANES, d), lambda j, b: (b, 0, 0, j, 0, 0)),
        out_shape=jax.ShapeDtypeStruct((bx, 2, n1e, n2 // SUBLANES, SUBLANES, d), F32),
        compiler_params=_params("parallel", "arbitrary"), name="hy_dft1")(
            x.reshape(bx, n1 // 2, n2 // SUBLANES, SUBLANES, d), tab)
    return a.reshape(bx, 2, n1e, n2, d)


def _filter_spectrum_body(a_ref, f_ref, o_ref):
    n2 = o_ref.shape[1]
    d = o_ref.shape[2]
    xf = _dot(f_ref[...], a_ref[0].reshape(2 * n2, d).astype(BF16))
    xb = _dot(f_ref[...], a_ref[1].reshape(2 * n2, d).astype(BF16))
    o_ref[0] = xf[:n2] + xb[:n2]
    o_ref[1] = xf[n2:] - xb[n2:]


def _filter_spectrum(kf, kb, length, tabs):
    d = kf.shape[1]
    n, n1, n2, n1e = _dft_plan(length)
    a = _dft1(jnp.stack([kf, kb]), length, tabs)
    return pl.pallas_call(
        _filter_spectrum_body, grid=(n1e,),
        in_specs=[pl.BlockSpec((2, 2, None, n2, d), lambda k: (0, 0, k, 0, 0)), _resident(tabs['fwd2'].shape)],
        out_specs=pl.BlockSpec((2, None, n2, d), lambda k: (0, k, 0, 0)),
        out_shape=jax.ShapeDtypeStruct((2, n1e, n2, d), F32),
        compiler_params=_params("parallel"), name="hy_filter_spectrum")(a, tabs['fwd2'])


def _spectral_body(a_ref, k_ref, f_ref, i_ref, o_ref):
    bb, _, n2, d = o_ref.shape
    a = jnp.concatenate([a_ref[b].reshape(2 * n2, d) for b in range(bb)], axis=1).astype(BF16)
    x = _dot(f_ref[...], a)
    xr, xi = x[:n2], x[n2:]
    kr = jnp.concatenate([k_ref[0]] * bb, axis=1)
    ki = jnp.concatenate([k_ref[1]] * bb, axis=1)
    y = jnp.concatenate([xr * kr - xi * ki, xr * ki + xi * kr], axis=0).astype(BF16)
    z = _dot(i_ref[...], y)
    for b in range(bb):
        o_ref[b] = z[:, b * d:(b + 1) * d].reshape(2, n2, d)


def _spectral_multiply(a, kc, length, tabs):
    bx = a.shape[0]
    _, n1e, n2, d = kc.shape
    bb = max(b for b in range(1, bx + 1) if bx % b == 0 and b * n2 <= 2 * MXU_DIM)
    blk = pl.BlockSpec((bb, 2, None, n2, d), lambda k, b: (b, 0, k, 0, 0))
    return pl.pallas_call(
        _spectral_body, grid=(n1e, bx // bb),
        in_specs=[blk, pl.BlockSpec((2, None, n2, d), lambda k, b: (0, k, 0, 0)),
                  _resident(tabs['fwd2'].shape), _resident(tabs['inv2'].shape)],
        out_specs=blk, out_shape=jax.ShapeDtypeStruct((bx, 2, n1e, n2, d), F32),
        compiler_params=_params("parallel", "arbitrary"), name="hy_spectral")(
            a, kc, tabs['fwd2'], tabs['inv2'])


def _idft1_body(z_ref, t_ref, o_ref):
    _, n1e, sub, d = z_ref.shape
    rows = 2 * n1e * sub
    z = z_ref[...].reshape(rows, d).astype(BF16)
    z = jnp.concatenate([z, jnp.zeros((t_ref.shape[1] - rows, d), BF16)], axis=0)
    o_ref[...] = _dot(t_ref[...], z).reshape(o_ref.shape)


def _idft1(z, length, tabs):
    bx, _, n1e, n2, d = z.shape
    n, n1, _, _ = _dft_plan(length)
    tab = tabs['inv1']
    y = pl.pallas_call(
        _idft1_body, grid=(n2 // SUBLANES, bx),
        in_specs=[pl.BlockSpec((None, 2, n1e, None, SUBLANES, d), lambda j, b: (b, 0, 0, j, 0, 0)),
                  pl.BlockSpec((None,) + tab.shape[1:], lambda j, b: (j, 0, 0))],
        out_specs=pl.BlockSpec((None, n1 // 2, None, SUBLANES, d), lambda j, b: (b, 0, j, 0, 0)),
        out_shape=jax.ShapeDtypeStruct((bx, n1 // 2, n2 // SUBLANES, SUBLANES, d), F32),
        compiler_params=_params("parallel", "arbitrary"), name="hy_idft1")(
            z.reshape(bx, 2, n1e, n2 // SUBLANES, SUBLANES, d), tab)
    return y.reshape(bx, length, d)


def _hy_out_body(y_ref, vv_ref, x0_ref, h_ref, sk_ref, w_ref, b_ref, o_ref):
    y = (y_ref[...] + vv_ref[...] * sk_ref[...]) * x0_ref[...]
    o_ref[...] = h_ref[...] + _dot(y.astype(BF16), w_ref[...]) + b_ref[...]


def _hy_out(yc, vv, x0, h, skip, w, b, tm=TOKEN_TILE):
    t, d = h.shape
    tile = pl.BlockSpec((tm, d), lambda i: (i, 0))
    return pl.pallas_call(
        _hy_out_body, grid=(t // tm,),
        in_specs=[tile, tile, tile, tile, _resident((1, d)), _resident((d, d)), _resident((1, d))],
        out_specs=tile, out_shape=jax.ShapeDtypeStruct((t, d), F32),
        compiler_params=_params("parallel"), name="hy_out")(yc, vv, x0, h, _row(skip), w, _row(b))


def _hyena_mixer(h, g, prm, seq_len):
    t, d = h.shape
    bsz = t // seq_len
    tabs = _dft_tables(seq_len)
    x0, vv = _hy_in(h, g, prm['in_w'], prm['in_b'], prm['conv_w'], prm['conv_b'], seq_len)
    kf, kb = _hy_filters(seq_len, prm, d)
    kc = _filter_spectrum(kf, kb, seq_len, tabs)
    a = _dft1(vv.reshape(bsz, seq_len, d), seq_len, tabs)
    z = _spectral_multiply(a, kc, seq_len, tabs)
    yc = _idft1(z, seq_len, tabs).reshape(t, d)
    return _hy_out(yc, vv, x0, h, prm['skip'], prm['out_w'], prm['out_b'])


def _trunk(x, p, w, depth):
    bsz, seq_len, d = x.shape
    h = x.reshape(bsz * seq_len, d)
    p = p.reshape(depth, bsz * seq_len, -1)
    for i in range(depth):
        kind, j = i % N_MIXERS, i // N_MIXERS
        h = _ffn(h, w['ln_ffn1'][i], w['ffn1_wg'][i], w['ffn1_wu'][i], w['ffn1_wd'][i])
        mix = (_hyena_mixer, _na_mixer, _lru_mixer)[kind]
        h = mix(h, w['ln_mix'][i], w[('hy', 'na', 'lru')[kind]][j], seq_len)
        h = _ffn(h, w['ln_ffn2'][i], w['ffn2_wg'][i], w['ffn2_wu'][i], w['ffn2_wd'][i],
                 ple=(w['ln_ple'][i], w['ple_gate'][i], p[i], w['ple_proj'][i]),
                 final=w['ln_final'] if i == depth - 1 else None)
    return h.reshape(bsz, seq_len, d)


def kernel(x_prompt, x_sample, p_prompt, p_sample, ln_ffn1, ffn1_wg, ffn1_wu, ffn1_wd, ln_mix, ln_ffn2, ffn2_wg, ffn2_wu, ffn2_wd, ln_ple, ple_gate, ple_proj, ln_final, hy_in_w, hy_in_b, hy_conv_w, hy_conv_b, hy_f_w1, hy_f_b1, hy_f_w2, hy_f_b2, hy_f_w3, hy_f_b3, hy_f_freq, hy_f_out, hy_skip, hy_out_w, hy_out_b, na_qkv_w, na_qkv_b, na_rpb, na_out_w, na_out_b, lru_in_w, lru_in_b, lru_conv_w, lru_conv_b, lru_wa, lru_ba, lru_wx, lru_bx, lru_lambda, lru_out_w, lru_out_b):
    depth = ln_ffn1.shape[0]
    d = x_prompt.shape[-1]
    b16 = lambda a: a.astype(BF16)
    w = {
        'ln_ffn1': ln_ffn1, 'ffn1_wg': b16(ffn1_wg), 'ffn1_wu': b16(ffn1_wu), 'ffn1_wd': b16(ffn1_wd),
        'ln_mix': ln_mix, 'ln_ffn2': ln_ffn2, 'ffn2_wg': b16(ffn2_wg), 'ffn2_wu': b16(ffn2_wu),
        'ffn2_wd': b16(ffn2_wd), 'ln_ple': ln_ple, 'ple_gate': b16(ple_gate), 'ple_proj': b16(ple_proj),
        'ln_final': ln_final,
        'hy': [dict(in_w=b16(hy_in_w[j]), in_b=hy_in_b[j], conv_w=hy_conv_w[j], conv_b=hy_conv_b[j],
                    f_w1=hy_f_w1[j], f_b1=hy_f_b1[j], f_w2=hy_f_w2[j], f_b2=hy_f_b2[j], f_w3=hy_f_w3[j],
                    f_b3=hy_f_b3[j], f_freq=hy_f_freq[j], f_out=hy_f_out[j], skip=hy_skip[j],
                    out_w=b16(hy_out_w[j]), out_b=hy_out_b[j]) for j in range(hy_in_w.shape[0])],
        'na': [dict(wq=b16(na_qkv_w[j][:, :d]), wkt=b16(na_qkv_w[j][:, d:2 * d].T), wv=b16(na_qkv_w[j][:, 2 * d:]),
                    bq=na_qkv_b[j][:d], bk=na_qkv_b[j][d:2 * d], bv=na_qkv_b[j][2 * d:],
                    bias=_na_bias_table(na_rpb[j]), out_w=b16(na_out_w[j]), out_b=na_out_b[j])
               for j in range(na_qkv_w.shape[0])],
        'lru': [dict(in_w=b16(lru_in_w[j]), in_b=lru_in_b[j], conv_w=lru_conv_w[j], conv_b=lru_conv_b[j],
                     wa=b16(lru_wa[j]), ba=lru_ba[j], wx=b16(lru_wx[j]), bx=lru_bx[j],
                     c8=-LRU_C * jax.nn.softplus(-lru_lambda[j].astype(F32)),
                     out_w=b16(lru_out_w[j]), out_b=lru_out_b[j]) for j in range(lru_in_w.shape[0])],
    }
    return _trunk(x_prompt, p_prompt, w, depth), _trunk(x_sample, p_sample, w, depth)
```

```python
import functools
import math

import numpy as np
import jax
import jax.numpy as jnp
from jax import lax
from jax.experimental import pallas as pl
from jax.experimental.pallas import tpu as pltpu

F32 = jnp.float32
BF16 = jnp.bfloat16

RMS_EPS = 1e-6
NEG_INF = -1e30
LANES = 128
SUBLANES = 8
MXU_DIM = 256
VMEM_LIMIT = 56 * 1024 * 1024

GRID_W = 64
NA_HEADS = 32
NA_KH = 8
NA_KW = 16
LRU_BLOCKS = 4
LRU_CONV = 4
LRU_C = 8.0
HY_EMB = 33
HY_SHORT = 3
HY_DECAY_TARGET = 1e-2
HY_DECAY_PCT_HI = 0.3
HY_DECAY_PCT_LO = 1.5
N_MIXERS = 3

TOKEN_TILE = 512
FFN_TILE = 1024
HALO = SUBLANES


def _params(*sem):
    return pltpu.CompilerParams(dimension_semantics=sem, vmem_limit_bytes=VMEM_LIMIT)


def _resident(shape):
    nd = len(shape)
    return pl.BlockSpec(shape, lambda *_: (0,) * nd, pipeline_mode=pl.Buffered(1))


def _rms(x, g):
    ms = jnp.mean(x * x, axis=-1, keepdims=True)
    return x * lax.rsqrt(ms + RMS_EPS) * g


def _dot(a, b):
    return jnp.dot(a, b, preferred_element_type=F32)


def _row(v):
    return v.reshape(1, -1).astype(F32)


def _ffn_body(*refs, ple, final, n_chunks):
    h_ref, g_ref, wg_ref, wu_ref, wd_ref = refs[:5]
    o_ref = refs[-1]
    h = h_ref[...]
    xn = _rms(h, g_ref[...]).astype(BF16)
    n_tiles = pl.cdiv(wg_ref.shape[1], MXU_DIM)
    bounds = [min(-(-n_tiles * c // n_chunks) * MXU_DIM, wg_ref.shape[1]) for c in range(n_chunks + 1)]
    acc = jnp.zeros_like(h)
    for c0, c1 in zip(bounds[:-1], bounds[1:]):
        gt = _dot(xn, wg_ref[:, c0:c1])
        up = _dot(xn, wu_ref[:, c0:c1])
        act = (gt * jax.nn.sigmoid(gt) * up).astype(BF16)
        acc = acc + _dot(act, wd_ref[c0:c1, :])
    h = h + 0.5 * acc
    k = 5
    if ple:
        gp_ref, wpg_ref, p_ref, wpp_ref = refs[k:k + 4]
        k += 4
        gate = jax.nn.sigmoid(_dot(_rms(h, gp_ref[...]).astype(BF16), wpg_ref[...]))
        h = h + gate * _dot(p_ref[...].astype(BF16), wpp_ref[...])
    if final:
        h = _rms(h, refs[k][...])
    o_ref[...] = h


def _ffn(h, g, wg, wu, wd, ple=None, final=None, tm=FFN_TILE):
    t, d = h.shape
    d_ff = wg.shape[1]
    tile = pl.BlockSpec((tm, d), lambda i: (i, 0))
    args = [h, _row(g), wg, wu, wd]
    specs = [tile, _resident((1, d)), _resident((d, d_ff)), _resident((d, d_ff)), _resident((d_ff, d))]
    if ple is not None:
        gp, wpg, p, layer, wpp = ple
        args += [_row(gp), wpg, p, wpp]
        specs += [_resident((1, d)), _resident((d, d)),
                  pl.BlockSpec((None, tm, p.shape[2]), lambda i: (layer, i, 0)), _resident(wpp.shape)]
    if final is not None:
        args.append(_row(final))
        specs.append(_resident((1, d)))
    return pl.pallas_call(
        functools.partial(_ffn_body, ple=ple is not None, final=final is not None, n_chunks=2),
        grid=(t // tm,), in_specs=specs, out_specs=tile,
        out_shape=jax.ShapeDtypeStruct((t, d), F32),
        compiler_params=_params("parallel"), name="ffn")(*args)


def _halo_specs(tm, d, n_rows):
    per = tm // HALO
    last = n_rows // HALO - 1
    return [pl.BlockSpec((HALO, d), lambda i: (jnp.maximum(i * per - 1, 0), 0)),
            pl.BlockSpec((tm, d), lambda i: (i, 0)),
            pl.BlockSpec((HALO, d), lambda i: (jnp.minimum((i + 1) * per, last), 0))]


def _normed_with_halo(hp_ref, hc_ref, hn_ref, g_ref):
    x = jnp.concatenate([hp_ref[...], hc_ref[...], hn_ref[...]], axis=0)
    return _rms(x, g_ref[...]).astype(BF16)


def _store_zero_padded(u_s, u, tiles_per_seq, tm):
    i = pl.program_id(0) % tiles_per_seq
    u_s[HALO:tm + HALO, :] = u[HALO:tm + HALO]
    u_s[0:HALO, :] = jnp.where(i == 0, 0.0, u[0:HALO])
    u_s[tm + HALO:tm + 2 * HALO, :] = jnp.where(i == tiles_per_seq - 1, 0.0, u[tm + HALO:tm + 2 * HALO])


def _proj_res_body(x_ref, w_ref, b_ref, h_ref, o_ref):
    o_ref[...] = h_ref[...] + _dot(x_ref[...], w_ref[...]) + b_ref[...]


def _proj_res(x, w, b, h, tm=TOKEN_TILE):
    t, d = h.shape
    tile = pl.BlockSpec((tm, d), lambda i: (i, 0))
    return pl.pallas_call(
        _proj_res_body, grid=(t // tm,),
        in_specs=[pl.BlockSpec((tm, x.shape[1]), lambda i: (i, 0)), _resident(w.shape), _resident((1, d)), tile],
        out_specs=tile, out_shape=jax.ShapeDtypeStruct((t, d), F32),
        compiler_params=_params("parallel"), name="proj_res")(x, w, _row(b), h)


def _lru_in_body(hp_ref, hc_ref, hn_ref, g_ref, w_ref, b_ref, cw_ref, cb_ref, gb_ref, xb_ref, u_s,
                 *, tiles_per_seq, tm):
    r_w = gb_ref.shape[1]
    xn = _normed_with_halo(hp_ref, hc_ref, hn_ref, g_ref)
    gb_ref[...] = jax.nn.gelu(_dot(xn[HALO:HALO + tm], w_ref[:, :r_w]) + b_ref[:, :r_w]).astype(BF16)
    u = _dot(xn, w_ref[:, r_w:]) + b_ref[:, r_w:]
    _store_zero_padded(u_s, u, tiles_per_seq, tm)
    left = LRU_CONV // 2
    y = cb_ref[...]
    for j in range(LRU_CONV):
        y = y + cw_ref[j:j + 1, :] * u_s[pl.ds(HALO + j - left, tm), :]
    xb_ref[...] = y


def _lru_in(h, g, w, b, cw, cb, seq_len, tm=TOKEN_TILE):
    t, d = h.shape
    r_w = cw.shape[1]
    tile = pl.BlockSpec((tm, r_w), lambda i: (i, 0))
    return pl.pallas_call(
        functools.partial(_lru_in_body, tiles_per_seq=seq_len // tm, tm=tm),
        grid=(t // tm,),
        in_specs=_halo_specs(tm, d, t) + [_resident((1, d)), _resident(w.shape), _resident((1, 2 * r_w)),
                                          _resident(cw.shape), _resident((1, r_w))],
        out_specs=[tile, tile],
        out_shape=[jax.ShapeDtypeStruct((t, r_w), BF16), jax.ShapeDtypeStruct((t, r_w), F32)],
        scratch_shapes=[pltpu.VMEM((tm + 2 * HALO, r_w), F32)],
        compiler_params=_params("parallel"), name="lru_in")(h, h, h, _row(g), w, _row(b), cw, _row(cb))


def _block_diag(x, w_ref):
    nb, bw, _ = w_ref.shape
    return jnp.concatenate([_dot(x[:, n * bw:(n + 1) * bw], w_ref[n]) for n in range(nb)], axis=1)


def _lru_scan_tile(a, b, a_s, b_s, h_s, carry_ref, reverse):
    tm, width = a.shape
    n_groups = tm // SUBLANES
    a = a.reshape(n_groups, SUBLANES, width)
    b = b.reshape(n_groups, SUBLANES, width)
    row = lax.broadcasted_iota(jnp.int32, (1, SUBLANES, 1), 1)
    for dist in (1, 2, 4):
        if reverse:
            shift, ok = SUBLANES - dist, row < SUBLANES - dist
        else:
            shift, ok = dist, row >= dist
        a_prev = jnp.where(ok, pltpu.roll(a, shift, 1), 1.0)
        b_prev = jnp.where(ok, pltpu.roll(b, shift, 1), 0.0)
        b = a * b_prev + b
        a = a * a_prev
    a_s[...] = a.reshape(tm, width)
    b_s[...] = b.reshape(tm, width)
    edge = 0 if reverse else SUBLANES - 1

    def group(k, carry):
        gi = n_groups - 1 - k if reverse else k
        r0 = pl.multiple_of(gi * SUBLANES, SUBLANES)
        hg = a_s[pl.ds(r0, SUBLANES), :] * carry + b_s[pl.ds(r0, SUBLANES), :]
        h_s[pl.ds(r0, SUBLANES), :] = hg
        return hg[edge:edge + 1, :]

    carry_ref[...] = lax.fori_loop(0, n_groups, group, carry_ref[...])


def _lru_gates(xb, wa_ref, ba_ref, wx_ref, bx_ref, c8_ref, first_row):
    xb16 = xb.astype(BF16)
    gate_x = jax.nn.sigmoid(_block_diag(xb16, wx_ref) + bx_ref[...])
    gate_a = jax.nn.sigmoid(_block_diag(xb16, wa_ref) + ba_ref[...])
    log_a = gate_a * c8_ref[...]
    a = jnp.exp(log_a)
    mult = jnp.sqrt(1.0 - a * a)
    r = lax.broadcasted_iota(jnp.int32, (xb.shape[0], 1), 0)
    mult = jnp.where(r == first_row, 1.0, mult)
    return a, mult * gate_x * xb


def _lru_fwd_body(xb_ref, wa_ref, ba_ref, wx_ref, bx_ref, c8_ref, h0_ref, a_s, b_s, carry, *, tm):
    ti = pl.program_id(1)

    @pl.when(ti == 0)
    def _():
        carry[...] = jnp.zeros_like(carry)

    first_row = jnp.where(ti == 0, 0, -1)
    a, b = _lru_gates(xb_ref[...], wa_ref, ba_ref, wx_ref, bx_ref, c8_ref, first_row)
    _lru_scan_tile(a, b, a_s, b_s, h0_ref, carry, reverse=False)


def _lru_bwd_body(xb_ref, wa_ref, ba_ref, wx_ref, bx_ref, c8_ref, h0_ref, gb_ref, h_ref, wo_ref, bo_ref,
                  o_ref, a_s, b_s, h1_s, carry, *, tm):
    ti = pl.program_id(1)

    @pl.when(ti == 0)
    def _():
        carry[...] = jnp.zeros_like(carry)

    first_row = jnp.where(ti == 0, tm - 1, -1)
    a, b = _lru_gates(xb_ref[...], wa_ref, ba_ref, wx_ref, bx_ref, c8_ref, first_row)
    _lru_scan_tile(a, b, a_s, b_s, h1_s, carry, reverse=True)
    y = h0_ref[...] + h1_s[...]
    z = (gb_ref[...].astype(F32) * y).astype(BF16)
    o_ref[...] = h_ref[...] + _dot(z, wo_ref[...]) + bo_ref[...]


def _lru_mixer(h, g, prm, seq_len, tm=TOKEN_TILE):
    t, d = h.shape
    bsz = t // seq_len
    nt = seq_len // tm
    gb, xb = _lru_in(h, g, prm['in_w'], prm['in_b'], prm['conv_w'], prm['conv_b'], seq_len, tm)
    r_w = xb.shape[1]
    bw = r_w // LRU_BLOCKS
    gate_specs = lambda dr: [_resident((LRU_BLOCKS, bw, bw)), _resident((1, r_w)),
                             _resident((LRU_BLOCKS, bw, bw)), _resident((1, r_w)), _resident((1, r_w))]
    gate_args = lambda dr: [prm['wa'][dr], _row(prm['ba'][dr]), prm['wx'][dr], _row(prm['bx'][dr]),
                            _row(prm['c8'][dr])]
    fwd = lambda b, i: (b * nt + i, 0)
    bwd = lambda b, i: (b * nt + nt - 1 - i, 0)
    scan_scratch = [pltpu.VMEM((tm, r_w), F32), pltpu.VMEM((tm, r_w), F32)]
    h0 = pl.pallas_call(
        functools.partial(_lru_fwd_body, tm=tm), grid=(bsz, nt),
        in_specs=[pl.BlockSpec((tm, r_w), fwd)] + gate_specs(0),
        out_specs=pl.BlockSpec((tm, r_w), fwd), out_shape=jax.ShapeDtypeStruct((t, r_w), F32),
        scratch_shapes=scan_scratch + [pltpu.VMEM((1, r_w), F32)],
        compiler_params=_params("arbitrary", "arbitrary"), name="lru_fwd")(xb, *gate_args(0))
    return pl.pallas_call(
        functools.partial(_lru_bwd_body, tm=tm), grid=(bsz, nt),
        in_specs=[pl.BlockSpec((tm, r_w), bwd)] + gate_specs(1)
        + [pl.BlockSpec((tm, r_w), bwd), pl.BlockSpec((tm, r_w), bwd), pl.BlockSpec((tm, d), bwd),
           _resident((r_w, d)), _resident((1, d))],
        out_specs=pl.BlockSpec((tm, d), bwd), out_shape=jax.ShapeDtypeStruct((t, d), F32),
        scratch_shapes=scan_scratch + [pltpu.VMEM((tm, r_w), F32), pltpu.VMEM((1, r_w), F32)],
        compiler_params=_params("arbitrary", "arbitrary"), name="lru_bwd")(
            xb, *gate_args(1), h0, gb, h, prm['out_w'], _row(prm['out_b']))


NA_PAIR = 2 * GRID_W
NA_WIN = 5
NA_PB = 4
NA_HG = LANES // (1024 // NA_HEADS)


def _na_qkv_body(h_ref, g_ref, wq_ref, wkt_ref, wv_ref, bq_ref, bk_ref, bv_ref, q_ref, kt_ref, v_ref, *, scale):
    xn = _rms(h_ref[...], g_ref[...]).astype(BF16)
    q_ref[...] = ((_dot(xn, wq_ref[...]) + bq_ref[...]) * scale).astype(BF16)
    kt = lax.dot_general(wkt_ref[...], xn, (((1,), (1,)), ((), ())), preferred_element_type=F32)
    kt_ref[...] = (kt + bk_ref[...]).astype(BF16)
    v_ref[...] = (_dot(xn, wv_ref[...]) + bv_ref[...]).astype(BF16)


def _na_qkv(h, g, wq, wkt, wv, bq, bk, bv, seq_len, scale, tm=TOKEN_TILE):
    t, d = h.shape
    bsz, nt = t // seq_len, seq_len // tm
    tile = pl.BlockSpec((tm, d), lambda i: (i, 0))
    return pl.pallas_call(
        functools.partial(_na_qkv_body, scale=scale), grid=(t // tm,),
        in_specs=[tile, _resident((1, d)), _resident((d, d)), _resident((d, d)), _resident((d, d)),
                  _resident((1, d)), _resident((d, 1)), _resident((1, d))],
        out_specs=[tile, pl.BlockSpec((None, d, tm), lambda i: (i // nt, 0, i % nt)), tile],
        out_shape=[jax.ShapeDtypeStruct((t, d), BF16), jax.ShapeDtypeStruct((bsz, d, seq_len), BF16),
                   jax.ShapeDtypeStruct((t, d), BF16)],
        compiler_params=_params("parallel"), name="na_qkv")(
            h, _row(g), wq, wkt, wv, _row(bq), bk.reshape(d, 1).astype(F32), _row(bv))


def _na_attn_body(q_ref, kp_ref, kc_ref, kn_ref, vp_ref, vc_ref, vn_ref, sb_ref, o_ref, kwin, vwin,
                  *, n_pairs, rows):
    pb = pl.program_id(2)
    blk = NA_PB * NA_PAIR
    for bi, (k_ref, v_ref) in enumerate(((kp_ref, vp_ref), (kc_ref, vc_ref), (kn_ref, vn_ref))):
        for t in range(NA_PB):
            kwin[bi * NA_PB + t] = k_ref[:, t * NA_PAIR:(t + 1) * NA_PAIR]
        vwin[bi * blk:(bi + 1) * blk, :] = v_ref[...]
    dh = LANES // NA_HG
    pad = 2 * SUBLANES
    lane = lax.broadcasted_iota(jnp.int32, (1, LANES), 1)
    in_head = [(lane // dh) == j for j in range(NA_HG)]
    q_row_onehot = (lax.broadcasted_iota(jnp.int32, (NA_PAIR, pad), 0) // GRID_W
                    == lax.broadcasted_iota(jnp.int32, (NA_PAIR, pad), 1)).astype(BF16)
    mask_row = lax.broadcasted_iota(jnp.int32, (pad, 1), 0)
    k_row_in_win = lax.broadcasted_iota(jnp.int32, (1, NA_WIN * NA_PAIR), 1) // GRID_W

    def pair(i, carry):
        p = pb * NA_PB + i
        kp0 = jnp.clip(p - 2, 0, n_pairs - NA_WIN)
        off = kp0 - (pb - 1) * NA_PB
        m0 = kp0 - p + (NA_WIN - 1)
        qi = q_ref[pl.ds(pl.multiple_of(i * NA_PAIR, NA_PAIR), NA_PAIR), :]
        ksl = jnp.concatenate([kwin[off + jp] for jp in range(NA_WIN)], axis=1)
        vsl = vwin[pl.ds(pl.multiple_of(off * NA_PAIR, NA_PAIR), NA_WIN * NA_PAIR), :]
        r_start = jnp.clip(2 * p + mask_row - NA_KH // 2, 0, rows - NA_KH)
        k_row = 2 * kp0 + k_row_in_win
        in_window = (k_row >= r_start) & (k_row < r_start + NA_KH)
        row_mask = jnp.where(in_window | (mask_row >= 2), 0.0, NEG_INF).astype(BF16)
        q_heads = jnp.concatenate(
            [jnp.concatenate([jnp.where(m, qi, jnp.zeros_like(qi)), q_row_onehot], axis=1) for m in in_head], axis=0)
        bias = jnp.concatenate(
            [jnp.concatenate([sb_ref[j, m0 + jp] for jp in range(NA_WIN)], axis=1) for j in range(NA_HG)], axis=0)
        s = _dot(q_heads, jnp.concatenate([ksl, row_mask], axis=0)) + bias
        e = jnp.exp(s - jnp.max(s, axis=-1, keepdims=True))
        denom = jnp.sum(e, axis=-1, keepdims=True)
        e = e.astype(BF16)
        acc = jnp.zeros((NA_PAIR, LANES), F32)
        for j, m in enumerate(in_head):
            rows_j = slice(j * NA_PAIR, (j + 1) * NA_PAIR)
            acc = acc + jnp.where(m, _dot(e[rows_j], vsl) / denom[rows_j], 0.0)
        o_ref[pl.ds(pl.multiple_of(i * NA_PAIR, NA_PAIR), NA_PAIR), :] = acc.astype(BF16)
        return carry

    lax.fori_loop(0, NA_PB, pair, 0, unroll=True)


def _na_bias_table(rpb):
    n_h, n_dr, n_dc = rpb.shape
    col = np.arange(GRID_W)
    q_start = np.clip(col - NA_KW // 2, 0, GRID_W - NA_KW)
    col_ok = (col[None, :] >= q_start[:, None]) & (col[None, :] < q_start[:, None] + NA_KW)
    dc_idx = col[None, :] - col[:, None] + NA_KW - 1
    onehot = ((np.arange(n_dc)[:, None, None] == dc_idx[None]) & col_ok[None]).reshape(n_dc, -1)
    tab = jnp.dot(rpb.astype(F32).reshape(n_h * n_dr, n_dc), jnp.asarray(onehot, F32),
                  precision=lax.Precision.HIGHEST)
    tab = tab.reshape(n_h, n_dr, GRID_W, GRID_W) + jnp.where(col_ok, 0.0, NEG_INF).astype(F32)
    masked = jnp.full((n_h, GRID_W, GRID_W), NEG_INF, F32)

    def quadrant(dr):
        return tab[:, dr + NA_KH - 1] if abs(dr) <= NA_KH - 1 else masked

    tiles = [jnp.concatenate([jnp.concatenate([quadrant(2 * (m - (NA_WIN - 1)) + c - a) for c in range(2)], axis=2)
                              for a in range(2)], axis=1) for m in range(2 * (NA_WIN - 1) + 1)]
    sb = jnp.stack(tiles, axis=1)
    return sb.reshape(n_h // NA_HG, NA_HG, *sb.shape[1:])


def _na_mixer(h, g, prm, seq_len):
    t, d = h.shape
    bsz = t // seq_len
    rows = seq_len // GRID_W
    n_pairs = rows // 2
    nblk = n_pairs // NA_PB
    blk = NA_PB * NA_PAIR
    q, kt, v = _na_qkv(h, g, prm['wq'], prm['wkt'], prm['wv'], prm['bq'], prm['bk'], prm['bv'], seq_len,
                       (d // NA_HEADS) ** -0.5)
    q = q.reshape(bsz, seq_len, d)
    v = v.reshape(bsz, seq_len, d)
    prev = lambda pb: jnp.maximum(pb - 1, 0)
    nxt = lambda pb: jnp.minimum(pb + 1, nblk - 1)
    k_spec = lambda f: pl.BlockSpec((None, LANES, blk), lambda gi, b, pb: (b, gi, f(pb)))
    v_spec = lambda f: pl.BlockSpec((None, blk, LANES), lambda gi, b, pb: (b, f(pb), gi))
    same = lambda pb: pb
    sb = prm['bias']
    o = pl.pallas_call(
        functools.partial(_na_attn_body, n_pairs=n_pairs, rows=rows),
        grid=(d // LANES, bsz, nblk),
        in_specs=[v_spec(same), k_spec(prev), k_spec(same), k_spec(nxt), v_spec(prev), v_spec(same), v_spec(nxt),
                  pl.BlockSpec((None,) + sb.shape[1:], lambda gi, b, pb: (gi, 0, 0, 0, 0))],
        out_specs=v_spec(same), out_shape=jax.ShapeDtypeStruct((bsz, seq_len, d), BF16),
        scratch_shapes=[pltpu.VMEM((3 * NA_PB, LANES, NA_PAIR), BF16), pltpu.VMEM((3 * blk, LANES), BF16)],
        compiler_params=_params("arbitrary", "arbitrary", "arbitrary"), name="na_attn")(
            q, kt, kt, kt, v, v, v, sb)
    return _proj_res(o.reshape(t, d), prm['out_w'], prm['out_b'], h)


def _hy_in_body(hp_ref, hc_ref, hn_ref, g_ref, w_ref, b_ref, cw_ref, cb_ref, x0_ref, vv_ref, u_s,
                *, tiles_per_seq, tm):
    d = x0_ref.shape[1]
    xn = _normed_with_halo(hp_ref, hc_ref, hn_ref, g_ref)
    left = (HY_SHORT - 1) // 2

    def conv(c0):
        u = _dot(xn, w_ref[:, c0:c0 + d]) + b_ref[:, c0:c0 + d]
        _store_zero_padded(u_s, u, tiles_per_seq, tm)
        y = cb_ref[:, c0:c0 + d]
        for j in range(HY_SHORT):
            y = y + cw_ref[j:j + 1, c0:c0 + d] * u_s[pl.ds(HALO + j - left, tm), :]
        return y

    x0_ref[...] = conv(0).astype(x0_ref.dtype)
    x1 = conv(d)
    vv_ref[...] = conv(2 * d) * x1


def _hy_in(h, g, w, b, cw, cb, seq_len, tm=TOKEN_TILE):
    t, d = h.shape
    tile = pl.BlockSpec((tm, d), lambda i: (i, 0))
    return pl.pallas_call(
        functools.partial(_hy_in_body, tiles_per_seq=seq_len // tm, tm=tm), grid=(t // tm,),
        in_specs=_halo_specs(tm, d, t) + [_resident((1, d)), _resident(w.shape), _resident((1, 3 * d)),
                                          _resident(cw.shape), _resident((1, 3 * d))],
        out_specs=[tile, tile],
        out_shape=[jax.ShapeDtypeStruct((t, d), BF16), jax.ShapeDtypeStruct((t, d), F32)],
        scratch_shapes=[pltpu.VMEM((tm + 2 * HALO, d), F32)],
        compiler_params=_params("parallel"), name="hy_in")(h, h, h, _row(g), w, _row(b), cw, _row(cb))


def _hy_filter_body(z_ref, w1_ref, b1_ref, w2_ref, b2_ref, w3_ref, b3_ref, fr_ref, wo_ref, dl_ref,
                    k_ref):
    hp = lax.Precision.HIGHEST
    d = k_ref.shape[2]
    z = z_ref[...]
    fr = fr_ref[...]
    a = jnp.sin(fr * (jnp.dot(z, w1_ref[...], precision=hp, preferred_element_type=F32) + b1_ref[...]))
    a = jnp.sin(fr * (jnp.dot(a, w2_ref[...], precision=hp, preferred_element_type=F32) + b2_ref[...]))
    a = jnp.sin(fr * (jnp.dot(a, w3_ref[...], precision=hp, preferred_element_type=F32) + b3_ref[...]))
    k = _dot(a.astype(BF16), wo_ref[...])
    window = jnp.exp(-z[:, 0:1] * dl_ref[...])
    k_ref[0] = k[:, :d] * window
    r = lax.broadcasted_iota(jnp.int32, (z.shape[0], 1), 0)
    k_ref[1] = jnp.where((r == 0) & (pl.program_id(0) == 0), 0.0, k[:, d:] * window)


def _pad_to(x, shape):
    return jnp.pad(x.astype(F32), [(0, s - n) for s, n in zip(shape, x.shape)])


@functools.lru_cache(maxsize=None)
def _hy_positions(length, d):
    bands = (HY_EMB - 1) // 2
    t = np.linspace(0.0, 1.0, length)[:, None]
    ang = (2.0 * math.pi / length) * np.arange(length)[:, None]
    fb = np.linspace(1e-4, bands - 1, bands)[None, :]
    z = np.concatenate([t, np.cos(fb * ang), -np.sin(fb * ang)], axis=-1)
    z = np.pad(z, ((0, 0), (0, LANES - z.shape[1]))).astype(np.float32)
    decay_min = math.log(HY_DECAY_TARGET) / HY_DECAY_PCT_LO
    decay_max = math.log(HY_DECAY_TARGET) / HY_DECAY_PCT_HI
    deltas = np.abs(np.linspace(decay_min, decay_max, d))[None, :].astype(np.float32)
    return z, deltas


def _hy_filters(length, prm, d, tl=TOKEN_TILE):
    z, deltas = _hy_positions(length, d)
    sq = (LANES, LANES)
    args = [z, _pad_to(prm['f_w1'], sq), _pad_to(prm['f_b1'][None], (1, LANES)),
            _pad_to(prm['f_w2'], sq), _pad_to(prm['f_b2'][None], (1, LANES)),
            _pad_to(prm['f_w3'], sq), _pad_to(prm['f_b3'][None], (1, LANES)),
            _pad_to(prm['f_freq'][None], (1, LANES)), _pad_to(prm['f_out'], (LANES, 2 * d)).astype(BF16), deltas]
    return pl.pallas_call(
        _hy_filter_body, grid=(length // tl,),
        in_specs=[pl.BlockSpec((tl, LANES), lambda i: (i, 0))] + [_resident(a.shape) for a in args[1:]],
        out_specs=pl.BlockSpec((2, tl, d), lambda i: (0, i, 0)),
        out_shape=jax.ShapeDtypeStruct((2, length, d), F32),
        compiler_params=_params("parallel"), name="hy_filter")(*args)


def _dft_plan(length):
    n = 2 * length
    n1 = min(128, n // 16)
    return n, n1, n // n1, n1 // 2 + 1


@functools.lru_cache(maxsize=None)
def _dft_tables(length):
    n, n1, n2, n1e = _dft_plan(length)
    n1h, sub = n1 // 2, SUBLANES
    groups = n2 // sub
    k1 = np.arange(n1e)
    t_idx = (n2 * np.arange(n1h)[None, None, None, :] + sub * np.arange(groups)[:, None, None, None]
             + np.arange(sub)[None, None, :, None])
    th = 2.0 * np.pi * ((k1[None, :, None, None] * t_idx) % n) / n

    def expand(m):
        return np.einsum('gksi,st->gksit', m, np.eye(sub)).reshape(groups, n1e * sub, n1h * sub)

    fwd1 = np.concatenate([expand(np.cos(th)), expand(-np.sin(th))], axis=1)
    weight = np.where((k1 == 0) | (k1 == n1 // 2), 1.0, 2.0)[None, :, None, None] / n
    inv1 = np.concatenate([expand(weight * np.cos(th)), expand(-weight * np.sin(th))], axis=1).transpose(0, 2, 1)
    inv1 = np.pad(inv1, ((0, 0), (0, 0), (0, -inv1.shape[2] % LANES)))
    idx = np.arange(n2, dtype=np.int64)
    th2 = 2.0 * np.pi * ((idx[:, None] * idx[None, :]) % n2) / n2
    c2, s2 = np.cos(th2), np.sin(th2)
    fwd2 = np.block([[c2, s2], [-s2, c2]])
    inv2 = np.block([[c2, -s2], [s2, c2]])
    return {name: tab.astype(np.float32).astype(BF16)
            for name, tab in (('fwd1', fwd1), ('inv1', inv1), ('fwd2', fwd2), ('inv2', inv2))}


def _dft1_body(x_ref, t_ref, o_ref):
    n1h, sub, d = x_ref.shape
    x = x_ref[...].reshape(n1h * sub, d).astype(BF16)
    o_ref[...] = _dot(t_ref[...], x).reshape(o_ref.shape)


def _dft1(x, length, tabs):
    bx, _, d = x.shape
    n, n1, n2, n1e = _dft_plan(length)
    tab = tabs['fwd1']
    a = pl.pallas_call(
        _dft1_body, grid=(n2 // SUBLANES, bx),
        in_specs=[pl.BlockSpec((None, n1 // 2, None, SUBLANES, d), lambda j, b: (b, 0, j, 0, 0)),
                  pl.BlockSpec((None,) + tab.shape[1:], lambda j, b: (j, 0, 0))],
        out_specs=pl.BlockSpec((None, 2, n1e, None, SUBLANES, d), lambda j, b: (b, 0, 0, j, 0, 0)),
        out_shape=jax.ShapeDtypeStruct((bx, 2, n1e, n2 // SUBLANES, SUBLANES, d), F32),
        compiler_params=_params("parallel", "arbitrary"), name="hy_dft1")(
            x.reshape(bx, n1 // 2, n2 // SUBLANES, SUBLANES, d), tab)
    return a.reshape(bx, 2, n1e, n2, d)


def _filter_spectrum_body(a_ref, f_ref, o_ref):
    n2 = o_ref.shape[1]
    d = o_ref.shape[2]
    xf = _dot(f_ref[...], a_ref[0].reshape(2 * n2, d).astype(BF16))
    xb = _dot(f_ref[...], a_ref[1].reshape(2 * n2, d).astype(BF16))
    o_ref[0] = xf[:n2] + xb[:n2]
    o_ref[1] = xf[n2:] - xb[n2:]


def _filter_spectrum(k, length, tabs):
    d = k.shape[2]
    n, n1, n2, n1e = _dft_plan(length)
    a = _dft1(k, length, tabs)
    return pl.pallas_call(
        _filter_spectrum_body, grid=(n1e,),
        in_specs=[pl.BlockSpec((2, 2, None, n2, d), lambda k: (0, 0, k, 0, 0)), _resident(tabs['fwd2'].shape)],
        out_specs=pl.BlockSpec((2, None, n2, d), lambda k: (0, k, 0, 0)),
        out_shape=jax.ShapeDtypeStruct((2, n1e, n2, d), F32),
        compiler_params=_params("parallel"), name="hy_filter_spectrum")(a, tabs['fwd2'])


def _spectral_body(a_ref, k_ref, f_ref, i_ref, o_ref):
    bb, _, n2, d = o_ref.shape
    a = jnp.concatenate([a_ref[b].reshape(2 * n2, d) for b in range(bb)], axis=1).astype(BF16)
    x = _dot(f_ref[...], a)
    xr, xi = x[:n2], x[n2:]
    kr = jnp.concatenate([k_ref[0]] * bb, axis=1)
    ki = jnp.concatenate([k_ref[1]] * bb, axis=1)
    y = jnp.concatenate([xr * kr - xi * ki, xr * ki + xi * kr], axis=0).astype(BF16)
    z = _dot(i_ref[...], y)
    for b in range(bb):
        o_ref[b] = z[:, b * d:(b + 1) * d].reshape(2, n2, d)


def _spectral_multiply(a, kc, length, tabs):
    bx = a.shape[0]
    _, n1e, n2, d = kc.shape
    bb = max(b for b in range(1, bx + 1) if bx % b == 0 and b * n2 <= 2 * MXU_DIM)
    blk = pl.BlockSpec((bb, 2, None, n2, d), lambda k, b: (b, 0, k, 0, 0))
    return pl.pallas_call(
        _spectral_body, grid=(n1e, bx // bb),
        in_specs=[blk, pl.BlockSpec((2, None, n2, d), lambda k, b: (0, k, 0, 0)),
                  _resident(tabs['fwd2'].shape), _resident(tabs['inv2'].shape)],
        out_specs=blk, out_shape=jax.ShapeDtypeStruct((bx, 2, n1e, n2, d), F32),
        compiler_params=_params("parallel", "arbitrary"), name="hy_spectral")(
            a, kc, tabs['fwd2'], tabs['inv2'])


def _idft1_body(z_ref, t_ref, o_ref):
    _, n1e, sub, d = z_ref.shape
    rows = 2 * n1e * sub
    z = z_ref[...].reshape(rows, d).astype(BF16)
    z = jnp.concatenate([z, jnp.zeros((t_ref.shape[1] - rows, d), BF16)], axis=0)
    o_ref[...] = _dot(t_ref[...], z).reshape(o_ref.shape)


def _idft1(z, length, tabs):
    bx, _, n1e, n2, d = z.shape
    n, n1, _, _ = _dft_plan(length)
    tab = tabs['inv1']
    y = pl.pallas_call(
        _idft1_body, grid=(n2 // SUBLANES, bx),
        in_specs=[pl.BlockSpec((None, 2, n1e, None, SUBLANES, d), lambda j, b: (b, 0, 0, j, 0, 0)),
                  pl.BlockSpec((None,) + tab.shape[1:], lambda j, b: (j, 0, 0))],
        out_specs=pl.BlockSpec((None, n1 // 2, None, SUBLANES, d), lambda j, b: (b, 0, j, 0, 0)),
        out_shape=jax.ShapeDtypeStruct((bx, n1 // 2, n2 // SUBLANES, SUBLANES, d), F32),
        compiler_params=_params("parallel", "arbitrary"), name="hy_idft1")(
            z.reshape(bx, 2, n1e, n2 // SUBLANES, SUBLANES, d), tab)
    return y.reshape(bx, length, d)


def _hy_out_body(y_ref, vv_ref, x0_ref, h_ref, sk_ref, w_ref, b_ref, o_ref):
    y = (y_ref[...] + vv_ref[...] * sk_ref[...]) * x0_ref[...]
    o_ref[...] = h_ref[...] + _dot(y.astype(BF16), w_ref[...]) + b_ref[...]


def _hy_out(yc, vv, x0, h, skip, w, b, tm=TOKEN_TILE):
    t, d = h.shape
    tile = pl.BlockSpec((tm, d), lambda i: (i, 0))
    return pl.pallas_call(
        _hy_out_body, grid=(t // tm,),
        in_specs=[tile, tile, tile, tile, _resident((1, d)), _resident((d, d)), _resident((1, d))],
        out_specs=tile, out_shape=jax.ShapeDtypeStruct((t, d), F32),
        compiler_params=_params("parallel"), name="hy_out")(yc, vv, x0, h, _row(skip), w, _row(b))


def _hyena_mixer(h, g, prm, seq_len):
    t, d = h.shape
    bsz = t // seq_len
    tabs = _dft_tables(seq_len)
    x0, vv = _hy_in(h, g, prm['in_w'], prm['in_b'], prm['conv_w'], prm['conv_b'], seq_len)
    kc = _filter_spectrum(_hy_filters(seq_len, prm, d), seq_len, tabs)
    a = _dft1(vv.reshape(bsz, seq_len, d), seq_len, tabs)
    z = _spectral_multiply(a, kc, seq_len, tabs)
    yc = _idft1(z, seq_len, tabs).reshape(t, d)
    return _hy_out(yc, vv, x0, h, prm['skip'], prm['out_w'], prm['out_b'])


def _trunk(x, p, w, depth):
    bsz, seq_len, d = x.shape
    h = x.reshape(bsz * seq_len, d)
    p = p.reshape(depth, bsz * seq_len, -1)
    for i in range(depth):
        kind, j = i % N_MIXERS, i // N_MIXERS
        h = _ffn(h, w['ln_ffn1'][i], w['ffn1_wg'][i], w['ffn1_wu'][i], w['ffn1_wd'][i])
        mix = (_hyena_mixer, _na_mixer, _lru_mixer)[kind]
        h = mix(h, w['ln_mix'][i], w[('hy', 'na', 'lru')[kind]][j], seq_len)
        h = _ffn(h, w['ln_ffn2'][i], w['ffn2_wg'][i], w['ffn2_wu'][i], w['ffn2_wd'][i],
                 ple=(w['ln_ple'][i], w['ple_gate'][i], p, i, w['ple_proj'][i]),
                 final=w['ln_final'] if i == depth - 1 else None)
    return h.reshape(bsz, seq_len, d)


def kernel(x_prompt, x_sample, p_prompt, p_sample, ln_ffn1, ffn1_wg, ffn1_wu, ffn1_wd, ln_mix, ln_ffn2, ffn2_wg, ffn2_wu, ffn2_wd, ln_ple, ple_gate, ple_proj, ln_final, hy_in_w, hy_in_b, hy_conv_w, hy_conv_b, hy_f_w1, hy_f_b1, hy_f_w2, hy_f_b2, hy_f_w3, hy_f_b3, hy_f_freq, hy_f_out, hy_skip, hy_out_w, hy_out_b, na_qkv_w, na_qkv_b, na_rpb, na_out_w, na_out_b, lru_in_w, lru_in_b, lru_conv_w, lru_conv_b, lru_wa, lru_ba, lru_wx, lru_bx, lru_lambda, lru_out_w, lru_out_b):
    depth = ln_ffn1.shape[0]
    d = x_prompt.shape[-1]
    b16 = lambda a: a.astype(BF16)
    w = {
        'ln_ffn1': ln_ffn1, 'ffn1_wg': b16(ffn1_wg), 'ffn1_wu': b16(ffn1_wu), 'ffn1_wd': b16(ffn1_wd),
        'ln_mix': ln_mix, 'ln_ffn2': ln_ffn2, 'ffn2_wg': b16(ffn2_wg), 'ffn2_wu': b16(ffn2_wu),
        'ffn2_wd': b16(ffn2_wd), 'ln_ple': ln_ple, 'ple_gate': b16(ple_gate), 'ple_proj': b16(ple_proj),
        'ln_final': ln_final,
        'hy': [dict(in_w=b16(hy_in_w[j]), in_b=hy_in_b[j], conv_w=hy_conv_w[j], conv_b=hy_conv_b[j],
                    f_w1=hy_f_w1[j], f_b1=hy_f_b1[j], f_w2=hy_f_w2[j], f_b2=hy_f_b2[j], f_w3=hy_f_w3[j],
                    f_b3=hy_f_b3[j], f_freq=hy_f_freq[j], f_out=hy_f_out[j], skip=hy_skip[j],
                    out_w=b16(hy_out_w[j]), out_b=hy_out_b[j]) for j in range(hy_in_w.shape[0])],
        'na': [dict(wq=b16(na_qkv_w[j][:, :d]), wkt=b16(na_qkv_w[j][:, d:2 * d].T), wv=b16(na_qkv_w[j][:, 2 * d:]),
                    bq=na_qkv_b[j][:d], bk=na_qkv_b[j][d:2 * d], bv=na_qkv_b[j][2 * d:],
                    bias=_na_bias_table(na_rpb[j]), out_w=b16(na_out_w[j]), out_b=na_out_b[j])
               for j in range(na_qkv_w.shape[0])],
        'lru': [dict(in_w=b16(lru_in_w[j]), in_b=lru_in_b[j], conv_w=lru_conv_w[j], conv_b=lru_conv_b[j],
                     wa=b16(lru_wa[j]), ba=lru_ba[j], wx=b16(lru_wx[j]), bx=lru_bx[j],
                     c8=-LRU_C * jax.nn.softplus(-lru_lambda[j].astype(F32)),
                     out_w=b16(lru_out_w[j]), out_b=lru_out_b[j]) for j in range(lru_in_w.shape[0])],
    }
    return _trunk(x_prompt, p_prompt, w, depth), _trunk(x_sample, p_sample, w, depth)
```

```python
import functools
import math

import numpy as np
import jax
import jax.numpy as jnp
from jax import lax
from jax.experimental import pallas as pl
from jax.experimental.pallas import tpu as pltpu

F32 = jnp.float32
BF16 = jnp.bfloat16

RMS_EPS = 1e-6
NEG_INF = -1e30
LANES = 128
SUBLANES = 8
MXU_DIM = 256
VMEM_LIMIT = 56 * 1024 * 1024

GRID_W = 64
NA_HEADS = 32
NA_KH = 8
NA_KW = 16
LRU_BLOCKS = 4
LRU_CONV = 4
LRU_C = 8.0
HY_EMB = 33
HY_SHORT = 3
HY_DECAY_TARGET = 1e-2
HY_DECAY_PCT_HI = 0.3
HY_DECAY_PCT_LO = 1.5
N_MIXERS = 3

TOKEN_TILE = 512
FFN_TILE = 1024
HALO = SUBLANES


def _params(*sem):
    return pltpu.CompilerParams(dimension_semantics=sem, vmem_limit_bytes=VMEM_LIMIT)


def _resident(shape):
    nd = len(shape)
    return pl.BlockSpec(shape, lambda *_: (0,) * nd, pipeline_mode=pl.Buffered(1))


def _rms(x, g):
    ms = jnp.mean(x * x, axis=-1, keepdims=True)
    return x * lax.rsqrt(ms + RMS_EPS) * g


def _dot(a, b):
    return jnp.dot(a, b, preferred_element_type=F32)


def _sigmoid(x):
    return 0.5 * jnp.tanh(0.5 * x) + 0.5


def _row(v):
    return v.reshape(1, -1).astype(F32)


def _ffn_body(*refs, ple, final, n_chunks):
    h_ref, g_ref, wg_ref, wu_ref, wd_ref = refs[:5]
    o_ref = refs[-1]
    h = h_ref[...]
    xn = _rms(h, g_ref[...]).astype(BF16)
    n_tiles = pl.cdiv(wg_ref.shape[1], MXU_DIM)
    bounds = [min(-(-n_tiles * c // n_chunks) * MXU_DIM, wg_ref.shape[1]) for c in range(n_chunks + 1)]
    acc = jnp.zeros_like(h)
    for c0, c1 in zip(bounds[:-1], bounds[1:]):
        gt = _dot(xn, wg_ref[:, c0:c1])
        up = _dot(xn, wu_ref[:, c0:c1])
        act = (gt * jax.nn.sigmoid(gt) * up).astype(BF16)
        acc = acc + _dot(act, wd_ref[c0:c1, :])
    h = h + 0.5 * acc
    k = 5
    if ple:
        gp_ref, wpg_ref, p_ref, wpp_ref = refs[k:k + 4]
        k += 4
        gate = jax.nn.sigmoid(_dot(_rms(h, gp_ref[...]).astype(BF16), wpg_ref[...]))
        h = h + gate * _dot(p_ref[...].astype(BF16), wpp_ref[...])
    if final:
        h = _rms(h, refs[k][...])
    o_ref[...] = h


def _ffn(h, g, wg, wu, wd, ple=None, final=None, tm=FFN_TILE):
    t, d = h.shape
    d_ff = wg.shape[1]
    tile = pl.BlockSpec((tm, d), lambda i: (i, 0))
    args = [h, _row(g), wg, wu, wd]
    specs = [tile, _resident((1, d)), _resident((d, d_ff)), _resident((d, d_ff)), _resident((d_ff, d))]
    if ple is not None:
        gp, wpg, p, layer, wpp = ple
        args += [_row(gp), wpg, p, wpp]
        specs += [_resident((1, d)), _resident((d, d)),
                  pl.BlockSpec((None, tm, p.shape[2]), lambda i: (layer, i, 0)), _resident(wpp.shape)]
    if final is not None:
        args.append(_row(final))
        specs.append(_resident((1, d)))
    return pl.pallas_call(
        functools.partial(_ffn_body, ple=ple is not None, final=final is not None, n_chunks=2),
        grid=(t // tm,), in_specs=specs, out_specs=tile,
        out_shape=jax.ShapeDtypeStruct((t, d), F32),
        compiler_params=_params("parallel"), name="ffn")(*args)


def _halo_specs(tm, d, n_rows):
    per = tm // HALO
    last = n_rows // HALO - 1
    return [pl.BlockSpec((HALO, d), lambda i: (jnp.maximum(i * per - 1, 0), 0)),
            pl.BlockSpec((tm, d), lambda i: (i, 0)),
            pl.BlockSpec((HALO, d), lambda i: (jnp.minimum((i + 1) * per, last), 0))]


def _normed_with_halo(hp_ref, hc_ref, hn_ref, g_ref):
    x = jnp.concatenate([hp_ref[...], hc_ref[...], hn_ref[...]], axis=0)
    return _rms(x, g_ref[...]).astype(BF16)


def _store_zero_padded(u_s, u, tiles_per_seq, tm):
    i = pl.program_id(0) % tiles_per_seq
    u_s[HALO:tm + HALO, :] = u[HALO:tm + HALO]
    u_s[0:HALO, :] = jnp.where(i == 0, 0.0, u[0:HALO])
    u_s[tm + HALO:tm + 2 * HALO, :] = jnp.where(i == tiles_per_seq - 1, 0.0, u[tm + HALO:tm + 2 * HALO])


def _proj_res_body(x_ref, w_ref, b_ref, h_ref, o_ref):
    o_ref[...] = h_ref[...] + _dot(x_ref[...], w_ref[...]) + b_ref[...]


def _proj_res(x, w, b, h, tm=TOKEN_TILE):
    t, d = h.shape
    tile = pl.BlockSpec((tm, d), lambda i: (i, 0))
    return pl.pallas_call(
        _proj_res_body, grid=(t // tm,),
        in_specs=[pl.BlockSpec((tm, x.shape[1]), lambda i: (i, 0)), _resident(w.shape), _resident((1, d)), tile],
        out_specs=tile, out_shape=jax.ShapeDtypeStruct((t, d), F32),
        compiler_params=_params("parallel"), name="proj_res")(x, w, _row(b), h)


def _lru_in_body(hp_ref, hc_ref, hn_ref, g_ref, w_ref, b_ref, cw_ref, cb_ref, gb_ref, xb_ref, u_s,
                 *, tiles_per_seq, tm):
    r_w = gb_ref.shape[1]
    xn = _normed_with_halo(hp_ref, hc_ref, hn_ref, g_ref)
    gb_ref[...] = jax.nn.gelu(_dot(xn[HALO:HALO + tm], w_ref[:, :r_w]) + b_ref[:, :r_w]).astype(BF16)
    u = _dot(xn, w_ref[:, r_w:]) + b_ref[:, r_w:]
    _store_zero_padded(u_s, u, tiles_per_seq, tm)
    left = LRU_CONV // 2
    y = cb_ref[...]
    for j in range(LRU_CONV):
        y = y + cw_ref[j:j + 1, :] * u_s[pl.ds(HALO + j - left, tm), :]
    xb_ref[...] = y


def _lru_in(h, g, w, b, cw, cb, seq_len, tm=TOKEN_TILE):
    t, d = h.shape
    r_w = cw.shape[1]
    tile = pl.BlockSpec((tm, r_w), lambda i: (i, 0))
    return pl.pallas_call(
        functools.partial(_lru_in_body, tiles_per_seq=seq_len // tm, tm=tm),
        grid=(t // tm,),
        in_specs=_halo_specs(tm, d, t) + [_resident((1, d)), _resident(w.shape), _resident((1, 2 * r_w)),
                                          _resident(cw.shape), _resident((1, r_w))],
        out_specs=[tile, tile],
        out_shape=[jax.ShapeDtypeStruct((t, r_w), BF16), jax.ShapeDtypeStruct((t, r_w), F32)],
        scratch_shapes=[pltpu.VMEM((tm + 2 * HALO, r_w), F32)],
        compiler_params=_params("parallel"), name="lru_in")(h, h, h, _row(g), w, _row(b), cw, _row(cb))


def _block_diag(x, w_ref):
    nb, bw, _ = w_ref.shape
    return jnp.concatenate([_dot(x[:, n * bw:(n + 1) * bw], w_ref[n]) for n in range(nb)], axis=1)


def _lru_scan_tile(a, b, a_s, b_s, h_s, carry_ref, reverse):
    tm, width = a.shape
    n_groups = tm // SUBLANES
    a = a.reshape(n_groups, SUBLANES, width)
    b = b.reshape(n_groups, SUBLANES, width)
    row = lax.broadcasted_iota(jnp.int32, (1, SUBLANES, 1), 1)
    for dist in (1, 2, 4):
        if reverse:
            shift, ok = SUBLANES - dist, row < SUBLANES - dist
        else:
            shift, ok = dist, row >= dist
        a_prev = jnp.where(ok, pltpu.roll(a, shift, 1), 1.0)
        b_prev = jnp.where(ok, pltpu.roll(b, shift, 1), 0.0)
        b = a * b_prev + b
        a = a * a_prev
    a_s[...] = a.reshape(tm, width)
    b_s[...] = b.reshape(tm, width)
    edge = 0 if reverse else SUBLANES - 1

    def group(k, carry):
        gi = n_groups - 1 - k if reverse else k
        r0 = pl.multiple_of(gi * SUBLANES, SUBLANES)
        hg = a_s[pl.ds(r0, SUBLANES), :] * carry + b_s[pl.ds(r0, SUBLANES), :]
        h_s[pl.ds(r0, SUBLANES), :] = hg
        return hg[edge:edge + 1, :]

    carry_ref[...] = lax.fori_loop(0, n_groups, group, carry_ref[...])


def _lru_gates(xb, wa_ref, ba_ref, wx_ref, bx_ref, c8_ref, first_row):
    xb16 = xb.astype(BF16)
    gate_x = _sigmoid(_block_diag(xb16, wx_ref) + bx_ref[...])
    gate_a = _sigmoid(_block_diag(xb16, wa_ref) + ba_ref[...])
    log_a = gate_a * c8_ref[...]
    a = jnp.exp(log_a)
    mult = jnp.sqrt(1.0 - a * a)
    r = lax.broadcasted_iota(jnp.int32, (xb.shape[0], 1), 0)
    mult = jnp.where(r == first_row, 1.0, mult)
    return a, mult * gate_x * xb


def _lru_fwd_body(xb_ref, wa_ref, ba_ref, wx_ref, bx_ref, c8_ref, h0_ref, a_s, b_s, carry, *, tm):
    ti = pl.program_id(1)

    @pl.when(ti == 0)
    def _():
        carry[...] = jnp.zeros_like(carry)

    first_row = jnp.where(ti == 0, 0, -1)
    a, b = _lru_gates(xb_ref[...], wa_ref, ba_ref, wx_ref, bx_ref, c8_ref, first_row)
    _lru_scan_tile(a, b, a_s, b_s, h0_ref, carry, reverse=False)


def _lru_bwd_body(xb_ref, wa_ref, ba_ref, wx_ref, bx_ref, c8_ref, h0_ref, gb_ref, h_ref, wo_ref, bo_ref,
                  o_ref, a_s, b_s, h1_s, carry, *, tm):
    ti = pl.program_id(1)

    @pl.when(ti == 0)
    def _():
        carry[...] = jnp.zeros_like(carry)

    first_row = jnp.where(ti == 0, tm - 1, -1)
    a, b = _lru_gates(xb_ref[...], wa_ref, ba_ref, wx_ref, bx_ref, c8_ref, first_row)
    _lru_scan_tile(a, b, a_s, b_s, h1_s, carry, reverse=True)
    y = h0_ref[...] + h1_s[...]
    z = (gb_ref[...].astype(F32) * y).astype(BF16)
    o_ref[...] = h_ref[...] + _dot(z, wo_ref[...]) + bo_ref[...]


def _lru_mixer(h, g, prm, seq_len, tm=TOKEN_TILE):
    t, d = h.shape
    bsz = t // seq_len
    nt = seq_len // tm
    gb, xb = _lru_in(h, g, prm['in_w'], prm['in_b'], prm['conv_w'], prm['conv_b'], seq_len, tm)
    r_w = xb.shape[1]
    bw = r_w // LRU_BLOCKS
    gate_specs = lambda dr: [_resident((LRU_BLOCKS, bw, bw)), _resident((1, r_w)),
                             _resident((LRU_BLOCKS, bw, bw)), _resident((1, r_w)), _resident((1, r_w))]
    gate_args = lambda dr: [prm['wa'][dr], _row(prm['ba'][dr]), prm['wx'][dr], _row(prm['bx'][dr]),
                            _row(prm['c8'][dr])]
    fwd = lambda b, i: (b * nt + i, 0)
    bwd = lambda b, i: (b * nt + nt - 1 - i, 0)
    scan_scratch = [pltpu.VMEM((tm, r_w), F32), pltpu.VMEM((tm, r_w), F32)]
    h0 = pl.pallas_call(
        functools.partial(_lru_fwd_body, tm=tm), grid=(bsz, nt),
        in_specs=[pl.BlockSpec((tm, r_w), fwd)] + gate_specs(0),
        out_specs=pl.BlockSpec((tm, r_w), fwd), out_shape=jax.ShapeDtypeStruct((t, r_w), F32),
        scratch_shapes=scan_scratch + [pltpu.VMEM((1, r_w), F32)],
        compiler_params=_params("arbitrary", "arbitrary"), name="lru_fwd")(xb, *gate_args(0))
    return pl.pallas_call(
        functools.partial(_lru_bwd_body, tm=tm), grid=(bsz, nt),
        in_specs=[pl.BlockSpec((tm, r_w), bwd)] + gate_specs(1)
        + [pl.BlockSpec((tm, r_w), bwd), pl.BlockSpec((tm, r_w), bwd), pl.BlockSpec((tm, d), bwd),
           _resident((r_w, d)), _resident((1, d))],
        out_specs=pl.BlockSpec((tm, d), bwd), out_shape=jax.ShapeDtypeStruct((t, d), F32),
        scratch_shapes=scan_scratch + [pltpu.VMEM((tm, r_w), F32), pltpu.VMEM((1, r_w), F32)],
        compiler_params=_params("arbitrary", "arbitrary"), name="lru_bwd")(
            xb, *gate_args(1), h0, gb, h, prm['out_w'], _row(prm['out_b']))


NA_PAIR = 2 * GRID_W
NA_WIN = 5
NA_PB = 4
NA_HG = LANES // (1024 // NA_HEADS)


def _na_qkv_body(h_ref, g_ref, wq_ref, wkt_ref, wv_ref, bq_ref, bk_ref, bv_ref, q_ref, kt_ref, v_ref, *, scale):
    xn = _rms(h_ref[...], g_ref[...]).astype(BF16)
    q_ref[...] = ((_dot(xn, wq_ref[...]) + bq_ref[...]) * scale).astype(BF16)
    kt = lax.dot_general(wkt_ref[...], xn, (((1,), (1,)), ((), ())), preferred_element_type=F32)
    kt_ref[...] = (kt + bk_ref[...]).astype(BF16)
    v_ref[...] = (_dot(xn, wv_ref[...]) + bv_ref[...]).astype(BF16)


def _na_qkv(h, g, wq, wkt, wv, bq, bk, bv, seq_len, scale, tm=TOKEN_TILE):
    t, d = h.shape
    bsz, nt = t // seq_len, seq_len // tm
    tile = pl.BlockSpec((tm, d), lambda i: (i, 0))
    return pl.pallas_call(
        functools.partial(_na_qkv_body, scale=scale), grid=(t // tm,),
        in_specs=[tile, _resident((1, d)), _resident((d, d)), _resident((d, d)), _resident((d, d)),
                  _resident((1, d)), _resident((d, 1)), _resident((1, d))],
        out_specs=[tile, pl.BlockSpec((None, d, tm), lambda i: (i // nt, 0, i % nt)), tile],
        out_shape=[jax.ShapeDtypeStruct((t, d), BF16), jax.ShapeDtypeStruct((bsz, d, seq_len), BF16),
                   jax.ShapeDtypeStruct((t, d), BF16)],
        compiler_params=_params("parallel"), name="na_qkv")(
            h, _row(g), wq, wkt, wv, _row(bq), bk.reshape(d, 1).astype(F32), _row(bv))


def _na_attn_body(q_ref, kp_ref, kc_ref, kn_ref, vp_ref, vc_ref, vn_ref, sb_ref, o_ref, kwin, vwin,
                  *, n_pairs, rows):
    pb = pl.program_id(2)
    blk = NA_PB * NA_PAIR
    for bi, (k_ref, v_ref) in enumerate(((kp_ref, vp_ref), (kc_ref, vc_ref), (kn_ref, vn_ref))):
        for t in range(NA_PB):
            kwin[bi * NA_PB + t] = k_ref[:, t * NA_PAIR:(t + 1) * NA_PAIR]
        vwin[bi * blk:(bi + 1) * blk, :] = v_ref[...]
    dh = LANES // NA_HG
    pad = 2 * SUBLANES
    lane = lax.broadcasted_iota(jnp.int32, (1, LANES), 1)
    in_head = [(lane // dh) == j for j in range(NA_HG)]
    q_row_onehot = (lax.broadcasted_iota(jnp.int32, (NA_PAIR, pad), 0) // GRID_W
                    == lax.broadcasted_iota(jnp.int32, (NA_PAIR, pad), 1)).astype(BF16)
    mask_row = lax.broadcasted_iota(jnp.int32, (pad, 1), 0)
    k_row_in_win = lax.broadcasted_iota(jnp.int32, (1, NA_WIN * NA_PAIR), 1) // GRID_W

    def pair(i, carry):
        p = pb * NA_PB + i
        kp0 = jnp.clip(p - 2, 0, n_pairs - NA_WIN)
        off = kp0 - (pb - 1) * NA_PB
        m0 = kp0 - p + (NA_WIN - 1)
        qi = q_ref[pl.ds(pl.multiple_of(i * NA_PAIR, NA_PAIR), NA_PAIR), :]
        ksl = jnp.concatenate([kwin[off + jp] for jp in range(NA_WIN)], axis=1)
        vsl = vwin[pl.ds(pl.multiple_of(off * NA_PAIR, NA_PAIR), NA_WIN * NA_PAIR), :]
        r_start = jnp.clip(2 * p + mask_row - NA_KH // 2, 0, rows - NA_KH)
        k_row = 2 * kp0 + k_row_in_win
        in_window = (k_row >= r_start) & (k_row < r_start + NA_KH)
        row_mask = jnp.where(in_window | (mask_row >= 2), 0.0, NEG_INF).astype(BF16)
        q_heads = jnp.concatenate(
            [jnp.concatenate([jnp.where(m, qi, jnp.zeros_like(qi)), q_row_onehot], axis=1) for m in in_head], axis=0)
        bias = jnp.concatenate(
            [jnp.concatenate([sb_ref[j, m0 + jp] for jp in range(NA_WIN)], axis=1) for j in range(NA_HG)], axis=0)
        s = _dot(q_heads, jnp.concatenate([ksl, row_mask], axis=0)) + bias
        e = jnp.exp(s - jnp.max(s, axis=-1, keepdims=True))
        denom = jnp.sum(e, axis=-1, keepdims=True)
        e = e.astype(BF16)
        acc = jnp.zeros((NA_PAIR, LANES), F32)
        for j, m in enumerate(in_head):
            rows_j = slice(j * NA_PAIR, (j + 1) * NA_PAIR)
            acc = acc + jnp.where(m, _dot(e[rows_j], vsl) / denom[rows_j], 0.0)
        o_ref[pl.ds(pl.multiple_of(i * NA_PAIR, NA_PAIR), NA_PAIR), :] = acc.astype(BF16)
        return carry

    lax.fori_loop(0, NA_PB, pair, 0, unroll=True)


def _na_bias_table(rpb):
    n_h, n_dr, n_dc = rpb.shape
    col = np.arange(GRID_W)
    q_start = np.clip(col - NA_KW // 2, 0, GRID_W - NA_KW)
    col_ok = (col[None, :] >= q_start[:, None]) & (col[None, :] < q_start[:, None] + NA_KW)
    dc_idx = col[None, :] - col[:, None] + NA_KW - 1
    onehot = ((np.arange(n_dc)[:, None, None] == dc_idx[None]) & col_ok[None]).reshape(n_dc, -1)
    tab = jnp.dot(rpb.astype(F32).reshape(n_h * n_dr, n_dc), jnp.asarray(onehot, F32),
                  precision=lax.Precision.HIGHEST)
    tab = tab.reshape(n_h, n_dr, GRID_W, GRID_W) + jnp.where(col_ok, 0.0, NEG_INF).astype(F32)
    masked = jnp.full((n_h, GRID_W, GRID_W), NEG_INF, F32)

    def quadrant(dr):
        return tab[:, dr + NA_KH - 1] if abs(dr) <= NA_KH - 1 else masked

    tiles = [jnp.concatenate([jnp.concatenate([quadrant(2 * (m - (NA_WIN - 1)) + c - a) for c in range(2)], axis=2)
                              for a in range(2)], axis=1) for m in range(2 * (NA_WIN - 1) + 1)]
    sb = jnp.stack(tiles, axis=1)
    return sb.reshape(n_h // NA_HG, NA_HG, *sb.shape[1:])


def _na_mixer(h, g, prm, seq_len):
    t, d = h.shape
    bsz = t // seq_len
    rows = seq_len // GRID_W
    n_pairs = rows // 2
    nblk = n_pairs // NA_PB
    blk = NA_PB * NA_PAIR
    q, kt, v = _na_qkv(h, g, prm['wq'], prm['wkt'], prm['wv'], prm['bq'], prm['bk'], prm['bv'], seq_len,
                       (d // NA_HEADS) ** -0.5)
    q = q.reshape(bsz, seq_len, d)
    v = v.reshape(bsz, seq_len, d)
    prev = lambda pb: jnp.maximum(pb - 1, 0)
    nxt = lambda pb: jnp.minimum(pb + 1, nblk - 1)
    k_spec = lambda f: pl.BlockSpec((None, LANES, blk), lambda gi, b, pb: (b, gi, f(pb)))
    v_spec = lambda f: pl.BlockSpec((None, blk, LANES), lambda gi, b, pb: (b, f(pb), gi))
    same = lambda pb: pb
    sb = prm['bias']
    o = pl.pallas_call(
        functools.partial(_na_attn_body, n_pairs=n_pairs, rows=rows),
        grid=(d // LANES, bsz, nblk),
        in_specs=[v_spec(same), k_spec(prev), k_spec(same), k_spec(nxt), v_spec(prev), v_spec(same), v_spec(nxt),
                  pl.BlockSpec((None,) + sb.shape[1:], lambda gi, b, pb: (gi, 0, 0, 0, 0))],
        out_specs=v_spec(same), out_shape=jax.ShapeDtypeStruct((bsz, seq_len, d), BF16),
        scratch_shapes=[pltpu.VMEM((3 * NA_PB, LANES, NA_PAIR), BF16), pltpu.VMEM((3 * blk, LANES), BF16)],
        compiler_params=_params("arbitrary", "arbitrary", "arbitrary"), name="na_attn")(
            q, kt, kt, kt, v, v, v, sb)
    return _proj_res(o.reshape(t, d), prm['out_w'], prm['out_b'], h)


def _hy_in_body(hp_ref, hc_ref, hn_ref, g_ref, w_ref, b_ref, cw_ref, cb_ref, x0_ref, vv_ref, u_s,
                *, tiles_per_seq, tm):
    d = x0_ref.shape[1]
    xn = _normed_with_halo(hp_ref, hc_ref, hn_ref, g_ref)
    left = (HY_SHORT - 1) // 2

    def conv(c0):
        u = _dot(xn, w_ref[:, c0:c0 + d]) + b_ref[:, c0:c0 + d]
        _store_zero_padded(u_s, u, tiles_per_seq, tm)
        y = cb_ref[:, c0:c0 + d]
        for j in range(HY_SHORT):
            y = y + cw_ref[j:j + 1, c0:c0 + d] * u_s[pl.ds(HALO + j - left, tm), :]
        return y

    x0_ref[...] = conv(0).astype(x0_ref.dtype)
    x1 = conv(d)
    vv_ref[...] = conv(2 * d) * x1


def _hy_in(h, g, w, b, cw, cb, seq_len, tm=TOKEN_TILE):
    t, d = h.shape
    tile = pl.BlockSpec((tm, d), lambda i: (i, 0))
    return pl.pallas_call(
        functools.partial(_hy_in_body, tiles_per_seq=seq_len // tm, tm=tm), grid=(t // tm,),
        in_specs=_halo_specs(tm, d, t) + [_resident((1, d)), _resident(w.shape), _resident((1, 3 * d)),
                                          _resident(cw.shape), _resident((1, 3 * d))],
        out_specs=[tile, tile],
        out_shape=[jax.ShapeDtypeStruct((t, d), BF16), jax.ShapeDtypeStruct((t, d), F32)],
        scratch_shapes=[pltpu.VMEM((tm + 2 * HALO, d), F32)],
        compiler_params=_params("parallel"), name="hy_in")(h, h, h, _row(g), w, _row(b), cw, _row(cb))


def _hy_filter_body(z_ref, w1_ref, b1_ref, w2_ref, b2_ref, w3_ref, b3_ref, fr_ref, wo_ref, dl_ref,
                    k_ref):
    hp = lax.Precision.HIGHEST
    d = k_ref.shape[2]
    z = z_ref[...]
    fr = fr_ref[...]
    a = jnp.sin(fr * (jnp.dot(z, w1_ref[...], precision=hp, preferred_element_type=F32) + b1_ref[...]))
    a = jnp.sin(fr * (jnp.dot(a, w2_ref[...], precision=hp, preferred_element_type=F32) + b2_ref[...]))
    a = jnp.sin(fr * (jnp.dot(a, w3_ref[...], precision=hp, preferred_element_type=F32) + b3_ref[...]))
    k = _dot(a.astype(BF16), wo_ref[...])
    window = jnp.exp(-z[:, 0:1] * dl_ref[...])
    k_ref[0] = k[:, :d] * window
    r = lax.broadcasted_iota(jnp.int32, (z.shape[0], 1), 0)
    k_ref[1] = jnp.where((r == 0) & (pl.program_id(0) == 0), 0.0, k[:, d:] * window)


def _pad_to(x, shape):
    return jnp.pad(x.astype(F32), [(0, s - n) for s, n in zip(shape, x.shape)])


@functools.lru_cache(maxsize=None)
def _hy_positions(length, d):
    bands = (HY_EMB - 1) // 2
    t = np.linspace(0.0, 1.0, length)[:, None]
    ang = (2.0 * math.pi / length) * np.arange(length)[:, None]
    fb = np.linspace(1e-4, bands - 1, bands)[None, :]
    z = np.concatenate([t, np.cos(fb * ang), -np.sin(fb * ang)], axis=-1)
    z = np.pad(z, ((0, 0), (0, LANES - z.shape[1]))).astype(np.float32)
    decay_min = math.log(HY_DECAY_TARGET) / HY_DECAY_PCT_LO
    decay_max = math.log(HY_DECAY_TARGET) / HY_DECAY_PCT_HI
    deltas = np.abs(np.linspace(decay_min, decay_max, d))[None, :].astype(np.float32)
    return z, deltas


def _hy_filters(length, prm, d, tl=TOKEN_TILE):
    z, deltas = _hy_positions(length, d)
    sq = (LANES, LANES)
    args = [z, _pad_to(prm['f_w1'], sq), _pad_to(prm['f_b1'][None], (1, LANES)),
            _pad_to(prm['f_w2'], sq), _pad_to(prm['f_b2'][None], (1, LANES)),
            _pad_to(prm['f_w3'], sq), _pad_to(prm['f_b3'][None], (1, LANES)),
            _pad_to(prm['f_freq'][None], (1, LANES)), _pad_to(prm['f_out'], (LANES, 2 * d)).astype(BF16), deltas]
    return pl.pallas_call(
        _hy_filter_body, grid=(length // tl,),
        in_specs=[pl.BlockSpec((tl, LANES), lambda i: (i, 0))] + [_resident(a.shape) for a in args[1:]],
        out_specs=pl.BlockSpec((2, tl, d), lambda i: (0, i, 0)),
        out_shape=jax.ShapeDtypeStruct((2, length, d), F32),
        compiler_params=_params("parallel"), name="hy_filter")(*args)


def _dft_plan(length):
    n = 2 * length
    n1 = min(128, n // 16)
    return n, n1, n // n1, n1 // 2 + 1


@functools.lru_cache(maxsize=None)
def _dft_tables(length):
    n, n1, n2, n1e = _dft_plan(length)
    n1h, sub = n1 // 2, SUBLANES
    groups = n2 // sub
    k1 = np.arange(n1e)
    t_idx = (n2 * np.arange(n1h)[None, None, None, :] + sub * np.arange(groups)[:, None, None, None]
             + np.arange(sub)[None, None, :, None])
    th = 2.0 * np.pi * ((k1[None, :, None, None] * t_idx) % n) / n

    def expand(m):
        return np.einsum('gksi,st->gksit', m, np.eye(sub)).reshape(groups, n1e * sub, n1h * sub)

    fwd1 = np.concatenate([expand(np.cos(th)), expand(-np.sin(th))], axis=1)
    weight = np.where((k1 == 0) | (k1 == n1 // 2), 1.0, 2.0)[None, :, None, None] / n
    inv1 = np.concatenate([expand(weight * np.cos(th)), expand(-weight * np.sin(th))], axis=1).transpose(0, 2, 1)
    inv1 = np.pad(inv1, ((0, 0), (0, 0), (0, -inv1.shape[2] % LANES)))
    idx = np.arange(n2, dtype=np.int64)
    th2 = 2.0 * np.pi * ((idx[:, None] * idx[None, :]) % n2) / n2
    c2, s2 = np.cos(th2), np.sin(th2)
    fwd2 = np.block([[c2, s2], [-s2, c2]])
    inv2 = np.block([[c2, -s2], [s2, c2]])
    return {name: tab.astype(np.float32).astype(BF16)
            for name, tab in (('fwd1', fwd1), ('inv1', inv1), ('fwd2', fwd2), ('inv2', inv2))}


DFT_ROWS = 2 * SUBLANES


def _dft1_body(x_ref, t_ref, o_ref):
    n1h, _, d = x_ref.shape
    x = x_ref[...]
    halves = [_dot(t_ref[g], x[:, g * SUBLANES:(g + 1) * SUBLANES, :].reshape(n1h * SUBLANES, d).astype(BF16))
              .reshape(o_ref.shape[:2] + (SUBLANES, d)) for g in range(DFT_ROWS // SUBLANES)]
    o_ref[...] = jnp.concatenate(halves, axis=2).astype(o_ref.dtype)


def _dft1(x, length, tabs):
    bx, _, d = x.shape
    n, n1, n2, n1e = _dft_plan(length)
    tab = tabs['fwd1']
    a = pl.pallas_call(
        _dft1_body, grid=(n2 // DFT_ROWS, bx),
        in_specs=[pl.BlockSpec((None, n1 // 2, None, DFT_ROWS, d), lambda j, b: (b, 0, j, 0, 0)),
                  pl.BlockSpec((DFT_ROWS // SUBLANES,) + tab.shape[1:], lambda j, b: (j, 0, 0))],
        out_specs=pl.BlockSpec((None, 2, n1e, None, DFT_ROWS, d), lambda j, b: (b, 0, 0, j, 0, 0)),
        out_shape=jax.ShapeDtypeStruct((bx, 2, n1e, n2 // DFT_ROWS, DFT_ROWS, d), BF16),
        compiler_params=_params("parallel", "arbitrary"), name="hy_dft1")(
            x.reshape(bx, n1 // 2, n2 // DFT_ROWS, DFT_ROWS, d), tab)
    return a.reshape(bx, 2, n1e, n2, d)


def _filter_spectrum_body(a_ref, f_ref, o_ref):
    n2 = o_ref.shape[1]
    d = o_ref.shape[2]
    xf = _dot(f_ref[...], a_ref[0].reshape(2 * n2, d))
    xb = _dot(f_ref[...], a_ref[1].reshape(2 * n2, d))
    o_ref[0] = (xf[:n2] + xb[:n2]).astype(o_ref.dtype)
    o_ref[1] = (xf[n2:] - xb[n2:]).astype(o_ref.dtype)


def _filter_spectrum(k, length, tabs):
    d = k.shape[2]
    n, n1, n2, n1e = _dft_plan(length)
    a = _dft1(k, length, tabs)
    return pl.pallas_call(
        _filter_spectrum_body, grid=(n1e,),
        in_specs=[pl.BlockSpec((2, 2, None, n2, d), lambda k: (0, 0, k, 0, 0)), _resident(tabs['fwd2'].shape)],
        out_specs=pl.BlockSpec((2, None, n2, d), lambda k: (0, k, 0, 0)),
        out_shape=jax.ShapeDtypeStruct((2, n1e, n2, d), BF16),
        compiler_params=_params("parallel"), name="hy_filter_spectrum")(a, tabs['fwd2'])


def _spectral_body(a_ref, k_ref, f_ref, i_ref, o_ref):
    bb, _, n2, d = o_ref.shape
    a = jnp.concatenate([a_ref[b].reshape(2 * n2, d) for b in range(bb)], axis=1)
    x = _dot(f_ref[...], a)
    xr, xi = x[:n2], x[n2:]
    kr = jnp.concatenate([k_ref[0]] * bb, axis=1)
    ki = jnp.concatenate([k_ref[1]] * bb, axis=1)
    y = jnp.concatenate([xr * kr - xi * ki, xr * ki + xi * kr], axis=0).astype(BF16)
    z = _dot(i_ref[...], y)
    for b in range(bb):
        o_ref[b] = z[:, b * d:(b + 1) * d].reshape(2, n2, d).astype(o_ref.dtype)


def _spectral_multiply(a, kc, length, tabs):
    bx = a.shape[0]
    _, n1e, n2, d = kc.shape
    bb = max(b for b in range(1, bx + 1) if bx % b == 0 and b * n2 <= 2 * MXU_DIM)
    blk = pl.BlockSpec((bb, 2, None, n2, d), lambda k, b: (b, 0, k, 0, 0))
    return pl.pallas_call(
        _spectral_body, grid=(n1e, bx // bb),
        in_specs=[blk, pl.BlockSpec((2, None, n2, d), lambda k, b: (0, k, 0, 0)),
                  _resident(tabs['fwd2'].shape), _resident(tabs['inv2'].shape)],
        out_specs=blk, out_shape=jax.ShapeDtypeStruct((bx, 2, n1e, n2, d), BF16),
        compiler_params=_params("parallel", "arbitrary"), name="hy_spectral")(
            a, kc, tabs['fwd2'], tabs['inv2'])


def _idft1_body(z_ref, t_ref, o_ref):
    _, n1e, _, d = z_ref.shape
    rows = 2 * n1e * SUBLANES
    z = z_ref[...].astype(F32)
    halves = []
    for g in range(DFT_ROWS // SUBLANES):
        zg = z[:, :, g * SUBLANES:(g + 1) * SUBLANES, :].reshape(rows, d).astype(BF16)
        zg = jnp.concatenate([zg, jnp.zeros((t_ref.shape[2] - rows, d), BF16)], axis=0)
        halves.append(_dot(t_ref[g], zg).reshape(o_ref.shape[0], SUBLANES, d))
    o_ref[...] = jnp.concatenate(halves, axis=1).astype(o_ref.dtype)


def _idft1(z, length, tabs):
    bx, _, n1e, n2, d = z.shape
    n, n1, _, _ = _dft_plan(length)
    tab = tabs['inv1']
    y = pl.pallas_call(
        _idft1_body, grid=(n2 // DFT_ROWS, bx),
        in_specs=[pl.BlockSpec((None, 2, n1e, None, DFT_ROWS, d), lambda j, b: (b, 0, 0, j, 0, 0)),
                  pl.BlockSpec((DFT_ROWS // SUBLANES,) + tab.shape[1:], lambda j, b: (j, 0, 0))],
        out_specs=pl.BlockSpec((None, n1 // 2, None, DFT_ROWS, d), lambda j, b: (b, 0, j, 0, 0)),
        out_shape=jax.ShapeDtypeStruct((bx, n1 // 2, n2 // DFT_ROWS, DFT_ROWS, d), BF16),
        compiler_params=_params("parallel", "arbitrary"), name="hy_idft1")(
            z.reshape(bx, 2, n1e, n2 // DFT_ROWS, DFT_ROWS, d), tab)
    return y.reshape(bx, length, d)


def _hy_out_body(y_ref, vv_ref, x0_ref, h_ref, sk_ref, w_ref, b_ref, o_ref):
    y = (y_ref[...] + vv_ref[...] * sk_ref[...]) * x0_ref[...]
    o_ref[...] = h_ref[...] + _dot(y.astype(BF16), w_ref[...]) + b_ref[...]


def _hy_out(yc, vv, x0, h, skip, w, b, tm=TOKEN_TILE):
    t, d = h.shape
    tile = pl.BlockSpec((tm, d), lambda i: (i, 0))
    return pl.pallas_call(
        _hy_out_body, grid=(t // tm,),
        in_specs=[tile, tile, tile, tile, _resident((1, d)), _resident((d, d)), _resident((1, d))],
        out_specs=tile, out_shape=jax.ShapeDtypeStruct((t, d), F32),
        compiler_params=_params("parallel"), name="hy_out")(yc, vv, x0, h, _row(skip), w, _row(b))


def _hyena_mixer(h, g, prm, seq_len):
    t, d = h.shape
    bsz = t // seq_len
    tabs = _dft_tables(seq_len)
    x0, vv = _hy_in(h, g, prm['in_w'], prm['in_b'], prm['conv_w'], prm['conv_b'], seq_len)
    kc = _filter_spectrum(_hy_filters(seq_len, prm, d), seq_len, tabs)
    a = _dft1(vv.reshape(bsz, seq_len, d), seq_len, tabs)
    z = _spectral_multiply(a, kc, seq_len, tabs)
    yc = _idft1(z, seq_len, tabs).reshape(t, d)
    return _hy_out(yc, vv, x0, h, prm['skip'], prm['out_w'], prm['out_b'])


def _trunk(x, p, w, depth):
    bsz, seq_len, d = x.shape
    h = x.reshape(bsz * seq_len, d)
    p = p.reshape(depth, bsz * seq_len, -1)
    for i in range(depth):
        kind, j = i % N_MIXERS, i // N_MIXERS
        h = _ffn(h, w['ln_ffn1'][i], w['ffn1_wg'][i], w['ffn1_wu'][i], w['ffn1_wd'][i])
        mix = (_hyena_mixer, _na_mixer, _lru_mixer)[kind]
        h = mix(h, w['ln_mix'][i], w[('hy', 'na', 'lru')[kind]][j], seq_len)
        h = _ffn(h, w['ln_ffn2'][i], w['ffn2_wg'][i], w['ffn2_wu'][i], w['ffn2_wd'][i],
                 ple=(w['ln_ple'][i], w['ple_gate'][i], p, i, w['ple_proj'][i]),
                 final=w['ln_final'] if i == depth - 1 else None)
    return h.reshape(bsz, seq_len, d)


def kernel(x_prompt, x_sample, p_prompt, p_sample, ln_ffn1, ffn1_wg, ffn1_wu, ffn1_wd, ln_mix, ln_ffn2, ffn2_wg, ffn2_wu, ffn2_wd, ln_ple, ple_gate, ple_proj, ln_final, hy_in_w, hy_in_b, hy_conv_w, hy_conv_b, hy_f_w1, hy_f_b1, hy_f_w2, hy_f_b2, hy_f_w3, hy_f_b3, hy_f_freq, hy_f_out, hy_skip, hy_out_w, hy_out_b, na_qkv_w, na_qkv_b, na_rpb, na_out_w, na_out_b, lru_in_w, lru_in_b, lru_conv_w, lru_conv_b, lru_wa, lru_ba, lru_wx, lru_bx, lru_lambda, lru_out_w, lru_out_b):
    depth = ln_ffn1.shape[0]
    d = x_prompt.shape[-1]
    b16 = lambda a: a.astype(BF16)
    w = {
        'ln_ffn1': ln_ffn1, 'ffn1_wg': b16(ffn1_wg), 'ffn1_wu': b16(ffn1_wu), 'ffn1_wd': b16(ffn1_wd),
        'ln_mix': ln_mix, 'ln_ffn2': ln_ffn2, 'ffn2_wg': b16(ffn2_wg), 'ffn2_wu': b16(ffn2_wu),
        'ffn2_wd': b16(ffn2_wd), 'ln_ple': ln_ple, 'ple_gate': b16(ple_gate), 'ple_proj': b16(ple_proj),
        'ln_final': ln_final,
        'hy': [dict(in_w=b16(hy_in_w[j]), in_b=hy_in_b[j], conv_w=hy_conv_w[j], conv_b=hy_conv_b[j],
                    f_w1=hy_f_w1[j], f_b1=hy_f_b1[j], f_w2=hy_f_w2[j], f_b2=hy_f_b2[j], f_w3=hy_f_w3[j],
                    f_b3=hy_f_b3[j], f_freq=hy_f_freq[j], f_out=hy_f_out[j], skip=hy_skip[j],
                    out_w=b16(hy_out_w[j]), out_b=hy_out_b[j]) for j in range(hy_in_w.shape[0])],
        'na': [dict(wq=b16(na_qkv_w[j][:, :d]), wkt=b16(na_qkv_w[j][:, d:2 * d].T), wv=b16(na_qkv_w[j][:, 2 * d:]),
                    bq=na_qkv_b[j][:d], bk=na_qkv_b[j][d:2 * d], bv=na_qkv_b[j][2 * d:],
                    bias=_na_bias_table(na_rpb[j]), out_w=b16(na_out_w[j]), out_b=na_out_b[j])
               for j in range(na_qkv_w.shape[0])],
        'lru': [dict(in_w=b16(lru_in_w[j]), in_b=lru_in_b[j], conv_w=lru_conv_w[j], conv_b=lru_conv_b[j],
                     wa=b16(lru_wa[j]), ba=lru_ba[j], wx=b16(lru_wx[j]), bx=lru_bx[j],
                     c8=-LRU_C * jax.nn.softplus(-lru_lambda[j].astype(F32)),
                     out_w=b16(lru_out_w[j]), out_b=lru_out_b[j]) for j in range(lru_in_w.shape[0])],
    }
    return _trunk(x_prompt, p_prompt, w, depth), _trunk(x_sample, p_sample, w, depth)
```

```python
import functools
import math

import numpy as np
import jax
import jax.numpy as jnp
from jax import lax
from jax.experimental import pallas as pl
from jax.experimental.pallas import tpu as pltpu

F32 = jnp.float32
BF16 = jnp.bfloat16

RMS_EPS = 1e-6
NEG_INF = -1e30
LANES = 128
SUBLANES = 8
MXU_DIM = 256
VMEM_LIMIT = 56 * 1024 * 1024

GRID_W = 64
NA_HEADS = 32
NA_KH = 8
NA_KW = 16
LRU_BLOCKS = 4
LRU_CONV = 4
LRU_C = 8.0
HY_EMB = 33
HY_SHORT = 3
HY_PACK = LANES // 2
HY_DECAY_TARGET = 1e-2
HY_DECAY_PCT_HI = 0.3
HY_DECAY_PCT_LO = 1.5
N_MIXERS = 3

TOKEN_TILE = 512
FFN_TILE = 1024
HALO = SUBLANES


def _params(*sem):
    return pltpu.CompilerParams(dimension_semantics=sem, vmem_limit_bytes=VMEM_LIMIT)


def _resident(shape):
    nd = len(shape)
    return pl.BlockSpec(shape, lambda *_: (0,) * nd, pipeline_mode=pl.Buffered(1))


def _rms(x, g):
    ms = jnp.mean(x * x, axis=-1, keepdims=True)
    return x * lax.rsqrt(ms + RMS_EPS) * g


def _dot(a, b):
    return jnp.dot(a, b, preferred_element_type=F32)


def _sigmoid(x):
    return 0.5 * jnp.tanh(0.5 * x) + 0.5


def _row(v):
    return v.reshape(1, -1).astype(F32)


def _ffn_body(*refs, ple, final, n_chunks):
    h_ref, g_ref, wg_ref, wu_ref, wd_ref = refs[:5]
    o_ref = refs[-1]
    h = h_ref[...]
    xn = _rms(h, g_ref[...]).astype(BF16)
    n_tiles = pl.cdiv(wg_ref.shape[1], MXU_DIM)
    bounds = [min(-(-n_tiles * c // n_chunks) * MXU_DIM, wg_ref.shape[1]) for c in range(n_chunks + 1)]
    acc = jnp.zeros_like(h)
    for c0, c1 in zip(bounds[:-1], bounds[1:]):
        gt = _dot(xn, wg_ref[:, c0:c1])
        up = _dot(xn, wu_ref[:, c0:c1])
        act = (gt * jax.nn.sigmoid(gt) * up).astype(BF16)
        acc = acc + _dot(act, wd_ref[c0:c1, :])
    h = h + 0.5 * acc
    k = 5
    if ple:
        gp_ref, wpg_ref, p_ref, wpp_ref = refs[k:k + 4]
        k += 4
        gate = jax.nn.sigmoid(_dot(_rms(h, gp_ref[...]).astype(BF16), wpg_ref[...]))
        h = h + gate * _dot(p_ref[...].astype(BF16), wpp_ref[...])
    if final:
        h = _rms(h, refs[k][...])
    o_ref[...] = h


def _ffn(h, g, wg, wu, wd, ple=None, final=None, tm=FFN_TILE):
    t, d = h.shape
    d_ff = wg.shape[1]
    tile = pl.BlockSpec((tm, d), lambda i: (i, 0))
    args = [h, _row(g), wg, wu, wd]
    specs = [tile, _resident((1, d)), _resident((d, d_ff)), _resident((d, d_ff)), _resident((d_ff, d))]
    if ple is not None:
        gp, wpg, p, layer, wpp = ple
        args += [_row(gp), wpg, p, wpp]
        specs += [_resident((1, d)), _resident((d, d)),
                  pl.BlockSpec((None, tm, p.shape[2]), lambda i: (layer, i, 0)), _resident(wpp.shape)]
    if final is not None:
        args.append(_row(final))
        specs.append(_resident((1, d)))
    return pl.pallas_call(
        functools.partial(_ffn_body, ple=ple is not None, final=final is not None, n_chunks=2),
        grid=(t // tm,), in_specs=specs, out_specs=tile,
        out_shape=jax.ShapeDtypeStruct((t, d), F32),
        compiler_params=_params("parallel"), name="ffn")(*args)


def _halo_specs(tm, d, n_rows):
    per = tm // HALO
    last = n_rows // HALO - 1
    return [pl.BlockSpec((HALO, d), lambda i: (jnp.maximum(i * per - 1, 0), 0)),
            pl.BlockSpec((tm, d), lambda i: (i, 0)),
            pl.BlockSpec((HALO, d), lambda i: (jnp.minimum((i + 1) * per, last), 0))]


def _normed_with_halo(hp_ref, hc_ref, hn_ref, g_ref):
    x = jnp.concatenate([hp_ref[...], hc_ref[...], hn_ref[...]], axis=0)
    return _rms(x, g_ref[...]).astype(BF16)


def _store_zero_padded(u_s, u, tiles_per_seq, tm):
    i = pl.program_id(0) % tiles_per_seq
    u_s[HALO:tm + HALO, :] = u[HALO:tm + HALO]
    u_s[0:HALO, :] = jnp.where(i == 0, 0.0, u[0:HALO])
    u_s[tm + HALO:tm + 2 * HALO, :] = jnp.where(i == tiles_per_seq - 1, 0.0, u[tm + HALO:tm + 2 * HALO])


def _proj_res_body(x_ref, w_ref, b_ref, h_ref, o_ref):
    o_ref[...] = h_ref[...] + _dot(x_ref[...], w_ref[...]) + b_ref[...]


def _proj_res(x, w, b, h, tm=TOKEN_TILE):
    t, d = h.shape
    tile = pl.BlockSpec((tm, d), lambda i: (i, 0))
    return pl.pallas_call(
        _proj_res_body, grid=(t // tm,),
        in_specs=[pl.BlockSpec((tm, x.shape[1]), lambda i: (i, 0)), _resident(w.shape), _resident((1, d)), tile],
        out_specs=tile, out_shape=jax.ShapeDtypeStruct((t, d), F32),
        compiler_params=_params("parallel"), name="proj_res")(x, w, _row(b), h)


def _lru_in_body(hp_ref, hc_ref, hn_ref, g_ref, w_ref, b_ref, cw_ref, cb_ref, gb_ref, xb_ref, u_s,
                 *, tiles_per_seq, tm):
    r_w = gb_ref.shape[1]
    xn = _normed_with_halo(hp_ref, hc_ref, hn_ref, g_ref)
    gb_ref[...] = jax.nn.gelu(_dot(xn[HALO:HALO + tm], w_ref[:, :r_w]) + b_ref[:, :r_w]).astype(BF16)
    u = _dot(xn, w_ref[:, r_w:]) + b_ref[:, r_w:]
    _store_zero_padded(u_s, u, tiles_per_seq, tm)
    left = LRU_CONV // 2
    y = cb_ref[...]
    for j in range(LRU_CONV):
        y = y + cw_ref[j:j + 1, :] * u_s[pl.ds(HALO + j - left, tm), :]
    xb_ref[...] = y


def _lru_in(h, g, w, b, cw, cb, seq_len, tm=TOKEN_TILE):
    t, d = h.shape
    r_w = cw.shape[1]
    tile = pl.BlockSpec((tm, r_w), lambda i: (i, 0))
    return pl.pallas_call(
        functools.partial(_lru_in_body, tiles_per_seq=seq_len // tm, tm=tm),
        grid=(t // tm,),
        in_specs=_halo_specs(tm, d, t) + [_resident((1, d)), _resident(w.shape), _resident((1, 2 * r_w)),
                                          _resident(cw.shape), _resident((1, r_w))],
        out_specs=[tile, tile],
        out_shape=[jax.ShapeDtypeStruct((t, r_w), BF16), jax.ShapeDtypeStruct((t, r_w), F32)],
        scratch_shapes=[pltpu.VMEM((tm + 2 * HALO, r_w), F32)],
        compiler_params=_params("parallel"), name="lru_in")(h, h, h, _row(g), w, _row(b), cw, _row(cb))


def _block_diag(x, w_ref):
    nb, bw, _ = w_ref.shape
    return jnp.concatenate([_dot(x[:, n * bw:(n + 1) * bw], w_ref[n]) for n in range(nb)], axis=1)


def _lru_scan_tile(a, b, a_s, b_s, h_s, carry_ref, reverse):
    tm, width = a.shape
    n_groups = tm // SUBLANES
    a = a.reshape(n_groups, SUBLANES, width)
    b = b.reshape(n_groups, SUBLANES, width)
    row = lax.broadcasted_iota(jnp.int32, (1, SUBLANES, 1), 1)
    for dist in (1, 2, 4):
        if reverse:
            shift, ok = SUBLANES - dist, row < SUBLANES - dist
        else:
            shift, ok = dist, row >= dist
        a_prev = jnp.where(ok, pltpu.roll(a, shift, 1), 1.0)
        b_prev = jnp.where(ok, pltpu.roll(b, shift, 1), 0.0)
        b = a * b_prev + b
        a = a * a_prev
    a_s[...] = a.reshape(tm, width)
    b_s[...] = b.reshape(tm, width)
    edge = 0 if reverse else SUBLANES - 1

    def group(k, carry):
        gi = n_groups - 1 - k if reverse else k
        r0 = pl.multiple_of(gi * SUBLANES, SUBLANES)
        hg = a_s[pl.ds(r0, SUBLANES), :] * carry + b_s[pl.ds(r0, SUBLANES), :]
        h_s[pl.ds(r0, SUBLANES), :] = hg
        return hg[edge:edge + 1, :]

    carry_ref[...] = lax.fori_loop(0, n_groups, group, carry_ref[...], unroll=True)


def _lru_gates(xb, wa_ref, ba_ref, wx_ref, bx_ref, c8_ref, first_row):
    xb16 = xb.astype(BF16)
    gate_x = _sigmoid(_block_diag(xb16, wx_ref) + bx_ref[...])
    gate_a = _sigmoid(_block_diag(xb16, wa_ref) + ba_ref[...])
    log_a = gate_a * c8_ref[...]
    a = jnp.exp(log_a)
    y = 1.0 - a * a
    mult = jnp.where(y > 0.0, y * lax.rsqrt(y), 0.0)
    r = lax.broadcasted_iota(jnp.int32, (xb.shape[0], 1), 0)
    mult = jnp.where(r == first_row, 1.0, mult)
    return a, mult * gate_x * xb


def _lru_fwd_body(xb_ref, wa_ref, ba_ref, wx_ref, bx_ref, c8_ref, h0_ref, a_s, b_s, carry, *, tm):
    ti = pl.program_id(1)

    @pl.when(ti == 0)
    def _():
        carry[...] = jnp.zeros_like(carry)

    first_row = jnp.where(ti == 0, 0, -1)
    a, b = _lru_gates(xb_ref[...], wa_ref, ba_ref, wx_ref, bx_ref, c8_ref, first_row)
    _lru_scan_tile(a, b, a_s, b_s, h0_ref, carry, reverse=False)


def _lru_bwd_body(xb_ref, wa_ref, ba_ref, wx_ref, bx_ref, c8_ref, h0_ref, gb_ref, h_ref, wo_ref, bo_ref,
                  o_ref, a_s, b_s, h1_s, carry, *, tm):
    ti = pl.program_id(1)

    @pl.when(ti == 0)
    def _():
        carry[...] = jnp.zeros_like(carry)

    first_row = jnp.where(ti == 0, tm - 1, -1)
    a, b = _lru_gates(xb_ref[...], wa_ref, ba_ref, wx_ref, bx_ref, c8_ref, first_row)
    _lru_scan_tile(a, b, a_s, b_s, h1_s, carry, reverse=True)
    y = h0_ref[...] + h1_s[...]
    z = (gb_ref[...].astype(F32) * y).astype(BF16)
    o_ref[...] = h_ref[...] + _dot(z, wo_ref[...]) + bo_ref[...]


def _lru_mixer(h, g, prm, seq_len, tm=TOKEN_TILE):
    t, d = h.shape
    bsz = t // seq_len
    nt = seq_len // tm
    gb, xb = _lru_in(h, g, prm['in_w'], prm['in_b'], prm['conv_w'], prm['conv_b'], seq_len, tm)
    r_w = xb.shape[1]
    bw = r_w // LRU_BLOCKS
    gate_specs = lambda dr: [_resident((LRU_BLOCKS, bw, bw)), _resident((1, r_w)),
                             _resident((LRU_BLOCKS, bw, bw)), _resident((1, r_w)), _resident((1, r_w))]
    gate_args = lambda dr: [prm['wa'][dr], _row(prm['ba'][dr]), prm['wx'][dr], _row(prm['bx'][dr]),
                            _row(prm['c8'][dr])]
    fwd = lambda b, i: (b * nt + i, 0)
    bwd = lambda b, i: (b * nt + nt - 1 - i, 0)
    scan_scratch = [pltpu.VMEM((tm, r_w), F32), pltpu.VMEM((tm, r_w), F32)]
    h0 = pl.pallas_call(
        functools.partial(_lru_fwd_body, tm=tm), grid=(bsz, nt),
        in_specs=[pl.BlockSpec((tm, r_w), fwd)] + gate_specs(0),
        out_specs=pl.BlockSpec((tm, r_w), fwd), out_shape=jax.ShapeDtypeStruct((t, r_w), F32),
        scratch_shapes=scan_scratch + [pltpu.VMEM((1, r_w), F32)],
        compiler_params=_params("arbitrary", "arbitrary"), name="lru_fwd")(xb, *gate_args(0))
    return pl.pallas_call(
        functools.partial(_lru_bwd_body, tm=tm), grid=(bsz, nt),
        in_specs=[pl.BlockSpec((tm, r_w), bwd)] + gate_specs(1)
        + [pl.BlockSpec((tm, r_w), bwd), pl.BlockSpec((tm, r_w), bwd), pl.BlockSpec((tm, d), bwd),
           _resident((r_w, d)), _resident((1, d))],
        out_specs=pl.BlockSpec((tm, d), bwd), out_shape=jax.ShapeDtypeStruct((t, d), F32),
        scratch_shapes=scan_scratch + [pltpu.VMEM((tm, r_w), F32), pltpu.VMEM((1, r_w), F32)],
        compiler_params=_params("arbitrary", "arbitrary"), name="lru_bwd")(
            xb, *gate_args(1), h0, gb, h, prm['out_w'], _row(prm['out_b']))


NA_PAIR = 2 * GRID_W
NA_WIN = 5
NA_PB = 4
NA_HG = LANES // (1024 // NA_HEADS)


def _na_qkv_body(h_ref, g_ref, wq_ref, wkt_ref, wv_ref, bq_ref, bk_ref, bv_ref, q_ref, kt_ref, v_ref, *, scale):
    xn = _rms(h_ref[...], g_ref[...]).astype(BF16)
    q_ref[...] = ((_dot(xn, wq_ref[...]) + bq_ref[...]) * scale).astype(BF16)
    kt = lax.dot_general(wkt_ref[...], xn, (((1,), (1,)), ((), ())), preferred_element_type=F32)
    kt_ref[...] = (kt + bk_ref[...]).astype(BF16)
    v_ref[...] = (_dot(xn, wv_ref[...]) + bv_ref[...]).astype(BF16)


def _na_qkv(h, g, wq, wkt, wv, bq, bk, bv, seq_len, scale, tm=TOKEN_TILE):
    t, d = h.shape
    bsz, nt = t // seq_len, seq_len // tm
    tile = pl.BlockSpec((tm, d), lambda i: (i, 0))
    return pl.pallas_call(
        functools.partial(_na_qkv_body, scale=scale), grid=(t // tm,),
        in_specs=[tile, _resident((1, d)), _resident((d, d)), _resident((d, d)), _resident((d, d)),
                  _resident((1, d)), _resident((d, 1)), _resident((1, d))],
        out_specs=[tile, pl.BlockSpec((None, d, tm), lambda i: (i // nt, 0, i % nt)), tile],
        out_shape=[jax.ShapeDtypeStruct((t, d), BF16), jax.ShapeDtypeStruct((bsz, d, seq_len), BF16),
                   jax.ShapeDtypeStruct((t, d), BF16)],
        compiler_params=_params("parallel"), name="na_qkv")(
            h, _row(g), wq, wkt, wv, _row(bq), bk.reshape(d, 1).astype(F32), _row(bv))


def _na_attn_body(q_ref, kp_ref, kc_ref, kn_ref, vp_ref, vc_ref, vn_ref, sb_ref, o_ref, kwin, vwin,
                  *, n_pairs, rows):
    pb = pl.program_id(2)
    blk = NA_PB * NA_PAIR
    for bi, (k_ref, v_ref) in enumerate(((kp_ref, vp_ref), (kc_ref, vc_ref), (kn_ref, vn_ref))):
        for t in range(NA_PB):
            kwin[bi * NA_PB + t] = k_ref[:, t * NA_PAIR:(t + 1) * NA_PAIR]
        vwin[bi * blk:(bi + 1) * blk, :] = v_ref[...]
    dh = LANES // NA_HG
    pad = 2 * SUBLANES
    lane = lax.broadcasted_iota(jnp.int32, (1, LANES), 1)
    in_head = [(lane // dh) == j for j in range(NA_HG)]
    q_row_onehot = (lax.broadcasted_iota(jnp.int32, (NA_PAIR, pad), 0) // GRID_W
                    == lax.broadcasted_iota(jnp.int32, (NA_PAIR, pad), 1)).astype(BF16)
    mask_row = lax.broadcasted_iota(jnp.int32, (pad, 1), 0)
    k_row_in_win = lax.broadcasted_iota(jnp.int32, (1, NA_WIN * NA_PAIR), 1) // GRID_W

    def pair(i, carry):
        p = pb * NA_PB + i
        kp0 = jnp.clip(p - 2, 0, n_pairs - NA_WIN)
        off = kp0 - (pb - 1) * NA_PB
        m0 = kp0 - p + (NA_WIN - 1)
        qi = q_ref[pl.ds(pl.multiple_of(i * NA_PAIR, NA_PAIR), NA_PAIR), :]
        ksl = jnp.concatenate([kwin[off + jp] for jp in range(NA_WIN)], axis=1)
        vsl = vwin[pl.ds(pl.multiple_of(off * NA_PAIR, NA_PAIR), NA_WIN * NA_PAIR), :]
        r_start = jnp.clip(2 * p + mask_row - NA_KH // 2, 0, rows - NA_KH)
        k_row = 2 * kp0 + k_row_in_win
        in_window = (k_row >= r_start) & (k_row < r_start + NA_KH)
        row_mask = jnp.where(in_window | (mask_row >= 2), 0.0, NEG_INF).astype(BF16)
        q_heads = jnp.concatenate(
            [jnp.concatenate([jnp.where(m, qi, jnp.zeros_like(qi)), q_row_onehot], axis=1) for m in in_head], axis=0)
        bias = jnp.concatenate(
            [jnp.concatenate([sb_ref[j, m0 + jp] for jp in range(NA_WIN)], axis=1) for j in range(NA_HG)], axis=0)
        s = _dot(q_heads, jnp.concatenate([ksl, row_mask], axis=0)) + bias
        e = jnp.exp(s - jnp.max(s, axis=-1, keepdims=True))
        denom = jnp.sum(e, axis=-1, keepdims=True)
        pv = _dot(e.astype(BF16), vsl) / denom
        acc = jnp.zeros((NA_PAIR, LANES), F32)
        for j, m in enumerate(in_head):
            acc = acc + jnp.where(m, pv[j * NA_PAIR:(j + 1) * NA_PAIR], 0.0)
        o_ref[pl.ds(pl.multiple_of(i * NA_PAIR, NA_PAIR), NA_PAIR), :] = acc.astype(BF16)
        return carry

    lax.fori_loop(0, NA_PB, pair, 0, unroll=True)


def _na_bias_table(rpb):
    n_h, n_dr, n_dc = rpb.shape
    col = np.arange(GRID_W)
    q_start = np.clip(col - NA_KW // 2, 0, GRID_W - NA_KW)
    col_ok = (col[None, :] >= q_start[:, None]) & (col[None, :] < q_start[:, None] + NA_KW)
    dc_idx = col[None, :] - col[:, None] + NA_KW - 1
    onehot = ((np.arange(n_dc)[:, None, None] == dc_idx[None]) & col_ok[None]).reshape(n_dc, -1)
    tab = jnp.dot(rpb.astype(F32).reshape(n_h * n_dr, n_dc), jnp.asarray(onehot, F32),
                  precision=lax.Precision.HIGHEST)
    tab = tab.reshape(n_h, n_dr, GRID_W, GRID_W) + jnp.where(col_ok, 0.0, NEG_INF).astype(F32)
    masked = jnp.full((n_h, GRID_W, GRID_W), NEG_INF, F32)

    def quadrant(dr):
        return tab[:, dr + NA_KH - 1] if abs(dr) <= NA_KH - 1 else masked

    tiles = [jnp.concatenate([jnp.concatenate([quadrant(2 * (m - (NA_WIN - 1)) + c - a) for c in range(2)], axis=2)
                              for a in range(2)], axis=1) for m in range(2 * (NA_WIN - 1) + 1)]
    sb = jnp.stack(tiles, axis=1)
    return sb.reshape(n_h // NA_HG, NA_HG, *sb.shape[1:])


def _na_mixer(h, g, prm, seq_len):
    t, d = h.shape
    bsz = t // seq_len
    rows = seq_len // GRID_W
    n_pairs = rows // 2
    nblk = n_pairs // NA_PB
    blk = NA_PB * NA_PAIR
    q, kt, v = _na_qkv(h, g, prm['wq'], prm['wkt'], prm['wv'], prm['bq'], prm['bk'], prm['bv'], seq_len,
                       (d // NA_HEADS) ** -0.5)
    q = q.reshape(bsz, seq_len, d)
    v = v.reshape(bsz, seq_len, d)
    prev = lambda pb: jnp.maximum(pb - 1, 0)
    nxt = lambda pb: jnp.minimum(pb + 1, nblk - 1)
    k_spec = lambda f: pl.BlockSpec((None, LANES, blk), lambda gi, b, pb: (b, gi, f(pb)))
    v_spec = lambda f: pl.BlockSpec((None, blk, LANES), lambda gi, b, pb: (b, f(pb), gi))
    same = lambda pb: pb
    sb = prm['bias']
    o = pl.pallas_call(
        functools.partial(_na_attn_body, n_pairs=n_pairs, rows=rows),
        grid=(d // LANES, bsz, nblk),
        in_specs=[v_spec(same), k_spec(prev), k_spec(same), k_spec(nxt), v_spec(prev), v_spec(same), v_spec(nxt),
                  pl.BlockSpec((None,) + sb.shape[1:], lambda gi, b, pb: (gi, 0, 0, 0, 0))],
        out_specs=v_spec(same), out_shape=jax.ShapeDtypeStruct((bsz, seq_len, d), BF16),
        scratch_shapes=[pltpu.VMEM((3 * NA_PB, LANES, NA_PAIR), BF16), pltpu.VMEM((3 * blk, LANES), BF16)],
        compiler_params=_params("arbitrary", "arbitrary", "arbitrary"), name="na_attn")(
            q, kt, kt, kt, v, v, v, sb)
    return _proj_res(o.reshape(t, d), prm['out_w'], prm['out_b'], h)


def _hy_in_body(hp_ref, hc_ref, hn_ref, g_ref, w_ref, b_ref, cw_ref, cb_ref, x0_ref, vv_ref, u_s,
                *, tiles_per_seq, tm):
    d = x0_ref.shape[1]
    xn = _normed_with_halo(hp_ref, hc_ref, hn_ref, g_ref)
    left = (HY_SHORT - 1) // 2

    def conv(c0):
        u = _dot(xn, w_ref[:, c0:c0 + d]) + b_ref[:, c0:c0 + d]
        _store_zero_padded(u_s, u, tiles_per_seq, tm)
        y = cb_ref[:, c0:c0 + d]
        for j in range(HY_SHORT):
            y = y + cw_ref[j:j + 1, c0:c0 + d] * u_s[pl.ds(HALO + j - left, tm), :]
        return y

    x0_ref[...] = conv(0).astype(x0_ref.dtype)
    x1 = conv(d)
    vv_ref[...] = conv(2 * d) * x1


def _hy_in(h, g, w, b, cw, cb, seq_len, tm=TOKEN_TILE):
    t, d = h.shape
    tile = pl.BlockSpec((tm, d), lambda i: (i, 0))
    return pl.pallas_call(
        functools.partial(_hy_in_body, tiles_per_seq=seq_len // tm, tm=tm), grid=(t // tm,),
        in_specs=_halo_specs(tm, d, t) + [_resident((1, d)), _resident(w.shape), _resident((1, 3 * d)),
                                          _resident(cw.shape), _resident((1, 3 * d))],
        out_specs=[tile, tile],
        out_shape=[jax.ShapeDtypeStruct((t, d), BF16), jax.ShapeDtypeStruct((t, d), F32)],
        scratch_shapes=[pltpu.VMEM((tm + 2 * HALO, d), F32)],
        compiler_params=_params("parallel"), name="hy_in")(h, h, h, _row(g), w, _row(b), cw, _row(cb))


def _hy_filter_body(z_ref, w1_ref, b1_ref, w2_ref, b2_ref, w3_ref, b3_ref, fr_ref, wo_ref, dl_ref,
                    k_ref):
    hp = lax.Precision.HIGHEST
    half, d = z_ref.shape[0], k_ref.shape[2]
    z = z_ref[...]
    fr = fr_ref[...]
    a = jnp.sin(fr * (jnp.dot(z, w1_ref[...], precision=hp, preferred_element_type=F32) + b1_ref[...]))
    a = jnp.sin(fr * (jnp.dot(a, w2_ref[...], precision=hp, preferred_element_type=F32) + b2_ref[...]))
    a = jnp.sin(fr * (jnp.dot(a, w3_ref[...], precision=hp, preferred_element_type=F32) + b3_ref[...]))
    a = a.astype(BF16)
    r = lax.broadcasted_iota(jnp.int32, (half, 1), 0)
    for part in range(2):
        k = _dot(a, wo_ref[part])
        window = jnp.exp(-z[:, part * HY_PACK:part * HY_PACK + 1] * dl_ref[...])
        rows = slice(part * half, (part + 1) * half)
        k_ref[0, rows, :] = k[:, :d] * window
        kb = k[:, d:] * window
        if part == 0:
            kb = jnp.where((r == 0) & (pl.program_id(0) == 0), 0.0, kb)
        k_ref[1, rows, :] = kb


def _block_diag2(x):
    zero = jnp.zeros_like(x)
    return jnp.concatenate([jnp.concatenate([x, zero], axis=-1), jnp.concatenate([zero, x], axis=-1)], axis=-2)


def _pad_to(x, shape):
    return jnp.pad(x.astype(F32), [(0, s - n) for s, n in zip(shape, x.shape)])


@functools.lru_cache(maxsize=None)
def _hy_positions(length, d, tl):
    bands = (HY_EMB - 1) // 2
    t = np.linspace(0.0, 1.0, length)[:, None]
    ang = (2.0 * math.pi / length) * np.arange(length)[:, None]
    fb = np.linspace(1e-4, bands - 1, bands)[None, :]
    z = np.concatenate([t, np.cos(fb * ang), -np.sin(fb * ang)], axis=-1)
    z = np.pad(z, ((0, 0), (0, HY_PACK - z.shape[1]))).reshape(length // tl, 2, tl // 2, HY_PACK)
    z = z.transpose(0, 2, 1, 3).reshape(length // 2, 2 * HY_PACK).astype(np.float32)
    decay_min = math.log(HY_DECAY_TARGET) / HY_DECAY_PCT_LO
    decay_max = math.log(HY_DECAY_TARGET) / HY_DECAY_PCT_HI
    deltas = np.abs(np.linspace(decay_min, decay_max, d))[None, :].astype(np.float32)
    return z, deltas


def _hy_filters(length, prm, d, tl=TOKEN_TILE):
    z, deltas = _hy_positions(length, d, tl)
    sq = (HY_PACK, HY_PACK)
    twice = lambda v: jnp.tile(_pad_to(v[None], (1, HY_PACK)), (1, 2))
    w_out = _pad_to(prm['f_out'], (HY_PACK, 2 * d))
    w_out = jnp.stack([jnp.concatenate([w_out, jnp.zeros_like(w_out)]),
                       jnp.concatenate([jnp.zeros_like(w_out), w_out])]).astype(BF16)
    args = [z, _block_diag2(_pad_to(prm['f_w1'], sq)), twice(prm['f_b1']),
            _block_diag2(_pad_to(prm['f_w2'], sq)), twice(prm['f_b2']),
            _block_diag2(_pad_to(prm['f_w3'], sq)), twice(prm['f_b3']), twice(prm['f_freq']), w_out, deltas]
    return pl.pallas_call(
        _hy_filter_body, grid=(length // tl,),
        in_specs=[pl.BlockSpec((tl // 2, 2 * HY_PACK), lambda i: (i, 0))] + [_resident(a.shape) for a in args[1:]],
        out_specs=pl.BlockSpec((2, tl, d), lambda i: (0, i, 0)),
        out_shape=jax.ShapeDtypeStruct((2, length, d), F32),
        compiler_params=_params("parallel"), name="hy_filter")(*args)


def _dft_plan(length):
    n = 2 * length
    n1 = min(128, n // 16)
    return n, n1, n // n1, n1 // 2 + 1


@functools.lru_cache(maxsize=None)
def _dft_tables(length):
    n, n1, n2, n1e = _dft_plan(length)
    n1h, sub = n1 // 2, SUBLANES
    groups = n2 // sub
    k1 = np.arange(n1e)
    t_idx = (n2 * np.arange(n1h)[None, None, None, :] + sub * np.arange(groups)[:, None, None, None]
             + np.arange(sub)[None, None, :, None])
    th = 2.0 * np.pi * ((k1[None, :, None, None] * t_idx) % n) / n

    def expand(m):
        return np.einsum('gksi,st->gksit', m, np.eye(sub)).reshape(groups, n1e * sub, n1h * sub)

    fwd1 = np.concatenate([expand(np.cos(th)), expand(-np.sin(th))], axis=1)
    weight = np.where((k1 == 0) | (k1 == n1 // 2), 1.0, 2.0)[None, :, None, None] / n
    inv1 = np.concatenate([expand(weight * np.cos(th)), expand(-weight * np.sin(th))], axis=1).transpose(0, 2, 1)
    inv1 = np.pad(inv1, ((0, 0), (0, 0), (0, -inv1.shape[2] % LANES)))
    idx = np.arange(n2, dtype=np.int64)
    th2 = 2.0 * np.pi * ((idx[:, None] * idx[None, :]) % n2) / n2
    c2, s2 = np.cos(th2), np.sin(th2)
    fwd2 = np.block([[c2, s2], [-s2, c2]])
    inv2 = np.block([[c2, -s2], [s2, c2]])
    return {name: tab.astype(np.float32).astype(BF16)
            for name, tab in (('fwd1', fwd1), ('inv1', inv1), ('fwd2', fwd2), ('inv2', inv2))}


DFT_ROWS = 2 * SUBLANES


def _dft1_body(x_ref, t_ref, o_ref):
    n1h, _, d = x_ref.shape
    x = x_ref[...]
    halves = [_dot(t_ref[g], x[:, g * SUBLANES:(g + 1) * SUBLANES, :].reshape(n1h * SUBLANES, d).astype(BF16))
              .reshape(o_ref.shape[:2] + (SUBLANES, d)) for g in range(DFT_ROWS // SUBLANES)]
    o_ref[...] = jnp.concatenate(halves, axis=2).astype(o_ref.dtype)


def _dft1(x, length, tabs):
    bx, _, d = x.shape
    n, n1, n2, n1e = _dft_plan(length)
    tab = tabs['fwd1']
    a = pl.pallas_call(
        _dft1_body, grid=(n2 // DFT_ROWS, bx),
        in_specs=[pl.BlockSpec((None, n1 // 2, None, DFT_ROWS, d), lambda j, b: (b, 0, j, 0, 0)),
                  pl.BlockSpec((DFT_ROWS // SUBLANES,) + tab.shape[1:], lambda j, b: (j, 0, 0))],
        out_specs=pl.BlockSpec((None, 2, n1e, None, DFT_ROWS, d), lambda j, b: (b, 0, 0, j, 0, 0)),
        out_shape=jax.ShapeDtypeStruct((bx, 2, n1e, n2 // DFT_ROWS, DFT_ROWS, d), BF16),
        compiler_params=_params("parallel", "arbitrary"), name="hy_dft1")(
            x.reshape(bx, n1 // 2, n2 // DFT_ROWS, DFT_ROWS, d), tab)
    return a.reshape(bx, 2, n1e, n2, d)


def _filter_spectrum_body(a_ref, f_ref, o_ref):
    n2 = o_ref.shape[1]
    d = o_ref.shape[2]
    xf = _dot(f_ref[...], a_ref[0].reshape(2 * n2, d))
    xb = _dot(f_ref[...], a_ref[1].reshape(2 * n2, d))
    o_ref[0] = (xf[:n2] + xb[:n2]).astype(o_ref.dtype)
    o_ref[1] = (xf[n2:] - xb[n2:]).astype(o_ref.dtype)


def _filter_spectrum(k, length, tabs):
    d = k.shape[2]
    n, n1, n2, n1e = _dft_plan(length)
    a = _dft1(k, length, tabs)
    return pl.pallas_call(
        _filter_spectrum_body, grid=(n1e,),
        in_specs=[pl.BlockSpec((2, 2, None, n2, d), lambda k: (0, 0, k, 0, 0)), _resident(tabs['fwd2'].shape)],
        out_specs=pl.BlockSpec((2, None, n2, d), lambda k: (0, k, 0, 0)),
        out_shape=jax.ShapeDtypeStruct((2, n1e, n2, d), BF16),
        compiler_params=_params("parallel"), name="hy_filter_spectrum")(a, tabs['fwd2'])


def _spectral_body(a_ref, k_ref, f_ref, i_ref, o_ref):
    bb, _, n2, d = o_ref.shape
    a = jnp.concatenate([a_ref[b].reshape(2 * n2, d) for b in range(bb)], axis=1)
    x = _dot(f_ref[...], a)
    xr, xi = x[:n2], x[n2:]
    kr = jnp.concatenate([k_ref[0]] * bb, axis=1)
    ki = jnp.concatenate([k_ref[1]] * bb, axis=1)
    y = jnp.concatenate([xr * kr - xi * ki, xr * ki + xi * kr], axis=0).astype(BF16)
    z = _dot(i_ref[...], y)
    for b in range(bb):
        o_ref[b] = z[:, b * d:(b + 1) * d].reshape(2, n2, d).astype(o_ref.dtype)


def _spectral_multiply(a, kc, length, tabs):
    bx = a.shape[0]
    _, n1e, n2, d = kc.shape
    bb = max(b for b in range(1, bx + 1) if bx % b == 0 and b * n2 <= 2 * MXU_DIM)
    blk = pl.BlockSpec((bb, 2, None, n2, d), lambda k, b: (b, 0, k, 0, 0))
    return pl.pallas_call(
        _spectral_body, grid=(n1e, bx // bb),
        in_specs=[blk, pl.BlockSpec((2, None, n2, d), lambda k, b: (0, k, 0, 0)),
                  _resident(tabs['fwd2'].shape), _resident(tabs['inv2'].shape)],
        out_specs=blk, out_shape=jax.ShapeDtypeStruct((bx, 2, n1e, n2, d), BF16),
        compiler_params=_params("parallel", "arbitrary"), name="hy_spectral")(
            a, kc, tabs['fwd2'], tabs['inv2'])


def _idft1_body(z_ref, t_ref, o_ref):
    _, n1e, _, d = z_ref.shape
    rows = 2 * n1e * SUBLANES
    z = z_ref[...].astype(F32)
    halves = []
    for g in range(DFT_ROWS // SUBLANES):
        zg = z[:, :, g * SUBLANES:(g + 1) * SUBLANES, :].reshape(rows, d).astype(BF16)
        zg = jnp.concatenate([zg, jnp.zeros((t_ref.shape[2] - rows, d), BF16)], axis=0)
        halves.append(_dot(t_ref[g], zg).reshape(o_ref.shape[0], SUBLANES, d))
    o_ref[...] = jnp.concatenate(halves, axis=1).astype(o_ref.dtype)


def _idft1(z, length, tabs):
    bx, _, n1e, n2, d = z.shape
    n, n1, _, _ = _dft_plan(length)
    tab = tabs['inv1']
    y = pl.pallas_call(
        _idft1_body, grid=(n2 // DFT_ROWS, bx),
        in_specs=[pl.BlockSpec((None, 2, n1e, None, DFT_ROWS, d), lambda j, b: (b, 0, 0, j, 0, 0)),
                  pl.BlockSpec((DFT_ROWS // SUBLANES,) + tab.shape[1:], lambda j, b: (j, 0, 0))],
        out_specs=pl.BlockSpec((None, n1 // 2, None, DFT_ROWS, d), lambda j, b: (b, 0, j, 0, 0)),
        out_shape=jax.ShapeDtypeStruct((bx, n1 // 2, n2 // DFT_ROWS, DFT_ROWS, d), BF16),
        compiler_params=_params("parallel", "arbitrary"), name="hy_idft1")(
            z.reshape(bx, 2, n1e, n2 // DFT_ROWS, DFT_ROWS, d), tab)
    return y.reshape(bx, length, d)


def _hy_out_body(y_ref, vv_ref, x0_ref, h_ref, sk_ref, w_ref, b_ref, o_ref):
    y = (y_ref[...] + vv_ref[...] * sk_ref[...]) * x0_ref[...]
    o_ref[...] = h_ref[...] + _dot(y.astype(BF16), w_ref[...]) + b_ref[...]


def _hy_out(yc, vv, x0, h, skip, w, b, tm=TOKEN_TILE):
    t, d = h.shape
    tile = pl.BlockSpec((tm, d), lambda i: (i, 0))
    return pl.pallas_call(
        _hy_out_body, grid=(t // tm,),
        in_specs=[tile, tile, tile, tile, _resident((1, d)), _resident((d, d)), _resident((1, d))],
        out_specs=tile, out_shape=jax.ShapeDtypeStruct((t, d), F32),
        compiler_params=_params("parallel"), name="hy_out")(yc, vv, x0, h, _row(skip), w, _row(b))


def _hyena_mixer(h, g, prm, seq_len):
    t, d = h.shape
    bsz = t // seq_len
    tabs = _dft_tables(seq_len)
    x0, vv = _hy_in(h, g, prm['in_w'], prm['in_b'], prm['conv_w'], prm['conv_b'], seq_len)
    kc = _filter_spectrum(_hy_filters(seq_len, prm, d), seq_len, tabs)
    a = _dft1(vv.reshape(bsz, seq_len, d), seq_len, tabs)
    z = _spectral_multiply(a, kc, seq_len, tabs)
    yc = _idft1(z, seq_len, tabs).reshape(t, d)
    return _hy_out(yc, vv, x0, h, prm['skip'], prm['out_w'], prm['out_b'])


def _trunk(x, p, w, depth):
    bsz, seq_len, d = x.shape
    h = x.reshape(bsz * seq_len, d)
    p = p.reshape(depth, bsz * seq_len, -1)
    for i in range(depth):
        kind, j = i % N_MIXERS, i // N_MIXERS
        h = _ffn(h, w['ln_ffn1'][i], w['ffn1_wg'][i], w['ffn1_wu'][i], w['ffn1_wd'][i])
        mix = (_hyena_mixer, _na_mixer, _lru_mixer)[kind]
        h = mix(h, w['ln_mix'][i], w[('hy', 'na', 'lru')[kind]][j], seq_len)
        h = _ffn(h, w['ln_ffn2'][i], w['ffn2_wg'][i], w['ffn2_wu'][i], w['ffn2_wd'][i],
                 ple=(w['ln_ple'][i], w['ple_gate'][i], p, i, w['ple_proj'][i]),
                 final=w['ln_final'] if i == depth - 1 else None)
    return h.reshape(bsz, seq_len, d)


def kernel(x_prompt, x_sample, p_prompt, p_sample, ln_ffn1, ffn1_wg, ffn1_wu, ffn1_wd, ln_mix, ln_ffn2, ffn2_wg, ffn2_wu, ffn2_wd, ln_ple, ple_gate, ple_proj, ln_final, hy_in_w, hy_in_b, hy_conv_w, hy_conv_b, hy_f_w1, hy_f_b1, hy_f_w2, hy_f_b2, hy_f_w3, hy_f_b3, hy_f_freq, hy_f_out, hy_skip, hy_out_w, hy_out_b, na_qkv_w, na_qkv_b, na_rpb, na_out_w, na_out_b, lru_in_w, lru_in_b, lru_conv_w, lru_conv_b, lru_wa, lru_ba, lru_wx, lru_bx, lru_lambda, lru_out_w, lru_out_b):
    depth = ln_ffn1.shape[0]
    d = x_prompt.shape[-1]
    b16 = lambda a: a.astype(BF16)
    w = {
        'ln_ffn1': ln_ffn1, 'ffn1_wg': b16(ffn1_wg), 'ffn1_wu': b16(ffn1_wu), 'ffn1_wd': b16(ffn1_wd),
        'ln_mix': ln_mix, 'ln_ffn2': ln_ffn2, 'ffn2_wg': b16(ffn2_wg), 'ffn2_wu': b16(ffn2_wu),
        'ffn2_wd': b16(ffn2_wd), 'ln_ple': ln_ple, 'ple_gate': b16(ple_gate), 'ple_proj': b16(ple_proj),
        'ln_final': ln_final,
        'hy': [dict(in_w=b16(hy_in_w[j]), in_b=hy_in_b[j], conv_w=hy_conv_w[j], conv_b=hy_conv_b[j],
                    f_w1=hy_f_w1[j], f_b1=hy_f_b1[j], f_w2=hy_f_w2[j], f_b2=hy_f_b2[j], f_w3=hy_f_w3[j],
                    f_b3=hy_f_b3[j], f_freq=hy_f_freq[j], f_out=hy_f_out[j], skip=hy_skip[j],
                    out_w=b16(hy_out_w[j]), out_b=hy_out_b[j]) for j in range(hy_in_w.shape[0])],
        'na': [dict(wq=b16(na_qkv_w[j][:, :d]), wkt=b16(na_qkv_w[j][:, d:2 * d].T), wv=b16(na_qkv_w[j][:, 2 * d:]),
                    bq=na_qkv_b[j][:d], bk=na_qkv_b[j][d:2 * d], bv=na_qkv_b[j][2 * d:],
                    bias=_na_bias_table(na_rpb[j]), out_w=b16(na_out_w[j]), out_b=na_out_b[j])
               for j in range(na_qkv_w.shape[0])],
        'lru': [dict(in_w=b16(lru_in_w[j]), in_b=lru_in_b[j], conv_w=lru_conv_w[j], conv_b=lru_conv_b[j],
                     wa=b16(lru_wa[j]), ba=lru_ba[j], wx=b16(lru_wx[j]), bx=lru_bx[j],
                     c8=-LRU_C * jax.nn.softplus(-lru_lambda[j].astype(F32)),
                     out_w=b16(lru_out_w[j]), out_b=lru_out_b[j]) for j in range(lru_in_w.shape[0])],
    }
    return _trunk(x_prompt, p_prompt, w, depth), _trunk(x_sample, p_sample, w, depth)
```

```python
import functools
import math

import numpy as np
import jax
import jax.numpy as jnp
from jax import lax
from jax.experimental import pallas as pl
from jax.experimental.pallas import tpu as pltpu

F32 = jnp.float32
BF16 = jnp.bfloat16

RMS_EPS = 1e-6
NEG_INF = -1e30
LANES = 128
SUBLANES = 8
MXU_DIM = 256
VMEM_LIMIT = 56 * 1024 * 1024

GRID_W = 64
NA_HEADS = 32
NA_KH = 8
NA_KW = 16
LRU_BLOCKS = 4
LRU_CONV = 4
LRU_C = 8.0
HY_EMB = 33
HY_SHORT = 3
HY_PACK = LANES // 2
HY_DECAY_TARGET = 1e-2
HY_DECAY_PCT_HI = 0.3
HY_DECAY_PCT_LO = 1.5
N_MIXERS = 3

TOKEN_TILE = 512
FFN_TILE = 1024
HALO = SUBLANES


def _params(*sem):
    return pltpu.CompilerParams(dimension_semantics=sem, vmem_limit_bytes=VMEM_LIMIT)


def _resident(shape):
    nd = len(shape)
    return pl.BlockSpec(shape, lambda *_: (0,) * nd, pipeline_mode=pl.Buffered(1))


def _rms(x, g):
    ms = jnp.mean(x * x, axis=-1, keepdims=True)
    return x * lax.rsqrt(ms + RMS_EPS) * g


def _dot(a, b):
    return jnp.dot(a, b, preferred_element_type=F32)


def _sigmoid(x):
    return 0.5 * jnp.tanh(0.5 * x) + 0.5


def _row(v):
    return v.reshape(1, -1).astype(F32)


def _ffn_body(*refs, pre, ple, final, n_chunks):
    h_ref, g_ref, wg_ref, wu_ref, wd_ref = refs[:5]
    o_ref = refs[-1]
    h = h_ref[...]
    k = 5
    if pre == 'gated':
        y_ref, v_ref, x0_ref, sk_ref, w_ref, b_ref = refs[k:k + 6]
        k += 6
        y = (y_ref[...] + v_ref[...] * sk_ref[...]) * x0_ref[...]
        h = h + _dot(y.astype(BF16), w_ref[...]) + b_ref[...]
    elif pre == 'plain':
        x_ref, w_ref, b_ref = refs[k:k + 3]
        k += 3
        h = h + _dot(x_ref[...], w_ref[...]) + b_ref[...]
    xn = _rms(h, g_ref[...]).astype(BF16)
    n_tiles = pl.cdiv(wg_ref.shape[1], MXU_DIM)
    bounds = [min(-(-n_tiles * c // n_chunks) * MXU_DIM, wg_ref.shape[1]) for c in range(n_chunks + 1)]
    acc = jnp.zeros_like(h)
    for c0, c1 in zip(bounds[:-1], bounds[1:]):
        gt = _dot(xn, wg_ref[:, c0:c1])
        up = _dot(xn, wu_ref[:, c0:c1])
        act = (gt * jax.nn.sigmoid(gt) * up).astype(BF16)
        acc = acc + _dot(act, wd_ref[c0:c1, :])
    h = h + 0.5 * acc
    if ple:
        gp_ref, wpg_ref, p_ref, wpp_ref = refs[k:k + 4]
        k += 4
        gate = jax.nn.sigmoid(_dot(_rms(h, gp_ref[...]).astype(BF16), wpg_ref[...]))
        h = h + gate * _dot(p_ref[...].astype(BF16), wpp_ref[...])
    if final:
        h = _rms(h, refs[k][...])
    o_ref[...] = h


def _ffn(h, g, wg, wu, wd, pre=None, ple=None, final=None):
    t, d = h.shape
    d_ff = wg.shape[1]
    tm = FFN_TILE if pre is None else TOKEN_TILE
    tile = lambda width: pl.BlockSpec((tm, width), lambda i: (i, 0))
    args = [h, _row(g), wg, wu, wd]
    specs = [tile(d), _resident((1, d)), _resident((d, d_ff)), _resident((d, d_ff)), _resident((d_ff, d))]
    kind = None
    if pre is not None:
        kind, per_token, shared = pre
        args += list(per_token) + [op if op.ndim == 2 else _row(op) for op in shared]
        specs += [tile(op.shape[1]) for op in per_token] + [_resident(op.shape) for op in args[-len(shared):]]
    if ple is not None:
        gp, wpg, p, layer, wpp = ple
        args += [_row(gp), wpg, p, wpp]
        specs += [_resident((1, d)), _resident((d, d)),
                  pl.BlockSpec((None, tm, p.shape[2]), lambda i: (layer, i, 0)), _resident(wpp.shape)]
    if final is not None:
        args.append(_row(final))
        specs.append(_resident((1, d)))
    return pl.pallas_call(
        functools.partial(_ffn_body, pre=kind, ple=ple is not None, final=final is not None, n_chunks=2),
        grid=(t // tm,), in_specs=specs, out_specs=tile(d),
        out_shape=jax.ShapeDtypeStruct((t, d), F32),
        compiler_params=_params("parallel"), name="ffn")(*args)


def _halo_specs(tm, d, n_rows):
    per = tm // HALO
    last = n_rows // HALO - 1
    return [pl.BlockSpec((HALO, d), lambda i: (jnp.maximum(i * per - 1, 0), 0)),
            pl.BlockSpec((tm, d), lambda i: (i, 0)),
            pl.BlockSpec((HALO, d), lambda i: (jnp.minimum((i + 1) * per, last), 0))]


def _normed_with_halo(hp_ref, hc_ref, hn_ref, g_ref):
    x = jnp.concatenate([hp_ref[...], hc_ref[...], hn_ref[...]], axis=0)
    return _rms(x, g_ref[...]).astype(BF16)


def _store_zero_padded(u_s, u, tiles_per_seq, tm):
    i = pl.program_id(0) % tiles_per_seq
    u_s[HALO:tm + HALO, :] = u[HALO:tm + HALO]
    u_s[0:HALO, :] = jnp.where(i == 0, 0.0, u[0:HALO])
    u_s[tm + HALO:tm + 2 * HALO, :] = jnp.where(i == tiles_per_seq - 1, 0.0, u[tm + HALO:tm + 2 * HALO])


def _lru_in_body(hp_ref, hc_ref, hn_ref, g_ref, w_ref, b_ref, cw_ref, cb_ref, gb_ref, xb_ref, u_s,
                 *, tiles_per_seq, tm):
    r_w = gb_ref.shape[1]
    xn = _normed_with_halo(hp_ref, hc_ref, hn_ref, g_ref)
    gb_ref[...] = jax.nn.gelu(_dot(xn[HALO:HALO + tm], w_ref[:, :r_w]) + b_ref[:, :r_w]).astype(BF16)
    u = _dot(xn, w_ref[:, r_w:]) + b_ref[:, r_w:]
    _store_zero_padded(u_s, u, tiles_per_seq, tm)
    left = LRU_CONV // 2
    y = cb_ref[...]
    for j in range(LRU_CONV):
        y = y + cw_ref[j:j + 1, :] * u_s[pl.ds(HALO + j - left, tm), :]
    xb_ref[...] = y


def _lru_in(h, g, w, b, cw, cb, seq_len, tm=TOKEN_TILE):
    t, d = h.shape
    r_w = cw.shape[1]
    tile = pl.BlockSpec((tm, r_w), lambda i: (i, 0))
    return pl.pallas_call(
        functools.partial(_lru_in_body, tiles_per_seq=seq_len // tm, tm=tm),
        grid=(t // tm,),
        in_specs=_halo_specs(tm, d, t) + [_resident((1, d)), _resident(w.shape), _resident((1, 2 * r_w)),
                                          _resident(cw.shape), _resident((1, r_w))],
        out_specs=[tile, tile],
        out_shape=[jax.ShapeDtypeStruct((t, r_w), BF16), jax.ShapeDtypeStruct((t, r_w), F32)],
        scratch_shapes=[pltpu.VMEM((tm + 2 * HALO, r_w), F32)],
        compiler_params=_params("parallel"), name="lru_in")(h, h, h, _row(g), w, _row(b), cw, _row(cb))


def _block_diag(x, w_ref):
    nb, bw, _ = w_ref.shape
    return jnp.concatenate([_dot(x[:, n * bw:(n + 1) * bw], w_ref[n]) for n in range(nb)], axis=1)


def _lru_scan_tile(a, b, a_s, b_s, h_s, carry_ref, reverse):
    tm, width = a.shape
    n_groups = tm // SUBLANES
    a = a.reshape(n_groups, SUBLANES, width)
    b = b.reshape(n_groups, SUBLANES, width)
    row = lax.broadcasted_iota(jnp.int32, (1, SUBLANES, 1), 1)
    for dist in (1, 2, 4):
        if reverse:
            shift, ok = SUBLANES - dist, row < SUBLANES - dist
        else:
            shift, ok = dist, row >= dist
        a_prev = jnp.where(ok, pltpu.roll(a, shift, 1), 1.0)
        b_prev = jnp.where(ok, pltpu.roll(b, shift, 1), 0.0)
        b = a * b_prev + b
        a = a * a_prev
    a_s[...] = a.reshape(tm, width)
    b_s[...] = b.reshape(tm, width)
    edge = 0 if reverse else SUBLANES - 1

    def group(k, carry):
        gi = n_groups - 1 - k if reverse else k
        r0 = pl.multiple_of(gi * SUBLANES, SUBLANES)
        hg = a_s[pl.ds(r0, SUBLANES), :] * carry + b_s[pl.ds(r0, SUBLANES), :]
        h_s[pl.ds(r0, SUBLANES), :] = hg
        return hg[edge:edge + 1, :]

    carry_ref[...] = lax.fori_loop(0, n_groups, group, carry_ref[...], unroll=True)


def _lru_gates(xb, wa_ref, ba_ref, wx_ref, bx_ref, c8_ref, first_row):
    xb16 = xb.astype(BF16)
    gate_x = _sigmoid(_block_diag(xb16, wx_ref) + bx_ref[...])
    gate_a = _sigmoid(_block_diag(xb16, wa_ref) + ba_ref[...])
    log_a = gate_a * c8_ref[...]
    a = jnp.exp(log_a)
    y = 1.0 - a * a
    mult = jnp.where(y > 0.0, y * lax.rsqrt(y), 0.0)
    r = lax.broadcasted_iota(jnp.int32, (xb.shape[0], 1), 0)
    mult = jnp.where(r == first_row, 1.0, mult)
    return a, mult * gate_x * xb


def _lru_fwd_body(xb_ref, wa_ref, ba_ref, wx_ref, bx_ref, c8_ref, h0_ref, a_s, b_s, carry, *, tm):
    ti = pl.program_id(1)

    @pl.when(ti == 0)
    def _():
        carry[...] = jnp.zeros_like(carry)

    first_row = jnp.where(ti == 0, 0, -1)
    a, b = _lru_gates(xb_ref[...], wa_ref, ba_ref, wx_ref, bx_ref, c8_ref, first_row)
    _lru_scan_tile(a, b, a_s, b_s, h0_ref, carry, reverse=False)


def _lru_bwd_body(xb_ref, wa_ref, ba_ref, wx_ref, bx_ref, c8_ref, h0_ref, gb_ref, h_ref, wo_ref, bo_ref,
                  o_ref, a_s, b_s, h1_s, carry, *, tm):
    ti = pl.program_id(1)

    @pl.when(ti == 0)
    def _():
        carry[...] = jnp.zeros_like(carry)

    first_row = jnp.where(ti == 0, tm - 1, -1)
    a, b = _lru_gates(xb_ref[...], wa_ref, ba_ref, wx_ref, bx_ref, c8_ref, first_row)
    _lru_scan_tile(a, b, a_s, b_s, h1_s, carry, reverse=True)
    y = h0_ref[...] + h1_s[...]
    z = (gb_ref[...].astype(F32) * y).astype(BF16)
    o_ref[...] = h_ref[...] + _dot(z, wo_ref[...]) + bo_ref[...]


def _lru_mixer(h, g, prm, seq_len, tm=TOKEN_TILE):
    t, d = h.shape
    bsz = t // seq_len
    nt = seq_len // tm
    gb, xb = _lru_in(h, g, prm['in_w'], prm['in_b'], prm['conv_w'], prm['conv_b'], seq_len, tm)
    r_w = xb.shape[1]
    bw = r_w // LRU_BLOCKS
    gate_specs = lambda dr: [_resident((LRU_BLOCKS, bw, bw)), _resident((1, r_w)),
                             _resident((LRU_BLOCKS, bw, bw)), _resident((1, r_w)), _resident((1, r_w))]
    gate_args = lambda dr: [prm['wa'][dr], _row(prm['ba'][dr]), prm['wx'][dr], _row(prm['bx'][dr]),
                            _row(prm['c8'][dr])]
    fwd = lambda b, i: (b * nt + i, 0)
    bwd = lambda b, i: (b * nt + nt - 1 - i, 0)
    scan_scratch = [pltpu.VMEM((tm, r_w), F32), pltpu.VMEM((tm, r_w), F32)]
    h0 = pl.pallas_call(
        functools.partial(_lru_fwd_body, tm=tm), grid=(bsz, nt),
        in_specs=[pl.BlockSpec((tm, r_w), fwd)] + gate_specs(0),
        out_specs=pl.BlockSpec((tm, r_w), fwd), out_shape=jax.ShapeDtypeStruct((t, r_w), F32),
        scratch_shapes=scan_scratch + [pltpu.VMEM((1, r_w), F32)],
        compiler_params=_params("arbitrary", "arbitrary"), name="lru_fwd")(xb, *gate_args(0))
    out = pl.pallas_call(
        functools.partial(_lru_bwd_body, tm=tm), grid=(bsz, nt),
        in_specs=[pl.BlockSpec((tm, r_w), bwd)] + gate_specs(1)
        + [pl.BlockSpec((tm, r_w), bwd), pl.BlockSpec((tm, r_w), bwd), pl.BlockSpec((tm, d), bwd),
           _resident((r_w, d)), _resident((1, d))],
        out_specs=pl.BlockSpec((tm, d), bwd), out_shape=jax.ShapeDtypeStruct((t, d), F32),
        scratch_shapes=scan_scratch + [pltpu.VMEM((tm, r_w), F32), pltpu.VMEM((1, r_w), F32)],
        compiler_params=_params("arbitrary", "arbitrary"), name="lru_bwd")(
            xb, *gate_args(1), h0, gb, h, prm['out_w'], _row(prm['out_b']))
    return out, None


NA_PAIR = 2 * GRID_W
NA_WIN = 5
NA_PB = 4
NA_HG = LANES // (1024 // NA_HEADS)


def _na_qkv_body(h_ref, g_ref, wq_ref, wkt_ref, wv_ref, bq_ref, bk_ref, bv_ref, q_ref, kt_ref, v_ref, *, scale):
    xn = _rms(h_ref[...], g_ref[...]).astype(BF16)
    q_ref[...] = ((_dot(xn, wq_ref[...]) + bq_ref[...]) * scale).astype(BF16)
    kt = lax.dot_general(wkt_ref[...], xn, (((1,), (1,)), ((), ())), preferred_element_type=F32)
    kt_ref[...] = (kt + bk_ref[...]).astype(BF16)
    v_ref[...] = (_dot(xn, wv_ref[...]) + bv_ref[...]).astype(BF16)


def _na_qkv(h, g, wq, wkt, wv, bq, bk, bv, seq_len, scale, tm=TOKEN_TILE):
    t, d = h.shape
    bsz, nt = t // seq_len, seq_len // tm
    tile = pl.BlockSpec((tm, d), lambda i: (i, 0))
    return pl.pallas_call(
        functools.partial(_na_qkv_body, scale=scale), grid=(t // tm,),
        in_specs=[tile, _resident((1, d)), _resident((d, d)), _resident((d, d)), _resident((d, d)),
                  _resident((1, d)), _resident((d, 1)), _resident((1, d))],
        out_specs=[tile, pl.BlockSpec((None, d, tm), lambda i: (i // nt, 0, i % nt)), tile],
        out_shape=[jax.ShapeDtypeStruct((t, d), BF16), jax.ShapeDtypeStruct((bsz, d, seq_len), BF16),
                   jax.ShapeDtypeStruct((t, d), BF16)],
        compiler_params=_params("parallel"), name="na_qkv")(
            h, _row(g), wq, wkt, wv, _row(bq), bk.reshape(d, 1).astype(F32), _row(bv))


def _na_attn_body(q_ref, kp_ref, kc_ref, kn_ref, vp_ref, vc_ref, vn_ref, sb_ref, o_ref, kwin, vwin,
                  *, n_pairs, rows):
    pb = pl.program_id(2)
    blk = NA_PB * NA_PAIR
    for bi, (k_ref, v_ref) in enumerate(((kp_ref, vp_ref), (kc_ref, vc_ref), (kn_ref, vn_ref))):
        for t in range(NA_PB):
            kwin[bi * NA_PB + t] = k_ref[:, t * NA_PAIR:(t + 1) * NA_PAIR]
        vwin[bi * blk:(bi + 1) * blk, :] = v_ref[...]
    dh = LANES // NA_HG
    pad = 2 * SUBLANES
    lane = lax.broadcasted_iota(jnp.int32, (1, LANES), 1)
    in_head = [(lane // dh) == j for j in range(NA_HG)]
    q_row_onehot = (lax.broadcasted_iota(jnp.int32, (NA_PAIR, pad), 0) // GRID_W
                    == lax.broadcasted_iota(jnp.int32, (NA_PAIR, pad), 1)).astype(BF16)
    mask_row = lax.broadcasted_iota(jnp.int32, (pad, 1), 0)
    k_row_in_win = lax.broadcasted_iota(jnp.int32, (1, NA_WIN * NA_PAIR), 1) // GRID_W

    def pair(i, carry):
        p = pb * NA_PB + i
        kp0 = jnp.clip(p - 2, 0, n_pairs - NA_WIN)
        off = kp0 - (pb - 1) * NA_PB
        m0 = kp0 - p + (NA_WIN - 1)
        qi = q_ref[pl.ds(pl.multiple_of(i * NA_PAIR, NA_PAIR), NA_PAIR), :]
        ksl = jnp.concatenate([kwin[off + jp] for jp in range(NA_WIN)], axis=1)
        vsl = vwin[pl.ds(pl.multiple_of(off * NA_PAIR, NA_PAIR), NA_WIN * NA_PAIR), :]
        r_start = jnp.clip(2 * p + mask_row - NA_KH // 2, 0, rows - NA_KH)
        k_row = 2 * kp0 + k_row_in_win
        in_window = (k_row >= r_start) & (k_row < r_start + NA_KH)
        row_mask = jnp.where(in_window | (mask_row >= 2), 0.0, NEG_INF).astype(BF16)
        q_heads = jnp.concatenate(
            [jnp.concatenate([jnp.where(m, qi, jnp.zeros_like(qi)), q_row_onehot], axis=1) for m in in_head], axis=0)
        bias = jnp.concatenate(
            [jnp.concatenate([sb_ref[j, m0 + jp] for jp in range(NA_WIN)], axis=1) for j in range(NA_HG)], axis=0)
        s = _dot(q_heads, jnp.concatenate([ksl, row_mask], axis=0)) + bias
        e = jnp.exp(s - jnp.max(s, axis=-1, keepdims=True))
        denom = jnp.sum(e, axis=-1, keepdims=True)
        pv = _dot(e.astype(BF16), vsl) / denom
        acc = jnp.zeros((NA_PAIR, LANES), F32)
        for j, m in enumerate(in_head):
            acc = acc + jnp.where(m, pv[j * NA_PAIR:(j + 1) * NA_PAIR], 0.0)
        o_ref[pl.ds(pl.multiple_of(i * NA_PAIR, NA_PAIR), NA_PAIR), :] = acc.astype(BF16)
        return carry

    lax.fori_loop(0, NA_PB, pair, 0, unroll=True)


def _na_bias_table(rpb):
    n_h, n_dr, n_dc = rpb.shape
    col = np.arange(GRID_W)
    q_start = np.clip(col - NA_KW // 2, 0, GRID_W - NA_KW)
    col_ok = (col[None, :] >= q_start[:, None]) & (col[None, :] < q_start[:, None] + NA_KW)
    dc_idx = col[None, :] - col[:, None] + NA_KW - 1
    onehot = ((np.arange(n_dc)[:, None, None] == dc_idx[None]) & col_ok[None]).reshape(n_dc, -1)
    tab = jnp.dot(rpb.astype(F32).reshape(n_h * n_dr, n_dc), jnp.asarray(onehot, F32),
                  precision=lax.Precision.HIGHEST)
    tab = tab.reshape(n_h, n_dr, GRID_W, GRID_W) + jnp.where(col_ok, 0.0, NEG_INF).astype(F32)
    masked = jnp.full((n_h, GRID_W, GRID_W), NEG_INF, F32)

    def quadrant(dr):
        return tab[:, dr + NA_KH - 1] if abs(dr) <= NA_KH - 1 else masked

    tiles = [jnp.concatenate([jnp.concatenate([quadrant(2 * (m - (NA_WIN - 1)) + c - a) for c in range(2)], axis=2)
                              for a in range(2)], axis=1) for m in range(2 * (NA_WIN - 1) + 1)]
    sb = jnp.stack(tiles, axis=1)
    return sb.reshape(n_h // NA_HG, NA_HG, *sb.shape[1:])


def _na_mixer(h, g, prm, seq_len):
    t, d = h.shape
    bsz = t // seq_len
    rows = seq_len // GRID_W
    n_pairs = rows // 2
    nblk = n_pairs // NA_PB
    blk = NA_PB * NA_PAIR
    q, kt, v = _na_qkv(h, g, prm['wq'], prm['wkt'], prm['wv'], prm['bq'], prm['bk'], prm['bv'], seq_len,
                       (d // NA_HEADS) ** -0.5)
    q = q.reshape(bsz, seq_len, d)
    v = v.reshape(bsz, seq_len, d)
    prev = lambda pb: jnp.maximum(pb - 1, 0)
    nxt = lambda pb: jnp.minimum(pb + 1, nblk - 1)
    k_spec = lambda f: pl.BlockSpec((None, LANES, blk), lambda gi, b, pb: (b, gi, f(pb)))
    v_spec = lambda f: pl.BlockSpec((None, blk, LANES), lambda gi, b, pb: (b, f(pb), gi))
    same = lambda pb: pb
    sb = prm['bias']
    o = pl.pallas_call(
        functools.partial(_na_attn_body, n_pairs=n_pairs, rows=rows),
        grid=(d // LANES, bsz, nblk),
        in_specs=[v_spec(same), k_spec(prev), k_spec(same), k_spec(nxt), v_spec(prev), v_spec(same), v_spec(nxt),
                  pl.BlockSpec((None,) + sb.shape[1:], lambda gi, b, pb: (gi, 0, 0, 0, 0))],
        out_specs=v_spec(same), out_shape=jax.ShapeDtypeStruct((bsz, seq_len, d), BF16),
        scratch_shapes=[pltpu.VMEM((3 * NA_PB, LANES, NA_PAIR), BF16), pltpu.VMEM((3 * blk, LANES), BF16)],
        compiler_params=_params("arbitrary", "arbitrary", "arbitrary"), name="na_attn")(
            q, kt, kt, kt, v, v, v, sb)
    return h, ('plain', [o.reshape(t, d)], [prm['out_w'], prm['out_b']])


def _hy_in_body(hp_ref, hc_ref, hn_ref, g_ref, w_ref, b_ref, cw_ref, cb_ref, x0_ref, vv_ref, u_s,
                *, tiles_per_seq, tm):
    d = x0_ref.shape[1]
    xn = _normed_with_halo(hp_ref, hc_ref, hn_ref, g_ref)
    left = (HY_SHORT - 1) // 2

    def conv(c0):
        u = _dot(xn, w_ref[:, c0:c0 + d]) + b_ref[:, c0:c0 + d]
        _store_zero_padded(u_s, u, tiles_per_seq, tm)
        y = cb_ref[:, c0:c0 + d]
        for j in range(HY_SHORT):
            y = y + cw_ref[j:j + 1, c0:c0 + d] * u_s[pl.ds(HALO + j - left, tm), :]
        return y

    x0_ref[...] = conv(0).astype(x0_ref.dtype)
    x1 = conv(d)
    vv_ref[...] = conv(2 * d) * x1


def _hy_in(h, g, w, b, cw, cb, seq_len, tm=TOKEN_TILE):
    t, d = h.shape
    tile = pl.BlockSpec((tm, d), lambda i: (i, 0))
    return pl.pallas_call(
        functools.partial(_hy_in_body, tiles_per_seq=seq_len // tm, tm=tm), grid=(t // tm,),
        in_specs=_halo_specs(tm, d, t) + [_resident((1, d)), _resident(w.shape), _resident((1, 3 * d)),
                                          _resident(cw.shape), _resident((1, 3 * d))],
        out_specs=[tile, tile],
        out_shape=[jax.ShapeDtypeStruct((t, d), BF16), jax.ShapeDtypeStruct((t, d), F32)],
        scratch_shapes=[pltpu.VMEM((tm + 2 * HALO, d), F32)],
        compiler_params=_params("parallel"), name="hy_in")(h, h, h, _row(g), w, _row(b), cw, _row(cb))


def _hy_filter_body(z_ref, w1_ref, b1_ref, w2_ref, b2_ref, w3_ref, b3_ref, fr_ref, wo_ref, dl_ref,
                    k_ref):
    hp = lax.Precision.HIGHEST
    half, d = z_ref.shape[0], k_ref.shape[2]
    z = z_ref[...]
    fr = fr_ref[...]
    a = jnp.sin(fr * (jnp.dot(z, w1_ref[...], precision=hp, preferred_element_type=F32) + b1_ref[...]))
    a = jnp.sin(fr * (jnp.dot(a, w2_ref[...], precision=hp, preferred_element_type=F32) + b2_ref[...]))
    a = jnp.sin(fr * (jnp.dot(a, w3_ref[...], precision=hp, preferred_element_type=F32) + b3_ref[...]))
    a = a.astype(BF16)
    r = lax.broadcasted_iota(jnp.int32, (half, 1), 0)
    for part in range(2):
        k = _dot(a, wo_ref[part])
        window = jnp.exp(-z[:, part * HY_PACK:part * HY_PACK + 1] * dl_ref[...])
        rows = slice(part * half, (part + 1) * half)
        k_ref[0, rows, :] = k[:, :d] * window
        kb = k[:, d:] * window
        if part == 0:
            kb = jnp.where((r == 0) & (pl.program_id(0) == 0), 0.0, kb)
        k_ref[1, rows, :] = kb


def _block_diag2(x):
    zero = jnp.zeros_like(x)
    return jnp.concatenate([jnp.concatenate([x, zero], axis=-1), jnp.concatenate([zero, x], axis=-1)], axis=-2)


def _pad_to(x, shape):
    return jnp.pad(x.astype(F32), [(0, s - n) for s, n in zip(shape, x.shape)])


@functools.lru_cache(maxsize=None)
def _hy_positions(length, d, tl):
    bands = (HY_EMB - 1) // 2
    t = np.linspace(0.0, 1.0, length)[:, None]
    ang = (2.0 * math.pi / length) * np.arange(length)[:, None]
    fb = np.linspace(1e-4, bands - 1, bands)[None, :]
    z = np.concatenate([t, np.cos(fb * ang), -np.sin(fb * ang)], axis=-1)
    z = np.pad(z, ((0, 0), (0, HY_PACK - z.shape[1]))).reshape(length // tl, 2, tl // 2, HY_PACK)
    z = z.transpose(0, 2, 1, 3).reshape(length // 2, 2 * HY_PACK).astype(np.float32)
    decay_min = math.log(HY_DECAY_TARGET) / HY_DECAY_PCT_LO
    decay_max = math.log(HY_DECAY_TARGET) / HY_DECAY_PCT_HI
    deltas = np.abs(np.linspace(decay_min, decay_max, d))[None, :].astype(np.float32)
    return z, deltas


def _hy_filters(length, prm, d, tl=TOKEN_TILE):
    z, deltas = _hy_positions(length, d, tl)
    sq = (HY_PACK, HY_PACK)
    twice = lambda v: jnp.tile(_pad_to(v[None], (1, HY_PACK)), (1, 2))
    w_out = _pad_to(prm['f_out'], (HY_PACK, 2 * d))
    w_out = jnp.stack([jnp.concatenate([w_out, jnp.zeros_like(w_out)]),
                       jnp.concatenate([jnp.zeros_like(w_out), w_out])]).astype(BF16)
    args = [z, _block_diag2(_pad_to(prm['f_w1'], sq)), twice(prm['f_b1']),
            _block_diag2(_pad_to(prm['f_w2'], sq)), twice(prm['f_b2']),
            _block_diag2(_pad_to(prm['f_w3'], sq)), twice(prm['f_b3']), twice(prm['f_freq']), w_out, deltas]
    return pl.pallas_call(
        _hy_filter_body, grid=(length // tl,),
        in_specs=[pl.BlockSpec((tl // 2, 2 * HY_PACK), lambda i: (i, 0))] + [_resident(a.shape) for a in args[1:]],
        out_specs=pl.BlockSpec((2, tl, d), lambda i: (0, i, 0)),
        out_shape=jax.ShapeDtypeStruct((2, length, d), F32),
        compiler_params=_params("parallel"), name="hy_filter")(*args)


def _dft_plan(length):
    n = 2 * length
    n1 = min(128, n // 16)
    return n, n1, n // n1, n1 // 2 + 1


@functools.lru_cache(maxsize=None)
def _dft_tables(length):
    n, n1, n2, n1e = _dft_plan(length)
    n1h, sub = n1 // 2, SUBLANES
    groups = n2 // sub
    k1 = np.arange(n1e)
    t_idx = (n2 * np.arange(n1h)[None, None, None, :] + sub * np.arange(groups)[:, None, None, None]
             + np.arange(sub)[None, None, :, None])
    th = 2.0 * np.pi * ((k1[None, :, None, None] * t_idx) % n) / n

    def expand(m):
        return np.einsum('gksi,st->gksit', m, np.eye(sub)).reshape(groups, n1e * sub, n1h * sub)

    fwd1 = np.concatenate([expand(np.cos(th)), expand(-np.sin(th))], axis=1)
    weight = np.where((k1 == 0) | (k1 == n1 // 2), 1.0, 2.0)[None, :, None, None] / n
    inv1 = np.concatenate([expand(weight * np.cos(th)), expand(-weight * np.sin(th))], axis=1).transpose(0, 2, 1)
    inv1 = np.pad(inv1, ((0, 0), (0, 0), (0, -inv1.shape[2] % LANES)))
    idx = np.arange(n2, dtype=np.int64)
    th2 = 2.0 * np.pi * ((idx[:, None] * idx[None, :]) % n2) / n2
    c2, s2 = np.cos(th2), np.sin(th2)
    fwd2 = np.block([[c2, s2], [-s2, c2]])
    inv2 = np.block([[c2, -s2], [s2, c2]])
    return {name: tab.astype(np.float32).astype(BF16)
            for name, tab in (('fwd1', fwd1), ('inv1', inv1), ('fwd2', fwd2), ('inv2', inv2))}


DFT_ROWS = 2 * SUBLANES


def _dft1_body(x_ref, t_ref, o_ref):
    n1h, _, d = x_ref.shape
    x = x_ref[...]
    halves = [_dot(t_ref[g], x[:, g * SUBLANES:(g + 1) * SUBLANES, :].reshape(n1h * SUBLANES, d).astype(BF16))
              .reshape(o_ref.shape[:2] + (SUBLANES, d)) for g in range(DFT_ROWS // SUBLANES)]
    o_ref[...] = jnp.concatenate(halves, axis=2).astype(o_ref.dtype)


def _dft1(x, length, tabs):
    bx, _, d = x.shape
    n, n1, n2, n1e = _dft_plan(length)
    tab = tabs['fwd1']
    a = pl.pallas_call(
        _dft1_body, grid=(n2 // DFT_ROWS, bx),
        in_specs=[pl.BlockSpec((None, n1 // 2, None, DFT_ROWS, d), lambda j, b: (b, 0, j, 0, 0)),
                  pl.BlockSpec((DFT_ROWS // SUBLANES,) + tab.shape[1:], lambda j, b: (j, 0, 0))],
        out_specs=pl.BlockSpec((None, 2, n1e, None, DFT_ROWS, d), lambda j, b: (b, 0, 0, j, 0, 0)),
        out_shape=jax.ShapeDtypeStruct((bx, 2, n1e, n2 // DFT_ROWS, DFT_ROWS, d), BF16),
        compiler_params=_params("parallel", "arbitrary"), name="hy_dft1")(
            x.reshape(bx, n1 // 2, n2 // DFT_ROWS, DFT_ROWS, d), tab)
    return a.reshape(bx, 2, n1e, n2, d)


def _filter_spectrum_body(a_ref, f_ref, o_ref):
    n2 = o_ref.shape[1]
    d = o_ref.shape[2]
    xf = _dot(f_ref[...], a_ref[0].reshape(2 * n2, d))
    xb = _dot(f_ref[...], a_ref[1].reshape(2 * n2, d))
    o_ref[0] = (xf[:n2] + xb[:n2]).astype(o_ref.dtype)
    o_ref[1] = (xf[n2:] - xb[n2:]).astype(o_ref.dtype)


def _filter_spectrum(k, length, tabs):
    d = k.shape[2]
    n, n1, n2, n1e = _dft_plan(length)
    a = _dft1(k, length, tabs)
    return pl.pallas_call(
        _filter_spectrum_body, grid=(n1e,),
        in_specs=[pl.BlockSpec((2, 2, None, n2, d), lambda k: (0, 0, k, 0, 0)), _resident(tabs['fwd2'].shape)],
        out_specs=pl.BlockSpec((2, None, n2, d), lambda k: (0, k, 0, 0)),
        out_shape=jax.ShapeDtypeStruct((2, n1e, n2, d), BF16),
        compiler_params=_params("parallel"), name="hy_filter_spectrum")(a, tabs['fwd2'])


def _spectral_body(a_ref, k_ref, f_ref, i_ref, o_ref):
    bb, _, n2, d = o_ref.shape
    a = jnp.concatenate([a_ref[b].reshape(2 * n2, d) for b in range(bb)], axis=1)
    x = _dot(f_ref[...], a)
    xr, xi = x[:n2], x[n2:]
    kr = jnp.concatenate([k_ref[0]] * bb, axis=1)
    ki = jnp.concatenate([k_ref[1]] * bb, axis=1)
    y = jnp.concatenate([xr * kr - xi * ki, xr * ki + xi * kr], axis=0).astype(BF16)
    z = _dot(i_ref[...], y)
    for b in range(bb):
        o_ref[b] = z[:, b * d:(b + 1) * d].reshape(2, n2, d).astype(o_ref.dtype)


def _spectral_multiply(a, kc, length, tabs):
    bx = a.shape[0]
    _, n1e, n2, d = kc.shape
    bb = max(b for b in range(1, bx + 1) if bx % b == 0 and b * n2 <= 2 * MXU_DIM)
    blk = pl.BlockSpec((bb, 2, None, n2, d), lambda k, b: (b, 0, k, 0, 0))
    return pl.pallas_call(
        _spectral_body, grid=(n1e, bx // bb),
        in_specs=[blk, pl.BlockSpec((2, None, n2, d), lambda k, b: (0, k, 0, 0)),
                  _resident(tabs['fwd2'].shape), _resident(tabs['inv2'].shape)],
        out_specs=blk, out_shape=jax.ShapeDtypeStruct((bx, 2, n1e, n2, d), BF16),
        compiler_params=_params("parallel", "arbitrary"), name="hy_spectral")(
            a, kc, tabs['fwd2'], tabs['inv2'])


def _idft1_body(z_ref, t_ref, o_ref):
    _, n1e, _, d = z_ref.shape
    rows = 2 * n1e * SUBLANES
    z = z_ref[...].astype(F32)
    halves = []
    for g in range(DFT_ROWS // SUBLANES):
        zg = z[:, :, g * SUBLANES:(g + 1) * SUBLANES, :].reshape(rows, d).astype(BF16)
        zg = jnp.concatenate([zg, jnp.zeros((t_ref.shape[2] - rows, d), BF16)], axis=0)
        halves.append(_dot(t_ref[g], zg).reshape(o_ref.shape[0], SUBLANES, d))
    o_ref[...] = jnp.concatenate(halves, axis=1).astype(o_ref.dtype)


def _idft1(z, length, tabs):
    bx, _, n1e, n2, d = z.shape
    n, n1, _, _ = _dft_plan(length)
    tab = tabs['inv1']
    y = pl.pallas_call(
        _idft1_body, grid=(n2 // DFT_ROWS, bx),
        in_specs=[pl.BlockSpec((None, 2, n1e, None, DFT_ROWS, d), lambda j, b: (b, 0, 0, j, 0, 0)),
                  pl.BlockSpec((DFT_ROWS // SUBLANES,) + tab.shape[1:], lambda j, b: (j, 0, 0))],
        out_specs=pl.BlockSpec((None, n1 // 2, None, DFT_ROWS, d), lambda j, b: (b, 0, j, 0, 0)),
        out_shape=jax.ShapeDtypeStruct((bx, n1 // 2, n2 // DFT_ROWS, DFT_ROWS, d), BF16),
        compiler_params=_params("parallel", "arbitrary"), name="hy_idft1")(
            z.reshape(bx, 2, n1e, n2 // DFT_ROWS, DFT_ROWS, d), tab)
    return y.reshape(bx, length, d)


def _hyena_mixer(h, g, prm, seq_len):
    t, d = h.shape
    bsz = t // seq_len
    tabs = _dft_tables(seq_len)
    x0, vv = _hy_in(h, g, prm['in_w'], prm['in_b'], prm['conv_w'], prm['conv_b'], seq_len)
    kc = _filter_spectrum(_hy_filters(seq_len, prm, d), seq_len, tabs)
    a = _dft1(vv.reshape(bsz, seq_len, d), seq_len, tabs)
    z = _spectral_multiply(a, kc, seq_len, tabs)
    yc = _idft1(z, seq_len, tabs).reshape(t, d)
    return h, ('gated', [yc, vv, x0], [prm['skip'], prm['out_w'], prm['out_b']])


def _trunk(x, p, w, depth):
    bsz, seq_len, d = x.shape
    h = x.reshape(bsz * seq_len, d)
    p = p.reshape(depth, bsz * seq_len, -1)
    for i in range(depth):
        kind, j = i % N_MIXERS, i // N_MIXERS
        h = _ffn(h, w['ln_ffn1'][i], w['ffn1_wg'][i], w['ffn1_wu'][i], w['ffn1_wd'][i])
        mix = (_hyena_mixer, _na_mixer, _lru_mixer)[kind]
        h, pending = mix(h, w['ln_mix'][i], w[('hy', 'na', 'lru')[kind]][j], seq_len)
        h = _ffn(h, w['ln_ffn2'][i], w['ffn2_wg'][i], w['ffn2_wu'][i], w['ffn2_wd'][i], pre=pending,
                 ple=(w['ln_ple'][i], w['ple_gate'][i], p, i, w['ple_proj'][i]),
                 final=w['ln_final'] if i == depth - 1 else None)
    return h.reshape(bsz, seq_len, d)


def kernel(x_prompt, x_sample, p_prompt, p_sample, ln_ffn1, ffn1_wg, ffn1_wu, ffn1_wd, ln_mix, ln_ffn2, ffn2_wg, ffn2_wu, ffn2_wd, ln_ple, ple_gate, ple_proj, ln_final, hy_in_w, hy_in_b, hy_conv_w, hy_conv_b, hy_f_w1, hy_f_b1, hy_f_w2, hy_f_b2, hy_f_w3, hy_f_b3, hy_f_freq, hy_f_out, hy_skip, hy_out_w, hy_out_b, na_qkv_w, na_qkv_b, na_rpb, na_out_w, na_out_b, lru_in_w, lru_in_b, lru_conv_w, lru_conv_b, lru_wa, lru_ba, lru_wx, lru_bx, lru_lambda, lru_out_w, lru_out_b):
    depth = ln_ffn1.shape[0]
    d = x_prompt.shape[-1]
    b16 = lambda a: a.astype(BF16)
    w = {
        'ln_ffn1': ln_ffn1, 'ffn1_wg': b16(ffn1_wg), 'ffn1_wu': b16(ffn1_wu), 'ffn1_wd': b16(ffn1_wd),
        'ln_mix': ln_mix, 'ln_ffn2': ln_ffn2, 'ffn2_wg': b16(ffn2_wg), 'ffn2_wu': b16(ffn2_wu),
        'ffn2_wd': b16(ffn2_wd), 'ln_ple': ln_ple, 'ple_gate': b16(ple_gate), 'ple_proj': b16(ple_proj),
        'ln_final': ln_final,
        'hy': [dict(in_w=b16(hy_in_w[j]), in_b=hy_in_b[j], conv_w=hy_conv_w[j], conv_b=hy_conv_b[j],
                    f_w1=hy_f_w1[j], f_b1=hy_f_b1[j], f_w2=hy_f_w2[j], f_b2=hy_f_b2[j], f_w3=hy_f_w3[j],
                    f_b3=hy_f_b3[j], f_freq=hy_f_freq[j], f_out=hy_f_out[j], skip=hy_skip[j],
                    out_w=b16(hy_out_w[j]), out_b=hy_out_b[j]) for j in range(hy_in_w.shape[0])],
        'na': [dict(wq=b16(na_qkv_w[j][:, :d]), wkt=b16(na_qkv_w[j][:, d:2 * d].T), wv=b16(na_qkv_w[j][:, 2 * d:]),
                    bq=na_qkv_b[j][:d], bk=na_qkv_b[j][d:2 * d], bv=na_qkv_b[j][2 * d:],
                    bias=_na_bias_table(na_rpb[j]), out_w=b16(na_out_w[j]), out_b=na_out_b[j])
               for j in range(na_qkv_w.shape[0])],
        'lru': [dict(in_w=b16(lru_in_w[j]), in_b=lru_in_b[j], conv_w=lru_conv_w[j], conv_b=lru_conv_b[j],
                     wa=b16(lru_wa[j]), ba=lru_ba[j], wx=b16(lru_wx[j]), bx=lru_bx[j],
                     c8=-LRU_C * jax.nn.softplus(-lru_lambda[j].astype(F32)),
                     out_w=b16(lru_out_w[j]), out_b=lru_out_b[j]) for j in range(lru_in_w.shape[0])],
    }
    return _trunk(x_prompt, p_prompt, w, depth), _trunk(x_sample, p_sample, w, depth)
```

```python
import functools
import math

import numpy as np
import jax
import jax.numpy as jnp
from jax import lax
from jax.experimental import pallas as pl
from jax.experimental.pallas import tpu as pltpu

F32 = jnp.float32
BF16 = jnp.bfloat16

RMS_EPS = 1e-6
NEG_INF = -1e30
LANES = 128
SUBLANES = 8
MXU_DIM = 256
VMEM_LIMIT = 56 * 1024 * 1024

GRID_W = 64
NA_HEADS = 32
NA_KH = 8
NA_KW = 16
LRU_BLOCKS = 4
LRU_CONV = 4
LRU_C = 8.0
HY_EMB = 33
HY_SHORT = 3
HY_PACK = LANES // 2
HY_DECAY_TARGET = 1e-2
HY_DECAY_PCT_HI = 0.3
HY_DECAY_PCT_LO = 1.5
N_MIXERS = 3

TOKEN_TILE = 512
FFN_TILE = 1024
HALO = SUBLANES


def _params(*sem):
    return pltpu.CompilerParams(dimension_semantics=sem, vmem_limit_bytes=VMEM_LIMIT)


def _resident(shape):
    nd = len(shape)
    return pl.BlockSpec(shape, lambda *_: (0,) * nd, pipeline_mode=pl.Buffered(1))


def _resident_layer(stacked, layer):
    nd = stacked.ndim - 1
    return pl.BlockSpec((None,) + stacked.shape[1:], lambda *_: (layer,) + (0,) * nd, pipeline_mode=pl.Buffered(1))


def _rms(x, g):
    ms = jnp.mean(x * x, axis=-1, keepdims=True)
    return x * lax.rsqrt(ms + RMS_EPS) * g


def _dot(a, b):
    return jnp.dot(a, b, preferred_element_type=F32)


def _sigmoid(x):
    return 0.5 * jnp.tanh(0.5 * x) + 0.5


def _row(v):
    return v.reshape(1, -1).astype(F32)


def _ffn_body(*refs, pre, ple, final, n_chunks):
    h_ref, g_ref, wg_ref, wu_ref, wd_ref = refs[:5]
    o_ref = refs[-1]
    h = h_ref[...]
    k = 5
    if pre == 'gated':
        y_ref, v_ref, x0_ref, sk_ref, w_ref, b_ref = refs[k:k + 6]
        k += 6
        y = (y_ref[...] + v_ref[...] * sk_ref[...]) * x0_ref[...]
        h = h + _dot(y.astype(BF16), w_ref[...]) + b_ref[...]
    elif pre == 'plain':
        x_ref, w_ref, b_ref = refs[k:k + 3]
        k += 3
        h = h + _dot(x_ref[...], w_ref[...]) + b_ref[...]
    xn = _rms(h, g_ref[...]).astype(BF16)
    n_tiles = pl.cdiv(wg_ref.shape[1], MXU_DIM)
    bounds = [min(-(-n_tiles * c // n_chunks) * MXU_DIM, wg_ref.shape[1]) for c in range(n_chunks + 1)]
    acc = jnp.zeros_like(h)
    for c0, c1 in zip(bounds[:-1], bounds[1:]):
        gt = _dot(xn, wg_ref[:, c0:c1])
        up = _dot(xn, wu_ref[:, c0:c1])
        act = (gt * jax.nn.sigmoid(gt) * up).astype(BF16)
        acc = acc + _dot(act, wd_ref[c0:c1, :])
    h = h + 0.5 * acc
    if ple:
        gp_ref, wpg_ref, p_ref, wpp_ref = refs[k:k + 4]
        k += 4
        gate = jax.nn.sigmoid(_dot(_rms(h, gp_ref[...]).astype(BF16), wpg_ref[...]))
        h = h + gate * _dot(p_ref[...].astype(BF16), wpp_ref[...])
    if final:
        h = _rms(h, refs[k][...])
    o_ref[...] = h


def _ffn(h, layer, g, wg, wu, wd, pre=None, ple=None, final=None):
    t, d = h.shape
    tm = FFN_TILE if pre is None else TOKEN_TILE
    tile = lambda width: pl.BlockSpec((tm, width), lambda i: (i, 0))
    args = [h, _row(g[layer]), wg, wu, wd]
    specs = [tile(d), _resident((1, d))] + [_resident_layer(w, layer) for w in (wg, wu, wd)]
    kind = None
    if pre is not None:
        kind, per_token, shared = pre
        args += list(per_token) + [op if op.ndim == 2 else _row(op) for op in shared]
        specs += [tile(op.shape[1]) for op in per_token] + [_resident(op.shape) for op in args[-len(shared):]]
    if ple is not None:
        gp, wpg, p, wpp = ple
        args += [_row(gp[layer]), wpg, p, wpp]
        specs += [_resident((1, d)), _resident_layer(wpg, layer),
                  pl.BlockSpec((None, tm, p.shape[2]), lambda i: (layer, i, 0)), _resident_layer(wpp, layer)]
    if final is not None:
        args.append(_row(final))
        specs.append(_resident((1, d)))
    return pl.pallas_call(
        functools.partial(_ffn_body, pre=kind, ple=ple is not None, final=final is not None, n_chunks=2),
        grid=(t // tm,), in_specs=specs, out_specs=tile(d),
        out_shape=jax.ShapeDtypeStruct((t, d), F32),
        compiler_params=_params("parallel"), name="ffn")(*args)


def _halo_specs(tm, d, n_rows):
    per = tm // HALO
    last = n_rows // HALO - 1
    return [pl.BlockSpec((HALO, d), lambda i: (jnp.maximum(i * per - 1, 0), 0)),
            pl.BlockSpec((tm, d), lambda i: (i, 0)),
            pl.BlockSpec((HALO, d), lambda i: (jnp.minimum((i + 1) * per, last), 0))]


def _normed_with_halo(hp_ref, hc_ref, hn_ref, g_ref):
    x = jnp.concatenate([hp_ref[...], hc_ref[...], hn_ref[...]], axis=0)
    return _rms(x, g_ref[...]).astype(BF16)


def _store_zero_padded(u_s, u, tiles_per_seq, tm):
    i = pl.program_id(0) % tiles_per_seq
    u_s[HALO:tm + HALO, :] = u[HALO:tm + HALO]
    u_s[0:HALO, :] = jnp.where(i == 0, 0.0, u[0:HALO])
    u_s[tm + HALO:tm + 2 * HALO, :] = jnp.where(i == tiles_per_seq - 1, 0.0, u[tm + HALO:tm + 2 * HALO])


def _lru_in_body(hp_ref, hc_ref, hn_ref, g_ref, w_ref, b_ref, cw_ref, cb_ref, gb_ref, xb_ref, u_s,
                 *, tiles_per_seq, tm):
    r_w = gb_ref.shape[1]
    xn = _normed_with_halo(hp_ref, hc_ref, hn_ref, g_ref)
    gb_ref[...] = jax.nn.gelu(_dot(xn[HALO:HALO + tm], w_ref[:, :r_w]) + b_ref[:, :r_w]).astype(BF16)
    u = _dot(xn, w_ref[:, r_w:]) + b_ref[:, r_w:]
    _store_zero_padded(u_s, u, tiles_per_seq, tm)
    left = LRU_CONV // 2
    y = cb_ref[...]
    for j in range(LRU_CONV):
        y = y + cw_ref[j:j + 1, :] * u_s[pl.ds(HALO + j - left, tm), :]
    xb_ref[...] = y


def _lru_in(h, g, w, b, cw, cb, seq_len, tm=TOKEN_TILE):
    t, d = h.shape
    r_w = cw.shape[1]
    tile = pl.BlockSpec((tm, r_w), lambda i: (i, 0))
    return pl.pallas_call(
        functools.partial(_lru_in_body, tiles_per_seq=seq_len // tm, tm=tm),
        grid=(t // tm,),
        in_specs=_halo_specs(tm, d, t) + [_resident((1, d)), _resident(w.shape), _resident((1, 2 * r_w)),
                                          _resident(cw.shape), _resident((1, r_w))],
        out_specs=[tile, tile],
        out_shape=[jax.ShapeDtypeStruct((t, r_w), BF16), jax.ShapeDtypeStruct((t, r_w), F32)],
        scratch_shapes=[pltpu.VMEM((tm + 2 * HALO, r_w), F32)],
        compiler_params=_params("parallel"), name="lru_in")(h, h, h, _row(g), w, _row(b), cw, _row(cb))


def _block_diag(x, w_ref):
    nb, bw, _ = w_ref.shape
    return jnp.concatenate([_dot(x[:, n * bw:(n + 1) * bw], w_ref[n]) for n in range(nb)], axis=1)


def _lru_scan_tile(a, b, a_s, b_s, h_s, carry_ref, reverse):
    tm, width = a.shape
    n_groups = tm // SUBLANES
    a = a.reshape(n_groups, SUBLANES, width)
    b = b.reshape(n_groups, SUBLANES, width)
    row = lax.broadcasted_iota(jnp.int32, (1, SUBLANES, 1), 1)
    for dist in (1, 2, 4):
        if reverse:
            shift, ok = SUBLANES - dist, row < SUBLANES - dist
        else:
            shift, ok = dist, row >= dist
        a_prev = jnp.where(ok, pltpu.roll(a, shift, 1), 1.0)
        b_prev = jnp.where(ok, pltpu.roll(b, shift, 1), 0.0)
        b = a * b_prev + b
        a = a * a_prev
    a_s[...] = a.reshape(tm, width)
    b_s[...] = b.reshape(tm, width)
    edge = 0 if reverse else SUBLANES - 1

    def group(k, carry):
        gi = n_groups - 1 - k if reverse else k
        r0 = pl.multiple_of(gi * SUBLANES, SUBLANES)
        hg = a_s[pl.ds(r0, SUBLANES), :] * carry + b_s[pl.ds(r0, SUBLANES), :]
        h_s[pl.ds(r0, SUBLANES), :] = hg
        return hg[edge:edge + 1, :]

    carry_ref[...] = lax.fori_loop(0, n_groups, group, carry_ref[...], unroll=True)


def _lru_gates(xb, wa_ref, ba_ref, wx_ref, bx_ref, c8_ref, first_row):
    xb16 = xb.astype(BF16)
    gate_x = _sigmoid(_block_diag(xb16, wx_ref) + bx_ref[...])
    gate_a = _sigmoid(_block_diag(xb16, wa_ref) + ba_ref[...])
    log_a = gate_a * c8_ref[...]
    a = jnp.exp(log_a)
    y = 1.0 - a * a
    mult = jnp.where(y > 0.0, y * lax.rsqrt(y), 0.0)
    r = lax.broadcasted_iota(jnp.int32, (xb.shape[0], 1), 0)
    mult = jnp.where(r == first_row, 1.0, mult)
    return a, mult * gate_x * xb


def _lru_fwd_body(xb_ref, wa_ref, ba_ref, wx_ref, bx_ref, c8_ref, h0_ref, a_s, b_s, carry, *, tm):
    ti = pl.program_id(1)

    @pl.when(ti == 0)
    def _():
        carry[...] = jnp.zeros_like(carry)

    first_row = jnp.where(ti == 0, 0, -1)
    a, b = _lru_gates(xb_ref[...], wa_ref, ba_ref, wx_ref, bx_ref, c8_ref, first_row)
    _lru_scan_tile(a, b, a_s, b_s, h0_ref, carry, reverse=False)


def _lru_bwd_body(xb_ref, wa_ref, ba_ref, wx_ref, bx_ref, c8_ref, h0_ref, gb_ref, h_ref, wo_ref, bo_ref,
                  o_ref, a_s, b_s, h1_s, carry, *, tm):
    ti = pl.program_id(1)

    @pl.when(ti == 0)
    def _():
        carry[...] = jnp.zeros_like(carry)

    first_row = jnp.where(ti == 0, tm - 1, -1)
    a, b = _lru_gates(xb_ref[...], wa_ref, ba_ref, wx_ref, bx_ref, c8_ref, first_row)
    _lru_scan_tile(a, b, a_s, b_s, h1_s, carry, reverse=True)
    y = h0_ref[...] + h1_s[...]
    z = (gb_ref[...].astype(F32) * y).astype(BF16)
    o_ref[...] = h_ref[...] + _dot(z, wo_ref[...]) + bo_ref[...]


def _lru_mixer(h, g, prm, seq_len, tm=TOKEN_TILE):
    t, d = h.shape
    bsz = t // seq_len
    nt = seq_len // tm
    gb, xb = _lru_in(h, g, prm['in_w'], prm['in_b'], prm['conv_w'], prm['conv_b'], seq_len, tm)
    r_w = xb.shape[1]
    bw = r_w // LRU_BLOCKS
    gate_specs = lambda dr: [_resident((LRU_BLOCKS, bw, bw)), _resident((1, r_w)),
                             _resident((LRU_BLOCKS, bw, bw)), _resident((1, r_w)), _resident((1, r_w))]
    gate_args = lambda dr: [prm['wa'][dr], _row(prm['ba'][dr]), prm['wx'][dr], _row(prm['bx'][dr]),
                            _row(prm['c8'][dr])]
    fwd = lambda b, i: (b * nt + i, 0)
    bwd = lambda b, i: (b * nt + nt - 1 - i, 0)
    scan_scratch = [pltpu.VMEM((tm, r_w), F32), pltpu.VMEM((tm, r_w), F32)]
    h0 = pl.pallas_call(
        functools.partial(_lru_fwd_body, tm=tm), grid=(bsz, nt),
        in_specs=[pl.BlockSpec((tm, r_w), fwd)] + gate_specs(0),
        out_specs=pl.BlockSpec((tm, r_w), fwd), out_shape=jax.ShapeDtypeStruct((t, r_w), F32),
        scratch_shapes=scan_scratch + [pltpu.VMEM((1, r_w), F32)],
        compiler_params=_params("arbitrary", "arbitrary"), name="lru_fwd")(xb, *gate_args(0))
    out = pl.pallas_call(
        functools.partial(_lru_bwd_body, tm=tm), grid=(bsz, nt),
        in_specs=[pl.BlockSpec((tm, r_w), bwd)] + gate_specs(1)
        + [pl.BlockSpec((tm, r_w), bwd), pl.BlockSpec((tm, r_w), bwd), pl.BlockSpec((tm, d), bwd),
           _resident((r_w, d)), _resident((1, d))],
        out_specs=pl.BlockSpec((tm, d), bwd), out_shape=jax.ShapeDtypeStruct((t, d), F32),
        scratch_shapes=scan_scratch + [pltpu.VMEM((tm, r_w), F32), pltpu.VMEM((1, r_w), F32)],
        compiler_params=_params("arbitrary", "arbitrary"), name="lru_bwd")(
            xb, *gate_args(1), h0, gb, h, prm['out_w'], _row(prm['out_b']))
    return out, None


NA_PAIR = 2 * GRID_W
NA_WIN = 5
NA_PB = 16
NA_HG = LANES // (1024 // NA_HEADS)


def _na_qkv_body(h_ref, g_ref, wq_ref, wkt_ref, wv_ref, bq_ref, bk_ref, bv_ref, q_ref, kt_ref, v_ref, *, scale):
    xn = _rms(h_ref[...], g_ref[...]).astype(BF16)
    q_ref[...] = ((_dot(xn, wq_ref[...]) + bq_ref[...]) * scale).astype(BF16)
    kt = lax.dot_general(wkt_ref[...], xn, (((1,), (1,)), ((), ())), preferred_element_type=F32)
    kt_ref[...] = (kt + bk_ref[...]).astype(BF16)
    v_ref[...] = (_dot(xn, wv_ref[...]) + bv_ref[...]).astype(BF16)


def _na_qkv(h, g, wq, wkt, wv, bq, bk, bv, seq_len, scale, tm=TOKEN_TILE):
    t, d = h.shape
    bsz, nt = t // seq_len, seq_len // tm
    tile = pl.BlockSpec((tm, d), lambda i: (i, 0))
    return pl.pallas_call(
        functools.partial(_na_qkv_body, scale=scale), grid=(t // tm,),
        in_specs=[tile, _resident((1, d)), _resident((d, d)), _resident((d, d)), _resident((d, d)),
                  _resident((1, d)), _resident((d, 1)), _resident((1, d))],
        out_specs=[tile, pl.BlockSpec((None, d, tm), lambda i: (i // nt, 0, i % nt)), tile],
        out_shape=[jax.ShapeDtypeStruct((t, d), BF16), jax.ShapeDtypeStruct((bsz, d, seq_len), BF16),
                   jax.ShapeDtypeStruct((t, d), BF16)],
        compiler_params=_params("parallel"), name="na_qkv")(
            h, _row(g), wq, wkt, wv, _row(bq), bk.reshape(d, 1).astype(F32), _row(bv))


def _na_attn_body(q_ref, kp_ref, kc_ref, kn_ref, vp_ref, vc_ref, vn_ref, sb_ref, o_ref, kwin, vwin,
                  *, n_pairs, rows):
    pb = pl.program_id(2)
    blk = NA_PB * NA_PAIR
    for bi, (k_ref, v_ref) in enumerate(((kp_ref, vp_ref), (kc_ref, vc_ref), (kn_ref, vn_ref))):
        for t in range(NA_PB):
            kwin[bi * NA_PB + t] = k_ref[:, t * NA_PAIR:(t + 1) * NA_PAIR]
        vwin[bi * blk:(bi + 1) * blk, :] = v_ref[...]
    dh = LANES // NA_HG
    pad = 2 * SUBLANES
    lane = lax.broadcasted_iota(jnp.int32, (1, LANES), 1)
    in_head = [(lane // dh) == j for j in range(NA_HG)]
    q_row_onehot = (lax.broadcasted_iota(jnp.int32, (NA_PAIR, pad), 0) // GRID_W
                    == lax.broadcasted_iota(jnp.int32, (NA_PAIR, pad), 1)).astype(BF16)
    mask_row = lax.broadcasted_iota(jnp.int32, (pad, 1), 0)
    k_row_in_win = lax.broadcasted_iota(jnp.int32, (1, NA_WIN * NA_PAIR), 1) // GRID_W

    def pair(i, carry):
        p = pb * NA_PB + i
        kp0 = jnp.clip(p - 2, 0, n_pairs - NA_WIN)
        off = kp0 - (pb - 1) * NA_PB
        m0 = kp0 - p + (NA_WIN - 1)
        qi = q_ref[pl.ds(pl.multiple_of(i * NA_PAIR, NA_PAIR), NA_PAIR), :]
        ksl = jnp.concatenate([kwin[off + jp] for jp in range(NA_WIN)], axis=1)
        vsl = vwin[pl.ds(pl.multiple_of(off * NA_PAIR, NA_PAIR), NA_WIN * NA_PAIR), :]
        r_start = jnp.clip(2 * p + mask_row - NA_KH // 2, 0, rows - NA_KH)
        k_row = 2 * kp0 + k_row_in_win
        in_window = (k_row >= r_start) & (k_row < r_start + NA_KH)
        row_mask = jnp.where(in_window | (mask_row >= 2), 0.0, NEG_INF).astype(BF16)
        q_heads = jnp.concatenate(
            [jnp.concatenate([jnp.where(m, qi, jnp.zeros_like(qi)), q_row_onehot], axis=1) for m in in_head], axis=0)
        bias = jnp.concatenate(
            [jnp.concatenate([sb_ref[j, m0 + jp] for jp in range(NA_WIN)], axis=1) for j in range(NA_HG)], axis=0)
        s = _dot(q_heads, jnp.concatenate([ksl, row_mask], axis=0)) + bias
        e = jnp.exp(s - jnp.max(s, axis=-1, keepdims=True))
        denom = jnp.sum(e, axis=-1, keepdims=True)
        pv = _dot(e.astype(BF16), vsl) / denom
        acc = jnp.zeros((NA_PAIR, LANES), F32)
        for j, m in enumerate(in_head):
            acc = acc + jnp.where(m, pv[j * NA_PAIR:(j + 1) * NA_PAIR], 0.0)
        o_ref[pl.ds(pl.multiple_of(i * NA_PAIR, NA_PAIR), NA_PAIR), :] = acc.astype(BF16)
        return carry

    lax.fori_loop(0, NA_PB, pair, 0, unroll=True)


def _na_bias_table(rpb):
    n_h, n_dr, n_dc = rpb.shape
    col = np.arange(GRID_W)
    q_start = np.clip(col - NA_KW // 2, 0, GRID_W - NA_KW)
    col_ok = (col[None, :] >= q_start[:, None]) & (col[None, :] < q_start[:, None] + NA_KW)
    dc_idx = col[None, :] - col[:, None] + NA_KW - 1
    onehot = ((np.arange(n_dc)[:, None, None] == dc_idx[None]) & col_ok[None]).reshape(n_dc, -1)
    tab = jnp.dot(rpb.astype(F32).reshape(n_h * n_dr, n_dc), jnp.asarray(onehot, F32),
                  precision=lax.Precision.HIGHEST)
    tab = tab.reshape(n_h, n_dr, GRID_W, GRID_W) + jnp.where(col_ok, 0.0, NEG_INF).astype(F32)
    masked = jnp.full((n_h, GRID_W, GRID_W), NEG_INF, F32)

    def quadrant(dr):
        return tab[:, dr + NA_KH - 1] if abs(dr) <= NA_KH - 1 else masked

    tiles = [jnp.concatenate([jnp.concatenate([quadrant(2 * (m - (NA_WIN - 1)) + c - a) for c in range(2)], axis=2)
                              for a in range(2)], axis=1) for m in range(2 * (NA_WIN - 1) + 1)]
    sb = jnp.stack(tiles, axis=1)
    return sb.reshape(n_h // NA_HG, NA_HG, *sb.shape[1:])


def _na_mixer(h, g, prm, seq_len):
    t, d = h.shape
    bsz = t // seq_len
    rows = seq_len // GRID_W
    n_pairs = rows // 2
    nblk = n_pairs // NA_PB
    blk = NA_PB * NA_PAIR
    q, kt, v = _na_qkv(h, g, prm['wq'], prm['wkt'], prm['wv'], prm['bq'], prm['bk'], prm['bv'], seq_len,
                       (d // NA_HEADS) ** -0.5)
    q = q.reshape(bsz, seq_len, d)
    v = v.reshape(bsz, seq_len, d)
    prev = lambda pb: jnp.maximum(pb - 1, 0)
    nxt = lambda pb: jnp.minimum(pb + 1, nblk - 1)
    k_spec = lambda f: pl.BlockSpec((None, LANES, blk), lambda gi, b, pb: (b, gi, f(pb)))
    v_spec = lambda f: pl.BlockSpec((None, blk, LANES), lambda gi, b, pb: (b, f(pb), gi))
    same = lambda pb: pb
    sb = prm['bias']
    o = pl.pallas_call(
        functools.partial(_na_attn_body, n_pairs=n_pairs, rows=rows),
        grid=(d // LANES, bsz, nblk),
        in_specs=[v_spec(same), k_spec(prev), k_spec(same), k_spec(nxt), v_spec(prev), v_spec(same), v_spec(nxt),
                  pl.BlockSpec((None,) + sb.shape[1:], lambda gi, b, pb: (gi, 0, 0, 0, 0))],
        out_specs=v_spec(same), out_shape=jax.ShapeDtypeStruct((bsz, seq_len, d), BF16),
        scratch_shapes=[pltpu.VMEM((3 * NA_PB, LANES, NA_PAIR), BF16), pltpu.VMEM((3 * blk, LANES), BF16)],
        compiler_params=_params("arbitrary", "arbitrary", "arbitrary"), name="na_attn")(
            q, kt, kt, kt, v, v, v, sb)
    return h, ('plain', [o.reshape(t, d)], [prm['out_w'], prm['out_b']])


def _hy_in_body(hp_ref, hc_ref, hn_ref, g_ref, w_ref, b_ref, cw_ref, cb_ref, x0_ref, vv_ref, u_s,
                *, tiles_per_seq, tm):
    d = x0_ref.shape[1]
    xn = _normed_with_halo(hp_ref, hc_ref, hn_ref, g_ref)
    left = (HY_SHORT - 1) // 2

    def conv(c0):
        u = _dot(xn, w_ref[:, c0:c0 + d]) + b_ref[:, c0:c0 + d]
        _store_zero_padded(u_s, u, tiles_per_seq, tm)
        y = cb_ref[:, c0:c0 + d]
        for j in range(HY_SHORT):
            y = y + cw_ref[j:j + 1, c0:c0 + d] * u_s[pl.ds(HALO + j - left, tm), :]
        return y

    x0_ref[...] = conv(0).astype(x0_ref.dtype)
    x1 = conv(d)
    vv_ref[...] = conv(2 * d) * x1


def _hy_in(h, g, w, b, cw, cb, seq_len, tm=TOKEN_TILE):
    t, d = h.shape
    tile = pl.BlockSpec((tm, d), lambda i: (i, 0))
    return pl.pallas_call(
        functools.partial(_hy_in_body, tiles_per_seq=seq_len // tm, tm=tm), grid=(t // tm,),
        in_specs=_halo_specs(tm, d, t) + [_resident((1, d)), _resident(w.shape), _resident((1, 3 * d)),
                                          _resident(cw.shape), _resident((1, 3 * d))],
        out_specs=[tile, tile],
        out_shape=[jax.ShapeDtypeStruct((t, d), BF16), jax.ShapeDtypeStruct((t, d), F32)],
        scratch_shapes=[pltpu.VMEM((tm + 2 * HALO, d), F32)],
        compiler_params=_params("parallel"), name="hy_in")(h, h, h, _row(g), w, _row(b), cw, _row(cb))


def _hy_filter_body(z_ref, w1_ref, b1_ref, w2_ref, b2_ref, w3_ref, b3_ref, fr_ref, wo_ref, dl_ref,
                    k_ref):
    hp = lax.Precision.HIGHEST
    half, d = z_ref.shape[0], k_ref.shape[2]
    z = z_ref[...]
    fr = fr_ref[...]
    a = jnp.sin(fr * (jnp.dot(z, w1_ref[...], precision=hp, preferred_element_type=F32) + b1_ref[...]))
    a = jnp.sin(fr * (jnp.dot(a, w2_ref[...], precision=hp, preferred_element_type=F32) + b2_ref[...]))
    a = jnp.sin(fr * (jnp.dot(a, w3_ref[...], precision=hp, preferred_element_type=F32) + b3_ref[...]))
    a = a.astype(BF16)
    r = lax.broadcasted_iota(jnp.int32, (half, 1), 0)
    for part in range(2):
        k = _dot(a, wo_ref[part])
        window = jnp.exp(-z[:, part * HY_PACK:part * HY_PACK + 1] * dl_ref[...])
        rows = slice(part * half, (part + 1) * half)
        k_ref[0, rows, :] = k[:, :d] * window
        kb = k[:, d:] * window
        if part == 0:
            kb = jnp.where((r == 0) & (pl.program_id(0) == 0), 0.0, kb)
        k_ref[1, rows, :] = kb


def _block_diag2(x):
    zero = jnp.zeros_like(x)
    return jnp.concatenate([jnp.concatenate([x, zero], axis=-1), jnp.concatenate([zero, x], axis=-1)], axis=-2)


def _pad_to(x, shape):
    return jnp.pad(x.astype(F32), [(0, s - n) for s, n in zip(shape, x.shape)])


@functools.lru_cache(maxsize=None)
def _hy_positions(length, d, tl):
    bands = (HY_EMB - 1) // 2
    t = np.linspace(0.0, 1.0, length)[:, None]
    ang = (2.0 * math.pi / length) * np.arange(length)[:, None]
    fb = np.linspace(1e-4, bands - 1, bands)[None, :]
    z = np.concatenate([t, np.cos(fb * ang), -np.sin(fb * ang)], axis=-1)
    z = np.pad(z, ((0, 0), (0, HY_PACK - z.shape[1]))).reshape(length // tl, 2, tl // 2, HY_PACK)
    z = z.transpose(0, 2, 1, 3).reshape(length // 2, 2 * HY_PACK).astype(np.float32)
    decay_min = math.log(HY_DECAY_TARGET) / HY_DECAY_PCT_LO
    decay_max = math.log(HY_DECAY_TARGET) / HY_DECAY_PCT_HI
    deltas = np.abs(np.linspace(decay_min, decay_max, d))[None, :].astype(np.float32)
    return z, deltas


def _hy_filters(length, prm, d, tl=TOKEN_TILE):
    z, deltas = _hy_positions(length, d, tl)
    sq = (HY_PACK, HY_PACK)
    twice = lambda v: jnp.tile(_pad_to(v[None], (1, HY_PACK)), (1, 2))
    w_out = _pad_to(prm['f_out'], (HY_PACK, 2 * d))
    w_out = jnp.stack([jnp.concatenate([w_out, jnp.zeros_like(w_out)]),
                       jnp.concatenate([jnp.zeros_like(w_out), w_out])]).astype(BF16)
    args = [z, _block_diag2(_pad_to(prm['f_w1'], sq)), twice(prm['f_b1']),
            _block_diag2(_pad_to(prm['f_w2'], sq)), twice(prm['f_b2']),
            _block_diag2(_pad_to(prm['f_w3'], sq)), twice(prm['f_b3']), twice(prm['f_freq']), w_out, deltas]
    return pl.pallas_call(
        _hy_filter_body, grid=(length // tl,),
        in_specs=[pl.BlockSpec((tl // 2, 2 * HY_PACK), lambda i: (i, 0))] + [_resident(a.shape) for a in args[1:]],
        out_specs=pl.BlockSpec((2, tl, d), lambda i: (0, i, 0)),
        out_shape=jax.ShapeDtypeStruct((2, length, d), F32),
        compiler_params=_params("parallel"), name="hy_filter")(*args)


def _dft_plan(length):
    n = 2 * length
    n1 = min(128, n // 16)
    return n, n1, n // n1, n1 // 2 + 1


@functools.lru_cache(maxsize=None)
def _dft_tables(length):
    n, n1, n2, n1e = _dft_plan(length)
    n1h, sub = n1 // 2, SUBLANES
    groups = n2 // sub
    k1 = np.arange(n1e)
    t_idx = (n2 * np.arange(n1h)[None, None, None, :] + sub * np.arange(groups)[:, None, None, None]
             + np.arange(sub)[None, None, :, None])
    th = 2.0 * np.pi * ((k1[None, :, None, None] * t_idx) % n) / n

    def expand(m):
        return np.einsum('gksi,st->gksit', m, np.eye(sub)).reshape(groups, n1e * sub, n1h * sub)

    fwd1 = np.concatenate([expand(np.cos(th)), expand(-np.sin(th))], axis=1)
    weight = np.where((k1 == 0) | (k1 == n1 // 2), 1.0, 2.0)[None, :, None, None] / n
    inv1 = np.concatenate([expand(weight * np.cos(th)), expand(-weight * np.sin(th))], axis=1).transpose(0, 2, 1)
    inv1 = np.pad(inv1, ((0, 0), (0, 0), (0, -inv1.shape[2] % LANES)))
    idx = np.arange(n2, dtype=np.int64)
    th2 = 2.0 * np.pi * ((idx[:, None] * idx[None, :]) % n2) / n2
    c2, s2 = np.cos(th2), np.sin(th2)
    fwd2 = np.block([[c2, s2], [-s2, c2]])
    inv2 = np.block([[c2, -s2], [s2, c2]])
    return {name: tab.astype(np.float32).astype(BF16)
            for name, tab in (('fwd1', fwd1), ('inv1', inv1), ('fwd2', fwd2), ('inv2', inv2))}


DFT_ROWS = 2 * SUBLANES


def _dft1_body(x_ref, t_ref, o_ref):
    n1h, _, d = x_ref.shape
    x = x_ref[...]
    halves = [_dot(t_ref[g], x[:, g * SUBLANES:(g + 1) * SUBLANES, :].reshape(n1h * SUBLANES, d).astype(BF16))
              .reshape(o_ref.shape[:2] + (SUBLANES, d)) for g in range(DFT_ROWS // SUBLANES)]
    o_ref[...] = jnp.concatenate(halves, axis=2).astype(o_ref.dtype)


def _dft1(x, length, tabs):
    bx, _, d = x.shape
    n, n1, n2, n1e = _dft_plan(length)
    tab = tabs['fwd1']
    a = pl.pallas_call(
        _dft1_body, grid=(n2 // DFT_ROWS, bx),
        in_specs=[pl.BlockSpec((None, n1 // 2, None, DFT_ROWS, d), lambda j, b: (b, 0, j, 0, 0)),
                  pl.BlockSpec((DFT_ROWS // SUBLANES,) + tab.shape[1:], lambda j, b: (j, 0, 0))],
        out_specs=pl.BlockSpec((None, 2, n1e, None, DFT_ROWS, d), lambda j, b: (b, 0, 0, j, 0, 0)),
        out_shape=jax.ShapeDtypeStruct((bx, 2, n1e, n2 // DFT_ROWS, DFT_ROWS, d), BF16),
        compiler_params=_params("parallel", "arbitrary"), name="hy_dft1")(
            x.reshape(bx, n1 // 2, n2 // DFT_ROWS, DFT_ROWS, d), tab)
    return a.reshape(bx, 2, n1e, n2, d)


def _filter_spectrum_body(a_ref, f_ref, o_ref):
    n2 = o_ref.shape[1]
    d = o_ref.shape[2]
    xf = _dot(f_ref[...], a_ref[0].reshape(2 * n2, d))
    xb = _dot(f_ref[...], a_ref[1].reshape(2 * n2, d))
    o_ref[0] = (xf[:n2] + xb[:n2]).astype(o_ref.dtype)
    o_ref[1] = (xf[n2:] - xb[n2:]).astype(o_ref.dtype)


def _filter_spectrum(k, length, tabs):
    d = k.shape[2]
    n, n1, n2, n1e = _dft_plan(length)
    a = _dft1(k, length, tabs)
    return pl.pallas_call(
        _filter_spectrum_body, grid=(n1e,),
        in_specs=[pl.BlockSpec((2, 2, None, n2, d), lambda k: (0, 0, k, 0, 0)), _resident(tabs['fwd2'].shape)],
        out_specs=pl.BlockSpec((2, None, n2, d), lambda k: (0, k, 0, 0)),
        out_shape=jax.ShapeDtypeStruct((2, n1e, n2, d), BF16),
        compiler_params=_params("parallel"), name="hy_filter_spectrum")(a, tabs['fwd2'])


def _spectral_body(a_ref, k_ref, f_ref, i_ref, o_ref):
    bb, _, n2, d = o_ref.shape
    a = jnp.concatenate([a_ref[b].reshape(2 * n2, d) for b in range(bb)], axis=1)
    x = _dot(f_ref[...], a)
    xr, xi = x[:n2], x[n2:]
    kr = jnp.concatenate([k_ref[0]] * bb, axis=1)
    ki = jnp.concatenate([k_ref[1]] * bb, axis=1)
    y = jnp.concatenate([xr * kr - xi * ki, xr * ki + xi * kr], axis=0).astype(BF16)
    z = _dot(i_ref[...], y)
    for b in range(bb):
        o_ref[b] = z[:, b * d:(b + 1) * d].reshape(2, n2, d).astype(o_ref.dtype)


def _spectral_multiply(a, kc, length, tabs):
    bx = a.shape[0]
    _, n1e, n2, d = kc.shape
    bb = max(b for b in range(1, bx + 1) if bx % b == 0 and b * n2 <= 2 * MXU_DIM)
    blk = pl.BlockSpec((bb, 2, None, n2, d), lambda k, b: (b, 0, k, 0, 0))
    return pl.pallas_call(
        _spectral_body, grid=(n1e, bx // bb),
        in_specs=[blk, pl.BlockSpec((2, None, n2, d), lambda k, b: (0, k, 0, 0)),
                  _resident(tabs['fwd2'].shape), _resident(tabs['inv2'].shape)],
        out_specs=blk, out_shape=jax.ShapeDtypeStruct((bx, 2, n1e, n2, d), BF16),
        compiler_params=_params("parallel", "arbitrary"), name="hy_spectral")(
            a, kc, tabs['fwd2'], tabs['inv2'])


def _idft1_body(z_ref, t_ref, o_ref):
    _, n1e, _, d = z_ref.shape
    rows = 2 * n1e * SUBLANES
    z = z_ref[...].astype(F32)
    halves = []
    for g in range(DFT_ROWS // SUBLANES):
        zg = z[:, :, g * SUBLANES:(g + 1) * SUBLANES, :].reshape(rows, d).astype(BF16)
        zg = jnp.concatenate([zg, jnp.zeros((t_ref.shape[2] - rows, d), BF16)], axis=0)
        halves.append(_dot(t_ref[g], zg).reshape(o_ref.shape[0], SUBLANES, d))
    o_ref[...] = jnp.concatenate(halves, axis=1).astype(o_ref.dtype)


def _idft1(z, length, tabs):
    bx, _, n1e, n2, d = z.shape
    n, n1, _, _ = _dft_plan(length)
    tab = tabs['inv1']
    y = pl.pallas_call(
        _idft1_body, grid=(n2 // DFT_ROWS, bx),
        in_specs=[pl.BlockSpec((None, 2, n1e, None, DFT_ROWS, d), lambda j, b: (b, 0, 0, j, 0, 0)),
                  pl.BlockSpec((DFT_ROWS // SUBLANES,) + tab.shape[1:], lambda j, b: (j, 0, 0))],
        out_specs=pl.BlockSpec((None, n1 // 2, None, DFT_ROWS, d), lambda j, b: (b, 0, j, 0, 0)),
        out_shape=jax.ShapeDtypeStruct((bx, n1 // 2, n2 // DFT_ROWS, DFT_ROWS, d), BF16),
        compiler_params=_params("parallel", "arbitrary"), name="hy_idft1")(
            z.reshape(bx, 2, n1e, n2 // DFT_ROWS, DFT_ROWS, d), tab)
    return y.reshape(bx, length, d)


def _hyena_mixer(h, g, prm, seq_len):
    t, d = h.shape
    bsz = t // seq_len
    tabs = _dft_tables(seq_len)
    x0, vv = _hy_in(h, g, prm['in_w'], prm['in_b'], prm['conv_w'], prm['conv_b'], seq_len)
    kc = _filter_spectrum(_hy_filters(seq_len, prm, d), seq_len, tabs)
    a = _dft1(vv.reshape(bsz, seq_len, d), seq_len, tabs)
    z = _spectral_multiply(a, kc, seq_len, tabs)
    yc = _idft1(z, seq_len, tabs).reshape(t, d)
    return h, ('gated', [yc, vv, x0], [prm['skip'], prm['out_w'], prm['out_b']])


def _trunk(x, p, w, depth):
    bsz, seq_len, d = x.shape
    h = x.reshape(bsz * seq_len, d)
    p = p.reshape(depth, bsz * seq_len, -1)
    for i in range(depth):
        kind, j = i % N_MIXERS, i // N_MIXERS
        h = _ffn(h, i, w['ln_ffn1'], w['ffn1_wg'], w['ffn1_wu'], w['ffn1_wd'])
        mix = (_hyena_mixer, _na_mixer, _lru_mixer)[kind]
        h, pending = mix(h, w['ln_mix'][i], w[('hy', 'na', 'lru')[kind]][j], seq_len)
        h = _ffn(h, i, w['ln_ffn2'], w['ffn2_wg'], w['ffn2_wu'], w['ffn2_wd'], pre=pending,
                 ple=(w['ln_ple'], w['ple_gate'], p, w['ple_proj']),
                 final=w['ln_final'] if i == depth - 1 else None)
    return h.reshape(bsz, seq_len, d)


def kernel(x_prompt, x_sample, p_prompt, p_sample, ln_ffn1, ffn1_wg, ffn1_wu, ffn1_wd, ln_mix, ln_ffn2, ffn2_wg, ffn2_wu, ffn2_wd, ln_ple, ple_gate, ple_proj, ln_final, hy_in_w, hy_in_b, hy_conv_w, hy_conv_b, hy_f_w1, hy_f_b1, hy_f_w2, hy_f_b2, hy_f_w3, hy_f_b3, hy_f_freq, hy_f_out, hy_skip, hy_out_w, hy_out_b, na_qkv_w, na_qkv_b, na_rpb, na_out_w, na_out_b, lru_in_w, lru_in_b, lru_conv_w, lru_conv_b, lru_wa, lru_ba, lru_wx, lru_bx, lru_lambda, lru_out_w, lru_out_b):
    depth = ln_ffn1.shape[0]
    d = x_prompt.shape[-1]
    b16 = lambda a: a.astype(BF16)
    w = {
        'ln_ffn1': ln_ffn1, 'ffn1_wg': b16(ffn1_wg), 'ffn1_wu': b16(ffn1_wu), 'ffn1_wd': b16(ffn1_wd),
        'ln_mix': ln_mix, 'ln_ffn2': ln_ffn2, 'ffn2_wg': b16(ffn2_wg), 'ffn2_wu': b16(ffn2_wu),
        'ffn2_wd': b16(ffn2_wd), 'ln_ple': ln_ple, 'ple_gate': b16(ple_gate), 'ple_proj': b16(ple_proj),
        'ln_final': ln_final,
        'hy': [dict(in_w=b16(hy_in_w[j]), in_b=hy_in_b[j], conv_w=hy_conv_w[j], conv_b=hy_conv_b[j],
                    f_w1=hy_f_w1[j], f_b1=hy_f_b1[j], f_w2=hy_f_w2[j], f_b2=hy_f_b2[j], f_w3=hy_f_w3[j],
                    f_b3=hy_f_b3[j], f_freq=hy_f_freq[j], f_out=hy_f_out[j], skip=hy_skip[j],
                    out_w=b16(hy_out_w[j]), out_b=hy_out_b[j]) for j in range(hy_in_w.shape[0])],
        'na': [dict(wq=b16(na_qkv_w[j][:, :d]), wkt=b16(na_qkv_w[j][:, d:2 * d].T), wv=b16(na_qkv_w[j][:, 2 * d:]),
                    bq=na_qkv_b[j][:d], bk=na_qkv_b[j][d:2 * d], bv=na_qkv_b[j][2 * d:],
                    bias=_na_bias_table(na_rpb[j]), out_w=b16(na_out_w[j]), out_b=na_out_b[j])
               for j in range(na_qkv_w.shape[0])],
        'lru': [dict(in_w=b16(lru_in_w[j]), in_b=lru_in_b[j], conv_w=lru_conv_w[j], conv_b=lru_conv_b[j],
                     wa=b16(lru_wa[j]), ba=lru_ba[j], wx=b16(lru_wx[j]), bx=lru_bx[j],
                     c8=-LRU_C * jax.nn.softplus(-lru_lambda[j].astype(F32)),
                     out_w=b16(lru_out_w[j]), out_b=lru_out_b[j]) for j in range(lru_in_w.shape[0])],
    }
    return _trunk(x_prompt, p_prompt, w, depth), _trunk(x_sample, p_sample, w, depth)
```

```python
import functools
import math

import numpy as np
import jax
import jax.numpy as jnp
from jax import lax
from jax.experimental import pallas as pl
from jax.experimental.pallas import tpu as pltpu

F32 = jnp.float32
BF16 = jnp.bfloat16

RMS_EPS = 1e-6
NEG_INF = -1e30
LANES = 128
SUBLANES = 8
MXU_DIM = 256
VMEM_LIMIT = 56 * 1024 * 1024

GRID_W = 64
NA_HEADS = 32
NA_KH = 8
NA_KW = 16
LRU_BLOCKS = 4
LRU_CONV = 4
LRU_C = 8.0
HY_EMB = 33
HY_SHORT = 3
HY_PACK = LANES // 2
HY_DECAY_TARGET = 1e-2
HY_DECAY_PCT_HI = 0.3
HY_DECAY_PCT_LO = 1.5
N_MIXERS = 3

TOKEN_TILE = 512
FFN_TILE = 1024
HALO = SUBLANES


def _params(*sem):
    return pltpu.CompilerParams(dimension_semantics=sem, vmem_limit_bytes=VMEM_LIMIT)


def _resident(shape):
    nd = len(shape)
    return pl.BlockSpec(shape, lambda *_: (0,) * nd, pipeline_mode=pl.Buffered(1))


def _resident_layer(stacked, layer):
    nd = stacked.ndim - 1
    return pl.BlockSpec((None,) + stacked.shape[1:], lambda *_: (layer,) + (0,) * nd, pipeline_mode=pl.Buffered(1))


def _rms(x, g):
    ms = jnp.mean(x * x, axis=-1, keepdims=True)
    return x * lax.rsqrt(ms + RMS_EPS) * g


def _dot(a, b):
    return jnp.dot(a, b, preferred_element_type=F32)


def _sigmoid(x):
    return 0.5 * jnp.tanh(0.5 * x) + 0.5


def _row(v):
    return v.reshape(1, -1).astype(F32)


def _ffn_body(*refs, pre, ple, final, n_chunks):
    h_ref, g_ref, wg_ref, wu_ref, wd_ref = refs[:5]
    o_ref = refs[-1]
    h = h_ref[...]
    k = 5
    if pre == 'gated':
        y_ref, v_ref, x0_ref, sk_ref, w_ref, b_ref = refs[k:k + 6]
        k += 6
        y = (y_ref[...] + v_ref[...] * sk_ref[...]) * x0_ref[...]
        h = h + _dot(y.astype(BF16), w_ref[...]) + b_ref[...]
    elif pre == 'plain':
        x_ref, w_ref, b_ref = refs[k:k + 3]
        k += 3
        h = h + _dot(x_ref[...], w_ref[...]) + b_ref[...]
    xn = _rms(h, g_ref[...]).astype(BF16)
    n_tiles = pl.cdiv(wg_ref.shape[1], MXU_DIM)
    bounds = [min(-(-n_tiles * c // n_chunks) * MXU_DIM, wg_ref.shape[1]) for c in range(n_chunks + 1)]
    acc = jnp.zeros_like(h)
    for c0, c1 in zip(bounds[:-1], bounds[1:]):
        gt = _dot(xn, wg_ref[:, c0:c1])
        up = _dot(xn, wu_ref[:, c0:c1])
        act = (gt * jax.nn.sigmoid(gt) * up).astype(BF16)
        acc = acc + _dot(act, wd_ref[c0:c1, :])
    h = h + 0.5 * acc
    if ple:
        gp_ref, wpg_ref, p_ref, wpp_ref = refs[k:k + 4]
        k += 4
        gate = jax.nn.sigmoid(_dot(_rms(h, gp_ref[...]).astype(BF16), wpg_ref[...]))
        h = h + gate * _dot(p_ref[...].astype(BF16), wpp_ref[...])
    if final:
        h = _rms(h, refs[k][...])
    o_ref[...] = h


def _ffn(h, layer, g, wg, wu, wd, pre=None, ple=None, final=None):
    t, d = h.shape
    tm = FFN_TILE if pre is None else TOKEN_TILE
    tile = lambda width: pl.BlockSpec((tm, width), lambda i: (i, 0))
    args = [h, _row(g[layer]), wg, wu, wd]
    specs = [tile(d), _resident((1, d))] + [_resident_layer(w, layer) for w in (wg, wu, wd)]
    kind = None
    if pre is not None:
        kind, per_token, shared = pre
        args += list(per_token) + [op if op.ndim == 2 else _row(op) for op in shared]
        specs += [tile(op.shape[1]) for op in per_token] + [_resident(op.shape) for op in args[-len(shared):]]
    if ple is not None:
        gp, wpg, p, wpp = ple
        args += [_row(gp[layer]), wpg, p, wpp]
        specs += [_resident((1, d)), _resident_layer(wpg, layer),
                  pl.BlockSpec((None, tm, p.shape[2]), lambda i: (layer, i, 0)), _resident_layer(wpp, layer)]
    if final is not None:
        args.append(_row(final))
        specs.append(_resident((1, d)))
    return pl.pallas_call(
        functools.partial(_ffn_body, pre=kind, ple=ple is not None, final=final is not None, n_chunks=2),
        grid=(t // tm,), in_specs=specs, out_specs=tile(d),
        out_shape=jax.ShapeDtypeStruct((t, d), F32),
        compiler_params=_params("parallel"), name="ffn")(*args)


def _halo_specs(tm, d, n_rows):
    per = tm // HALO
    last = n_rows // HALO - 1
    return [pl.BlockSpec((HALO, d), lambda i: (jnp.maximum(i * per - 1, 0), 0)),
            pl.BlockSpec((tm, d), lambda i: (i, 0)),
            pl.BlockSpec((HALO, d), lambda i: (jnp.minimum((i + 1) * per, last), 0))]


def _normed_with_halo(hp_ref, hc_ref, hn_ref, g_ref):
    x = jnp.concatenate([hp_ref[...], hc_ref[...], hn_ref[...]], axis=0)
    return _rms(x, g_ref[...]).astype(BF16)


def _store_zero_padded(u_s, u, tiles_per_seq, tm):
    i = pl.program_id(0) % tiles_per_seq
    u_s[HALO:tm + HALO, :] = u[HALO:tm + HALO]
    u_s[0:HALO, :] = jnp.where(i == 0, 0.0, u[0:HALO])
    u_s[tm + HALO:tm + 2 * HALO, :] = jnp.where(i == tiles_per_seq - 1, 0.0, u[tm + HALO:tm + 2 * HALO])


def _lru_in_body(hp_ref, hc_ref, hn_ref, g_ref, w_ref, b_ref, cw_ref, cb_ref, gb_ref, xb_ref, u_s,
                 *, tiles_per_seq, tm):
    r_w = gb_ref.shape[1]
    xn = _normed_with_halo(hp_ref, hc_ref, hn_ref, g_ref)
    gb_ref[...] = jax.nn.gelu(_dot(xn[HALO:HALO + tm], w_ref[:, :r_w]) + b_ref[:, :r_w]).astype(BF16)
    u = _dot(xn, w_ref[:, r_w:]) + b_ref[:, r_w:]
    _store_zero_padded(u_s, u, tiles_per_seq, tm)
    left = LRU_CONV // 2
    y = cb_ref[...]
    for j in range(LRU_CONV):
        y = y + cw_ref[j:j + 1, :] * u_s[pl.ds(HALO + j - left, tm), :]
    xb_ref[...] = y


def _lru_in(h, g, w, b, cw, cb, seq_len, tm=TOKEN_TILE):
    t, d = h.shape
    r_w = cw.shape[1]
    tile = pl.BlockSpec((tm, r_w), lambda i: (i, 0))
    return pl.pallas_call(
        functools.partial(_lru_in_body, tiles_per_seq=seq_len // tm, tm=tm),
        grid=(t // tm,),
        in_specs=_halo_specs(tm, d, t) + [_resident((1, d)), _resident(w.shape), _resident((1, 2 * r_w)),
                                          _resident(cw.shape), _resident((1, r_w))],
        out_specs=[tile, tile],
        out_shape=[jax.ShapeDtypeStruct((t, r_w), BF16), jax.ShapeDtypeStruct((t, r_w), F32)],
        scratch_shapes=[pltpu.VMEM((tm + 2 * HALO, r_w), F32)],
        compiler_params=_params("parallel"), name="lru_in")(h, h, h, _row(g), w, _row(b), cw, _row(cb))


def _block_diag(x, w_ref):
    nb, bw, _ = w_ref.shape
    return jnp.concatenate([_dot(x[:, n * bw:(n + 1) * bw], w_ref[n]) for n in range(nb)], axis=1)


def _lru_scan_tile(a, b, a_s, b_s, h_s, carry_ref, reverse):
    tm, width = a.shape
    n_groups = tm // SUBLANES
    a = a.reshape(n_groups, SUBLANES, width)
    b = b.reshape(n_groups, SUBLANES, width)
    row = lax.broadcasted_iota(jnp.int32, (1, SUBLANES, 1), 1)
    for dist in (1, 2, 4):
        if reverse:
            shift, ok = SUBLANES - dist, row < SUBLANES - dist
        else:
            shift, ok = dist, row >= dist
        a_prev = jnp.where(ok, pltpu.roll(a, shift, 1), 1.0)
        b_prev = jnp.where(ok, pltpu.roll(b, shift, 1), 0.0)
        b = a * b_prev + b
        a = a * a_prev
    a_s[...] = a.reshape(tm, width)
    b_s[...] = b.reshape(tm, width)
    edge = 0 if reverse else SUBLANES - 1

    def group(k, carry):
        gi = n_groups - 1 - k if reverse else k
        r0 = pl.multiple_of(gi * SUBLANES, SUBLANES)
        hg = a_s[pl.ds(r0, SUBLANES), :] * carry + b_s[pl.ds(r0, SUBLANES), :]
        h_s[pl.ds(r0, SUBLANES), :] = hg
        return hg[edge:edge + 1, :]

    carry_ref[...] = lax.fori_loop(0, n_groups, group, carry_ref[...], unroll=True)


def _lru_gates(xb, wa_ref, ba_ref, wx_ref, bx_ref, c8_ref, first_row):
    xb16 = xb.astype(BF16)
    gate_x = _sigmoid(_block_diag(xb16, wx_ref) + bx_ref[...])
    gate_a = _sigmoid(_block_diag(xb16, wa_ref) + ba_ref[...])
    log_a = gate_a * c8_ref[...]
    a = jnp.exp(log_a)
    y = 1.0 - a * a
    mult = jnp.where(y > 0.0, y * lax.rsqrt(y), 0.0)
    r = lax.broadcasted_iota(jnp.int32, (xb.shape[0], 1), 0)
    mult = jnp.where(r == first_row, 1.0, mult)
    return a, mult * gate_x * xb


def _lru_fwd_body(xb_ref, wa_ref, ba_ref, wx_ref, bx_ref, c8_ref, h0_ref, a_s, b_s, carry):
    ti = pl.program_id(1)

    @pl.when(ti == 0)
    def _():
        carry[...] = jnp.zeros_like(carry)

    first_row = jnp.where(ti == 0, 0, -1)
    a, b = _lru_gates(xb_ref[...], wa_ref, ba_ref, wx_ref, bx_ref, c8_ref, first_row)
    _lru_scan_tile(a, b, a_s, b_s, h0_ref, carry, reverse=False)


def _lru_bwd_body(xb_ref, wa_ref, ba_ref, wx_ref, bx_ref, c8_ref, h0_ref, gb_ref, h_ref, wo_ref, bo_ref,
                  o_ref, a_s, b_s, h1_s, carry, *, tm):
    ti = pl.program_id(1)

    @pl.when(ti == 0)
    def _():
        carry[...] = jnp.zeros_like(carry)

    first_row = jnp.where(ti == 0, tm - 1, -1)
    a, b = _lru_gates(xb_ref[...], wa_ref, ba_ref, wx_ref, bx_ref, c8_ref, first_row)
    _lru_scan_tile(a, b, a_s, b_s, h1_s, carry, reverse=True)
    y = h0_ref[...] + h1_s[...]
    z = (gb_ref[...].astype(F32) * y).astype(BF16)
    o_ref[...] = h_ref[...] + _dot(z, wo_ref[...]) + bo_ref[...]


def _lru_mixer(h, g, prm, seq_len, tm=TOKEN_TILE):
    t, d = h.shape
    bsz = t // seq_len
    nt = seq_len // tm
    gb, xb = _lru_in(h, g, prm['in_w'], prm['in_b'], prm['conv_w'], prm['conv_b'], seq_len, tm)
    r_w = xb.shape[1]
    bw = r_w // LRU_BLOCKS
    gate_specs = lambda dr: [_resident((LRU_BLOCKS, bw, bw)), _resident((1, r_w)),
                             _resident((LRU_BLOCKS, bw, bw)), _resident((1, r_w)), _resident((1, r_w))]
    gate_args = lambda dr: [prm['wa'][dr], _row(prm['ba'][dr]), prm['wx'][dr], _row(prm['bx'][dr]),
                            _row(prm['c8'][dr])]
    fwd = lambda b, i: (b * nt + i, 0)
    bwd = lambda b, i: (b * nt + nt - 1 - i, 0)
    scan_scratch = [pltpu.VMEM((tm, r_w), F32), pltpu.VMEM((tm, r_w), F32)]
    h0 = pl.pallas_call(
        _lru_fwd_body, grid=(bsz, nt),
        in_specs=[pl.BlockSpec((tm, r_w), fwd)] + gate_specs(0),
        out_specs=pl.BlockSpec((tm, r_w), fwd), out_shape=jax.ShapeDtypeStruct((t, r_w), F32),
        scratch_shapes=scan_scratch + [pltpu.VMEM((1, r_w), F32)],
        compiler_params=_params("arbitrary", "arbitrary"), name="lru_fwd")(xb, *gate_args(0))
    out = pl.pallas_call(
        functools.partial(_lru_bwd_body, tm=tm), grid=(bsz, nt),
        in_specs=[pl.BlockSpec((tm, r_w), bwd)] + gate_specs(1)
        + [pl.BlockSpec((tm, r_w), bwd), pl.BlockSpec((tm, r_w), bwd), pl.BlockSpec((tm, d), bwd),
           _resident((r_w, d)), _resident((1, d))],
        out_specs=pl.BlockSpec((tm, d), bwd), out_shape=jax.ShapeDtypeStruct((t, d), F32),
        scratch_shapes=scan_scratch + [pltpu.VMEM((tm, r_w), F32), pltpu.VMEM((1, r_w), F32)],
        compiler_params=_params("arbitrary", "arbitrary"), name="lru_bwd")(
            xb, *gate_args(1), h0, gb, h, prm['out_w'], _row(prm['out_b']))
    return out, None


NA_PAIR = 2 * GRID_W
NA_WIN = 5
NA_PB = 16


def _na_qkv_body(h_ref, g_ref, wq_ref, wkt_ref, wv_ref, bq_ref, bk_ref, bv_ref, q_ref, kt_ref, v_ref, *, scale):
    xn = _rms(h_ref[...], g_ref[...]).astype(BF16)
    q_ref[...] = ((_dot(xn, wq_ref[...]) + bq_ref[...]) * scale).astype(BF16)
    kt = lax.dot_general(wkt_ref[...], xn, (((1,), (1,)), ((), ())), preferred_element_type=F32)
    kt_ref[...] = (kt + bk_ref[...]).astype(BF16)
    v_ref[...] = (_dot(xn, wv_ref[...]) + bv_ref[...]).astype(BF16)


def _na_qkv(h, g, wq, wkt, wv, bq, bk, bv, seq_len, scale, tm=TOKEN_TILE):
    t, d = h.shape
    bsz, nt = t // seq_len, seq_len // tm
    tile = pl.BlockSpec((tm, d), lambda i: (i, 0))
    return pl.pallas_call(
        functools.partial(_na_qkv_body, scale=scale), grid=(t // tm,),
        in_specs=[tile, _resident((1, d)), _resident((d, d)), _resident((d, d)), _resident((d, d)),
                  _resident((1, d)), _resident((d, 1)), _resident((1, d))],
        out_specs=[tile, pl.BlockSpec((None, d, tm), lambda i: (i // nt, 0, i % nt)), tile],
        out_shape=[jax.ShapeDtypeStruct((t, d), BF16), jax.ShapeDtypeStruct((bsz, d, seq_len), BF16),
                   jax.ShapeDtypeStruct((t, d), BF16)],
        compiler_params=_params("parallel"), name="na_qkv")(
            h, _row(g), wq, wkt, wv, _row(bq), bk.reshape(d, 1).astype(F32), _row(bv))


def _na_attn_body(q_ref, kp_ref, kc_ref, kn_ref, vp_ref, vc_ref, vn_ref, sb_ref, o_ref, kwin, vwin,
                  *, n_pairs, rows):
    pb = pl.program_id(2)
    blk = NA_PB * NA_PAIR
    for bi, (k_ref, v_ref) in enumerate(((kp_ref, vp_ref), (kc_ref, vc_ref), (kn_ref, vn_ref))):
        for t in range(NA_PB):
            kwin[bi * NA_PB + t] = k_ref[:, t * NA_PAIR:(t + 1) * NA_PAIR]
        vwin[bi * blk:(bi + 1) * blk, :] = v_ref[...]
    n_hg = sb_ref.shape[0]
    dh = LANES // n_hg
    pad = 2 * SUBLANES
    lane = lax.broadcasted_iota(jnp.int32, (1, LANES), 1)
    in_head = [(lane // dh) == j for j in range(n_hg)]
    q_row_onehot = (lax.broadcasted_iota(jnp.int32, (NA_PAIR, pad), 0) // GRID_W
                    == lax.broadcasted_iota(jnp.int32, (NA_PAIR, pad), 1)).astype(BF16)
    mask_row = lax.broadcasted_iota(jnp.int32, (pad, 1), 0)
    k_row_in_win = lax.broadcasted_iota(jnp.int32, (1, NA_WIN * NA_PAIR), 1) // GRID_W

    def pair(i, carry):
        p = pb * NA_PB + i
        kp0 = jnp.clip(p - 2, 0, n_pairs - NA_WIN)
        off = kp0 - (pb - 1) * NA_PB
        m0 = kp0 - p + (NA_WIN - 1)
        qi = q_ref[pl.ds(pl.multiple_of(i * NA_PAIR, NA_PAIR), NA_PAIR), :]
        ksl = jnp.concatenate([kwin[off + jp] for jp in range(NA_WIN)], axis=1)
        vsl = vwin[pl.ds(pl.multiple_of(off * NA_PAIR, NA_PAIR), NA_WIN * NA_PAIR), :]
        r_start = jnp.clip(2 * p + mask_row - NA_KH // 2, 0, rows - NA_KH)
        k_row = 2 * kp0 + k_row_in_win
        in_window = (k_row >= r_start) & (k_row < r_start + NA_KH)
        row_mask = jnp.where(in_window | (mask_row >= 2), 0.0, NEG_INF).astype(BF16)
        q_heads = jnp.concatenate(
            [jnp.concatenate([jnp.where(m, qi, jnp.zeros_like(qi)), q_row_onehot], axis=1) for m in in_head], axis=0)
        bias = jnp.concatenate(
            [jnp.concatenate([sb_ref[j, m0 + jp] for jp in range(NA_WIN)], axis=1) for j in range(n_hg)], axis=0)
        s = _dot(q_heads, jnp.concatenate([ksl, row_mask], axis=0)) + bias
        e = jnp.exp(s - jnp.max(s, axis=-1, keepdims=True))
        denom = jnp.sum(e, axis=-1, keepdims=True)
        pv = _dot(e.astype(BF16), vsl) / denom
        acc = jnp.zeros((NA_PAIR, LANES), F32)
        for j, m in enumerate(in_head):
            acc = acc + jnp.where(m, pv[j * NA_PAIR:(j + 1) * NA_PAIR], 0.0)
        o_ref[pl.ds(pl.multiple_of(i * NA_PAIR, NA_PAIR), NA_PAIR), :] = acc.astype(BF16)
        return carry

    lax.fori_loop(0, NA_PB, pair, 0, unroll=True)


def _na_bias_table(rpb, head_dim):
    n_h, n_dr, n_dc = rpb.shape
    n_hg = LANES // head_dim
    col = np.arange(GRID_W)
    q_start = np.clip(col - NA_KW // 2, 0, GRID_W - NA_KW)
    col_ok = (col[None, :] >= q_start[:, None]) & (col[None, :] < q_start[:, None] + NA_KW)
    dc_idx = col[None, :] - col[:, None] + NA_KW - 1
    onehot = ((np.arange(n_dc)[:, None, None] == dc_idx[None]) & col_ok[None]).reshape(n_dc, -1)
    tab = jnp.dot(rpb.astype(F32).reshape(n_h * n_dr, n_dc), jnp.asarray(onehot, F32),
                  precision=lax.Precision.HIGHEST)
    tab = tab.reshape(n_h, n_dr, GRID_W, GRID_W) + jnp.where(col_ok, 0.0, NEG_INF).astype(F32)
    masked = jnp.full((n_h, GRID_W, GRID_W), NEG_INF, F32)

    def quadrant(dr):
        return tab[:, dr + NA_KH - 1] if abs(dr) <= NA_KH - 1 else masked

    tiles = [jnp.concatenate([jnp.concatenate([quadrant(2 * (m - (NA_WIN - 1)) + c - a) for c in range(2)], axis=2)
                              for a in range(2)], axis=1) for m in range(2 * (NA_WIN - 1) + 1)]
    sb = jnp.stack(tiles, axis=1)
    return sb.reshape(n_h // n_hg, n_hg, *sb.shape[1:])


def _na_mixer(h, g, prm, seq_len):
    t, d = h.shape
    bsz = t // seq_len
    rows = seq_len // GRID_W
    n_pairs = rows // 2
    nblk = n_pairs // NA_PB
    assert seq_len == 2 * nblk * NA_PB * GRID_W and rows >= NA_KH and LANES % (d // NA_HEADS) == 0
    blk = NA_PB * NA_PAIR
    q, kt, v = _na_qkv(h, g, prm['wq'], prm['wkt'], prm['wv'], prm['bq'], prm['bk'], prm['bv'], seq_len,
                       (d // NA_HEADS) ** -0.5)
    q = q.reshape(bsz, seq_len, d)
    v = v.reshape(bsz, seq_len, d)
    prev = lambda pb: jnp.maximum(pb - 1, 0)
    nxt = lambda pb: jnp.minimum(pb + 1, nblk - 1)
    k_spec = lambda f: pl.BlockSpec((None, LANES, blk), lambda gi, b, pb: (b, gi, f(pb)))
    v_spec = lambda f: pl.BlockSpec((None, blk, LANES), lambda gi, b, pb: (b, f(pb), gi))
    same = lambda pb: pb
    sb = prm['bias']
    o = pl.pallas_call(
        functools.partial(_na_attn_body, n_pairs=n_pairs, rows=rows),
        grid=(d // LANES, bsz, nblk),
        in_specs=[v_spec(same), k_spec(prev), k_spec(same), k_spec(nxt), v_spec(prev), v_spec(same), v_spec(nxt),
                  pl.BlockSpec((None,) + sb.shape[1:], lambda gi, b, pb: (gi, 0, 0, 0, 0))],
        out_specs=v_spec(same), out_shape=jax.ShapeDtypeStruct((bsz, seq_len, d), BF16),
        scratch_shapes=[pltpu.VMEM((3 * NA_PB, LANES, NA_PAIR), BF16), pltpu.VMEM((3 * blk, LANES), BF16)],
        compiler_params=_params("arbitrary", "arbitrary", "arbitrary"), name="na_attn")(
            q, kt, kt, kt, v, v, v, sb)
    return h, ('plain', [o.reshape(t, d)], [prm['out_w'], prm['out_b']])


def _hy_in_body(hp_ref, hc_ref, hn_ref, g_ref, w_ref, b_ref, cw_ref, cb_ref, x0_ref, vv_ref, u_s,
                *, tiles_per_seq, tm):
    d = x0_ref.shape[1]
    xn = _normed_with_halo(hp_ref, hc_ref, hn_ref, g_ref)
    left = (HY_SHORT - 1) // 2

    def conv(c0):
        u = _dot(xn, w_ref[:, c0:c0 + d]) + b_ref[:, c0:c0 + d]
        _store_zero_padded(u_s, u, tiles_per_seq, tm)
        y = cb_ref[:, c0:c0 + d]
        for j in range(HY_SHORT):
            y = y + cw_ref[j:j + 1, c0:c0 + d] * u_s[pl.ds(HALO + j - left, tm), :]
        return y

    x0_ref[...] = conv(0).astype(x0_ref.dtype)
    x1 = conv(d)
    vv_ref[...] = conv(2 * d) * x1


def _hy_in(h, g, w, b, cw, cb, seq_len, tm=TOKEN_TILE):
    t, d = h.shape
    tile = pl.BlockSpec((tm, d), lambda i: (i, 0))
    return pl.pallas_call(
        functools.partial(_hy_in_body, tiles_per_seq=seq_len // tm, tm=tm), grid=(t // tm,),
        in_specs=_halo_specs(tm, d, t) + [_resident((1, d)), _resident(w.shape), _resident((1, 3 * d)),
                                          _resident(cw.shape), _resident((1, 3 * d))],
        out_specs=[tile, tile],
        out_shape=[jax.ShapeDtypeStruct((t, d), BF16), jax.ShapeDtypeStruct((t, d), F32)],
        scratch_shapes=[pltpu.VMEM((tm + 2 * HALO, d), F32)],
        compiler_params=_params("parallel"), name="hy_in")(h, h, h, _row(g), w, _row(b), cw, _row(cb))


def _hy_filter_body(z_ref, w1_ref, b1_ref, w2_ref, b2_ref, w3_ref, b3_ref, fr_ref, wo_ref, dl_ref,
                    k_ref):
    hp = lax.Precision.HIGHEST
    half, d = z_ref.shape[0], k_ref.shape[2]
    z = z_ref[...]
    fr = fr_ref[...]
    a = jnp.sin(fr * (jnp.dot(z, w1_ref[...], precision=hp, preferred_element_type=F32) + b1_ref[...]))
    a = jnp.sin(fr * (jnp.dot(a, w2_ref[...], precision=hp, preferred_element_type=F32) + b2_ref[...]))
    a = jnp.sin(fr * (jnp.dot(a, w3_ref[...], precision=hp, preferred_element_type=F32) + b3_ref[...]))
    a = a.astype(BF16)
    r = lax.broadcasted_iota(jnp.int32, (half, 1), 0)
    for part in range(2):
        k = _dot(a, wo_ref[part])
        window = jnp.exp(-z[:, part * HY_PACK:part * HY_PACK + 1] * dl_ref[...])
        rows = slice(part * half, (part + 1) * half)
        k_ref[0, rows, :] = k[:, :d] * window
        kb = k[:, d:] * window
        if part == 0:
            kb = jnp.where((r == 0) & (pl.program_id(0) == 0), 0.0, kb)
        k_ref[1, rows, :] = kb


def _block_diag2(x):
    zero = jnp.zeros_like(x)
    return jnp.concatenate([jnp.concatenate([x, zero], axis=-1), jnp.concatenate([zero, x], axis=-1)], axis=-2)


def _pad_to(x, shape):
    return jnp.pad(x.astype(F32), [(0, s - n) for s, n in zip(shape, x.shape)])


@functools.lru_cache(maxsize=None)
def _hy_positions(length, d, tl):
    bands = (HY_EMB - 1) // 2
    t = np.linspace(0.0, 1.0, length)[:, None]
    ang = (2.0 * math.pi / length) * np.arange(length)[:, None]
    fb = np.linspace(1e-4, bands - 1, bands)[None, :]
    z = np.concatenate([t, np.cos(fb * ang), -np.sin(fb * ang)], axis=-1)
    z = np.pad(z, ((0, 0), (0, HY_PACK - z.shape[1]))).reshape(length // tl, 2, tl // 2, HY_PACK)
    z = z.transpose(0, 2, 1, 3).reshape(length // 2, 2 * HY_PACK).astype(np.float32)
    decay_min = math.log(HY_DECAY_TARGET) / HY_DECAY_PCT_LO
    decay_max = math.log(HY_DECAY_TARGET) / HY_DECAY_PCT_HI
    deltas = np.abs(np.linspace(decay_min, decay_max, d))[None, :].astype(np.float32)
    return z, deltas


def _hy_filters(length, prm, d, tl=TOKEN_TILE):
    assert max(prm['f_w1'].shape + prm['f_w2'].shape) <= HY_PACK and length % tl == 0
    z, deltas = _hy_positions(length, d, tl)
    sq = (HY_PACK, HY_PACK)
    twice = lambda v: jnp.tile(_pad_to(v[None], (1, HY_PACK)), (1, 2))
    w_out = _pad_to(prm['f_out'], (HY_PACK, 2 * d))
    w_out = jnp.stack([jnp.concatenate([w_out, jnp.zeros_like(w_out)]),
                       jnp.concatenate([jnp.zeros_like(w_out), w_out])]).astype(BF16)
    args = [z, _block_diag2(_pad_to(prm['f_w1'], sq)), twice(prm['f_b1']),
            _block_diag2(_pad_to(prm['f_w2'], sq)), twice(prm['f_b2']),
            _block_diag2(_pad_to(prm['f_w3'], sq)), twice(prm['f_b3']), twice(prm['f_freq']), w_out, deltas]
    return pl.pallas_call(
        _hy_filter_body, grid=(length // tl,),
        in_specs=[pl.BlockSpec((tl // 2, 2 * HY_PACK), lambda i: (i, 0))] + [_resident(a.shape) for a in args[1:]],
        out_specs=pl.BlockSpec((2, tl, d), lambda i: (0, i, 0)),
        out_shape=jax.ShapeDtypeStruct((2, length, d), F32),
        compiler_params=_params("parallel"), name="hy_filter")(*args)


def _dft_plan(length):
    n = 2 * length
    n1 = min(128, n // 16)
    return n, n1, n // n1, n1 // 2 + 1


@functools.lru_cache(maxsize=None)
def _dft_tables(length):
    n, n1, n2, n1e = _dft_plan(length)
    n1h, sub = n1 // 2, SUBLANES
    groups = n2 // sub
    k1 = np.arange(n1e)
    t_idx = (n2 * np.arange(n1h)[None, None, None, :] + sub * np.arange(groups)[:, None, None, None]
             + np.arange(sub)[None, None, :, None])
    th = 2.0 * np.pi * ((k1[None, :, None, None] * t_idx) % n) / n

    def expand(m):
        return np.einsum('gksi,st->gksit', m, np.eye(sub)).reshape(groups, n1e * sub, n1h * sub)

    fwd1 = np.concatenate([expand(np.cos(th)), expand(-np.sin(th))], axis=1)
    weight = np.where((k1 == 0) | (k1 == n1 // 2), 1.0, 2.0)[None, :, None, None] / n
    inv1 = np.concatenate([expand(weight * np.cos(th)), expand(-weight * np.sin(th))], axis=1).transpose(0, 2, 1)
    inv1 = np.pad(inv1, ((0, 0), (0, 0), (0, -inv1.shape[2] % LANES)))
    idx = np.arange(n2, dtype=np.int64)
    th2 = 2.0 * np.pi * ((idx[:, None] * idx[None, :]) % n2) / n2
    c2, s2 = np.cos(th2), np.sin(th2)
    fwd2 = np.block([[c2, s2], [-s2, c2]])
    inv2 = np.block([[c2, -s2], [s2, c2]])
    return {name: tab.astype(np.float32).astype(BF16)
            for name, tab in (('fwd1', fwd1), ('inv1', inv1), ('fwd2', fwd2), ('inv2', inv2))}


DFT_ROWS = 2 * SUBLANES


def _dft1_body(x_ref, t_ref, o_ref):
    n1h, _, d = x_ref.shape
    x = x_ref[...]
    halves = [_dot(t_ref[g], x[:, g * SUBLANES:(g + 1) * SUBLANES, :].reshape(n1h * SUBLANES, d).astype(BF16))
              .reshape(o_ref.shape[:2] + (SUBLANES, d)) for g in range(DFT_ROWS // SUBLANES)]
    o_ref[...] = jnp.concatenate(halves, axis=2).astype(o_ref.dtype)


def _dft1(x, length, tabs):
    bx, _, d = x.shape
    n, n1, n2, n1e = _dft_plan(length)
    tab = tabs['fwd1']
    a = pl.pallas_call(
        _dft1_body, grid=(n2 // DFT_ROWS, bx),
        in_specs=[pl.BlockSpec((None, n1 // 2, None, DFT_ROWS, d), lambda j, b: (b, 0, j, 0, 0)),
                  pl.BlockSpec((DFT_ROWS // SUBLANES,) + tab.shape[1:], lambda j, b: (j, 0, 0))],
        out_specs=pl.BlockSpec((None, 2, n1e, None, DFT_ROWS, d), lambda j, b: (b, 0, 0, j, 0, 0)),
        out_shape=jax.ShapeDtypeStruct((bx, 2, n1e, n2 // DFT_ROWS, DFT_ROWS, d), BF16),
        compiler_params=_params("parallel", "arbitrary"), name="hy_dft1")(
            x.reshape(bx, n1 // 2, n2 // DFT_ROWS, DFT_ROWS, d), tab)
    return a.reshape(bx, 2, n1e, n2, d)


def _filter_spectrum_body(a_ref, f_ref, o_ref):
    n2 = o_ref.shape[1]
    d = o_ref.shape[2]
    xf = _dot(f_ref[...], a_ref[0].reshape(2 * n2, d))
    xb = _dot(f_ref[...], a_ref[1].reshape(2 * n2, d))
    o_ref[0] = (xf[:n2] + xb[:n2]).astype(o_ref.dtype)
    o_ref[1] = (xf[n2:] - xb[n2:]).astype(o_ref.dtype)


def _filter_spectrum(k, length, tabs):
    d = k.shape[2]
    n, n1, n2, n1e = _dft_plan(length)
    a = _dft1(k, length, tabs)
    return pl.pallas_call(
        _filter_spectrum_body, grid=(n1e,),
        in_specs=[pl.BlockSpec((2, 2, None, n2, d), lambda k: (0, 0, k, 0, 0)), _resident(tabs['fwd2'].shape)],
        out_specs=pl.BlockSpec((2, None, n2, d), lambda k: (0, k, 0, 0)),
        out_shape=jax.ShapeDtypeStruct((2, n1e, n2, d), BF16),
        compiler_params=_params("parallel"), name="hy_filter_spectrum")(a, tabs['fwd2'])


def _spectral_body(a_ref, k_ref, f_ref, i_ref, o_ref):
    bb, _, n2, d = o_ref.shape
    a = jnp.concatenate([a_ref[b].reshape(2 * n2, d) for b in range(bb)], axis=1)
    x = _dot(f_ref[...], a)
    xr, xi = x[:n2], x[n2:]
    kr = jnp.concatenate([k_ref[0]] * bb, axis=1)
    ki = jnp.concatenate([k_ref[1]] * bb, axis=1)
    y = jnp.concatenate([xr * kr - xi * ki, xr * ki + xi * kr], axis=0).astype(BF16)
    z = _dot(i_ref[...], y)
    for b in range(bb):
        o_ref[b] = z[:, b * d:(b + 1) * d].reshape(2, n2, d).astype(o_ref.dtype)


def _spectral_multiply(a, kc, length, tabs):
    bx = a.shape[0]
    _, n1e, n2, d = kc.shape
    bb = max(b for b in range(1, bx + 1) if bx % b == 0 and b * n2 <= 2 * MXU_DIM)
    blk = pl.BlockSpec((bb, 2, None, n2, d), lambda k, b: (b, 0, k, 0, 0))
    return pl.pallas_call(
        _spectral_body, grid=(n1e, bx // bb),
        in_specs=[blk, pl.BlockSpec((2, None, n2, d), lambda k, b: (0, k, 0, 0)),
                  _resident(tabs['fwd2'].shape), _resident(tabs['inv2'].shape)],
        out_specs=blk, out_shape=jax.ShapeDtypeStruct((bx, 2, n1e, n2, d), BF16),
        compiler_params=_params("parallel", "arbitrary"), name="hy_spectral")(
            a, kc, tabs['fwd2'], tabs['inv2'])


def _idft1_body(z_ref, t_ref, o_ref):
    _, n1e, _, d = z_ref.shape
    rows = 2 * n1e * SUBLANES
    z = z_ref[...].astype(F32)
    halves = []
    for g in range(DFT_ROWS // SUBLANES):
        zg = z[:, :, g * SUBLANES:(g + 1) * SUBLANES, :].reshape(rows, d).astype(BF16)
        zg = jnp.concatenate([zg, jnp.zeros((t_ref.shape[2] - rows, d), BF16)], axis=0)
        halves.append(_dot(t_ref[g], zg).reshape(o_ref.shape[0], SUBLANES, d))
    o_ref[...] = jnp.concatenate(halves, axis=1).astype(o_ref.dtype)


def _idft1(z, length, tabs):
    bx, _, n1e, n2, d = z.shape
    n, n1, _, _ = _dft_plan(length)
    tab = tabs['inv1']
    y = pl.pallas_call(
        _idft1_body, grid=(n2 // DFT_ROWS, bx),
        in_specs=[pl.BlockSpec((None, 2, n1e, None, DFT_ROWS, d), lambda j, b: (b, 0, 0, j, 0, 0)),
                  pl.BlockSpec((DFT_ROWS // SUBLANES,) + tab.shape[1:], lambda j, b: (j, 0, 0))],
        out_specs=pl.BlockSpec((None, n1 // 2, None, DFT_ROWS, d), lambda j, b: (b, 0, j, 0, 0)),
        out_shape=jax.ShapeDtypeStruct((bx, n1 // 2, n2 // DFT_ROWS, DFT_ROWS, d), BF16),
        compiler_params=_params("parallel", "arbitrary"), name="hy_idft1")(
            z.reshape(bx, 2, n1e, n2 // DFT_ROWS, DFT_ROWS, d), tab)
    return y.reshape(bx, length, d)


def _hyena_mixer(h, g, prm, seq_len):
    t, d = h.shape
    bsz = t // seq_len
    tabs = _dft_tables(seq_len)
    x0, vv = _hy_in(h, g, prm['in_w'], prm['in_b'], prm['conv_w'], prm['conv_b'], seq_len)
    kc = _filter_spectrum(_hy_filters(seq_len, prm, d), seq_len, tabs)
    a = _dft1(vv.reshape(bsz, seq_len, d), seq_len, tabs)
    z = _spectral_multiply(a, kc, seq_len, tabs)
    yc = _idft1(z, seq_len, tabs).reshape(t, d)
    return h, ('gated', [yc, vv, x0], [prm['skip'], prm['out_w'], prm['out_b']])


def _trunk(x, p, w, depth):
    bsz, seq_len, d = x.shape
    assert seq_len % FFN_TILE == 0 and d % LANES == 0
    h = x.reshape(bsz * seq_len, d)
    p = p.reshape(depth, bsz * seq_len, -1)
    for i in range(depth):
        kind, j = i % N_MIXERS, i // N_MIXERS
        h = _ffn(h, i, w['ln_ffn1'], w['ffn1_wg'], w['ffn1_wu'], w['ffn1_wd'])
        mix = (_hyena_mixer, _na_mixer, _lru_mixer)[kind]
        h, pending = mix(h, w['ln_mix'][i], w[('hy', 'na', 'lru')[kind]][j], seq_len)
        h = _ffn(h, i, w['ln_ffn2'], w['ffn2_wg'], w['ffn2_wu'], w['ffn2_wd'], pre=pending,
                 ple=(w['ln_ple'], w['ple_gate'], p, w['ple_proj']),
                 final=w['ln_final'] if i == depth - 1 else None)
    return h.reshape(bsz, seq_len, d)


def kernel(x_prompt, x_sample, p_prompt, p_sample, ln_ffn1, ffn1_wg, ffn1_wu, ffn1_wd, ln_mix, ln_ffn2, ffn2_wg, ffn2_wu, ffn2_wd, ln_ple, ple_gate, ple_proj, ln_final, hy_in_w, hy_in_b, hy_conv_w, hy_conv_b, hy_f_w1, hy_f_b1, hy_f_w2, hy_f_b2, hy_f_w3, hy_f_b3, hy_f_freq, hy_f_out, hy_skip, hy_out_w, hy_out_b, na_qkv_w, na_qkv_b, na_rpb, na_out_w, na_out_b, lru_in_w, lru_in_b, lru_conv_w, lru_conv_b, lru_wa, lru_ba, lru_wx, lru_bx, lru_lambda, lru_out_w, lru_out_b):
    depth = ln_ffn1.shape[0]
    d = x_prompt.shape[-1]
    b16 = lambda a: a.astype(BF16)
    w = {
        'ln_ffn1': ln_ffn1, 'ffn1_wg': b16(ffn1_wg), 'ffn1_wu': b16(ffn1_wu), 'ffn1_wd': b16(ffn1_wd),
        'ln_mix': ln_mix, 'ln_ffn2': ln_ffn2, 'ffn2_wg': b16(ffn2_wg), 'ffn2_wu': b16(ffn2_wu),
        'ffn2_wd': b16(ffn2_wd), 'ln_ple': ln_ple, 'ple_gate': b16(ple_gate), 'ple_proj': b16(ple_proj),
        'ln_final': ln_final,
        'hy': [dict(in_w=b16(hy_in_w[j]), in_b=hy_in_b[j], conv_w=hy_conv_w[j], conv_b=hy_conv_b[j],
                    f_w1=hy_f_w1[j], f_b1=hy_f_b1[j], f_w2=hy_f_w2[j], f_b2=hy_f_b2[j], f_w3=hy_f_w3[j],
                    f_b3=hy_f_b3[j], f_freq=hy_f_freq[j], f_out=hy_f_out[j], skip=hy_skip[j],
                    out_w=b16(hy_out_w[j]), out_b=hy_out_b[j]) for j in range(hy_in_w.shape[0])],
        'na': [dict(wq=b16(na_qkv_w[j][:, :d]), wkt=b16(na_qkv_w[j][:, d:2 * d].T), wv=b16(na_qkv_w[j][:, 2 * d:]),
                    bq=na_qkv_b[j][:d], bk=na_qkv_b[j][d:2 * d], bv=na_qkv_b[j][2 * d:],
                    bias=_na_bias_table(na_rpb[j], d // NA_HEADS), out_w=b16(na_out_w[j]), out_b=na_out_b[j])
               for j in range(na_qkv_w.shape[0])],
        'lru': [dict(in_w=b16(lru_in_w[j]), in_b=lru_in_b[j], conv_w=lru_conv_w[j], conv_b=lru_conv_b[j],
                     wa=b16(lru_wa[j]), ba=lru_ba[j], wx=b16(lru_wx[j]), bx=lru_bx[j],
                     c8=-LRU_C * jax.nn.softplus(-lru_lambda[j].astype(F32)),
                     out_w=b16(lru_out_w[j]), out_b=lru_out_b[j]) for j in range(lru_in_w.shape[0])],
    }
    return _trunk(x_prompt, p_prompt, w, depth), _trunk(x_sample, p_sample, w, depth)
```

```python
import functools
import math

import numpy as np
import jax
import jax.numpy as jnp
from jax import lax
from jax.experimental import pallas as pl
from jax.experimental.pallas import tpu as pltpu

F32 = jnp.float32
BF16 = jnp.bfloat16

RMS_EPS = 1e-6
NEG_INF = -1e30
LANES = 128
SUBLANES = 8
MXU_DIM = 256
VMEM_LIMIT = 56 * 1024 * 1024

GRID_W = 64
NA_HEADS = 32
NA_KH = 8
NA_KW = 16
LRU_BLOCKS = 4
LRU_CONV = 4
LRU_C = 8.0
HY_EMB = 33
HY_SHORT = 3
HY_PACK = LANES // 2
HY_DECAY_TARGET = 1e-2
HY_DECAY_PCT_HI = 0.3
HY_DECAY_PCT_LO = 1.5
N_MIXERS = 3

TOKEN_TILE = 512
FFN_TILE = 1024
HALO = SUBLANES


def _params(*sem):
    return pltpu.CompilerParams(dimension_semantics=sem, vmem_limit_bytes=VMEM_LIMIT)


def _resident(shape):
    nd = len(shape)
    return pl.BlockSpec(shape, lambda *_: (0,) * nd, pipeline_mode=pl.Buffered(1))


def _resident_layer(stacked, layer):
    nd = stacked.ndim - 1
    return pl.BlockSpec((None,) + stacked.shape[1:], lambda *_: (layer,) + (0,) * nd, pipeline_mode=pl.Buffered(1))


def _rms(x, g):
    ms = jnp.mean(x * x, axis=-1, keepdims=True)
    return x * lax.rsqrt(ms + RMS_EPS) * g


def _dot(a, b):
    return jnp.dot(a, b, preferred_element_type=F32)


def _sigmoid(x):
    return 0.5 * jnp.tanh(0.5 * x) + 0.5


def _row(v):
    return v.reshape(1, -1).astype(F32)


def _ffn_body(*refs, pre, ple, final, n_chunks):
    h_ref, g_ref, wg_ref, wu_ref, wd_ref = refs[:5]
    o_ref = refs[-1]
    h = h_ref[...]
    k = 5
    if pre == 'gated':
        y_ref, v_ref, x0_ref, sk_ref, w_ref, b_ref = refs[k:k + 6]
        k += 6
        y = (y_ref[...] + v_ref[...] * sk_ref[...]) * x0_ref[...]
        h = h + _dot(y.astype(BF16), w_ref[...]) + b_ref[...]
    elif pre == 'plain':
        x_ref, w_ref, b_ref = refs[k:k + 3]
        k += 3
        h = h + _dot(x_ref[...], w_ref[...]) + b_ref[...]
    xn = _rms(h, g_ref[...]).astype(BF16)
    n_tiles = pl.cdiv(wg_ref.shape[1], MXU_DIM)
    bounds = [min(-(-n_tiles * c // n_chunks) * MXU_DIM, wg_ref.shape[1]) for c in range(n_chunks + 1)]
    acc = jnp.zeros_like(h)
    for c0, c1 in zip(bounds[:-1], bounds[1:]):
        gt = _dot(xn, wg_ref[:, c0:c1])
        up = _dot(xn, wu_ref[:, c0:c1])
        act = (gt * jax.nn.sigmoid(gt) * up).astype(BF16)
        acc = acc + _dot(act, wd_ref[c0:c1, :])
    h = h + 0.5 * acc
    if ple:
        gp_ref, wpg_ref, p_ref, wpp_ref = refs[k:k + 4]
        k += 4
        gate = jax.nn.sigmoid(_dot(_rms(h, gp_ref[...]).astype(BF16), wpg_ref[...]))
        h = h + gate * _dot(p_ref[...].astype(BF16), wpp_ref[...])
    if final:
        h = _rms(h, refs[k][...])
    o_ref[...] = h


def _ffn(h, layer, g, wg, wu, wd, pre=None, ple=None, final=None):
    t, d = h.shape
    tm = FFN_TILE if pre is None else TOKEN_TILE
    tile = lambda width: pl.BlockSpec((tm, width), lambda i: (i, 0))
    args = [h, _row(g[layer]), wg, wu, wd]
    specs = [tile(d), _resident((1, d))] + [_resident_layer(w, layer) for w in (wg, wu, wd)]
    kind = None
    if pre is not None:
        kind, per_token, shared = pre
        args += list(per_token) + [op if op.ndim == 2 else _row(op) for op in shared]
        specs += [tile(op.shape[1]) for op in per_token] + [_resident(op.shape) for op in args[-len(shared):]]
    if ple is not None:
        gp, wpg, p, wpp = ple
        args += [_row(gp[layer]), wpg, p, wpp]
        specs += [_resident((1, d)), _resident_layer(wpg, layer),
                  pl.BlockSpec((None, tm, p.shape[2]), lambda i: (layer, i, 0)), _resident_layer(wpp, layer)]
    if final is not None:
        args.append(_row(final))
        specs.append(_resident((1, d)))
    return pl.pallas_call(
        functools.partial(_ffn_body, pre=kind, ple=ple is not None, final=final is not None, n_chunks=2),
        grid=(t // tm,), in_specs=specs, out_specs=tile(d),
        out_shape=jax.ShapeDtypeStruct((t, d), F32),
        compiler_params=_params("parallel"), name="ffn")(*args)


def _halo_specs(tm, d, n_rows):
    per = tm // HALO
    last = n_rows // HALO - 1
    return [pl.BlockSpec((HALO, d), lambda i: (jnp.maximum(i * per - 1, 0), 0)),
            pl.BlockSpec((tm, d), lambda i: (i, 0)),
            pl.BlockSpec((HALO, d), lambda i: (jnp.minimum((i + 1) * per, last), 0))]


def _normed_with_halo(hp_ref, hc_ref, hn_ref, g_ref):
    x = jnp.concatenate([hp_ref[...], hc_ref[...], hn_ref[...]], axis=0)
    return _rms(x, g_ref[...]).astype(BF16)


def _store_zero_padded(u_s, u, tiles_per_seq, tm):
    i = pl.program_id(0) % tiles_per_seq
    u_s[HALO:tm + HALO, :] = u[HALO:tm + HALO]
    u_s[0:HALO, :] = jnp.where(i == 0, 0.0, u[0:HALO])
    u_s[tm + HALO:tm + 2 * HALO, :] = jnp.where(i == tiles_per_seq - 1, 0.0, u[tm + HALO:tm + 2 * HALO])


def _lru_in_body(hp_ref, hc_ref, hn_ref, g_ref, w_ref, b_ref, cw_ref, cb_ref, gb_ref, xb_ref, u_s,
                 *, tiles_per_seq, tm):
    r_w = gb_ref.shape[1]
    xn = _normed_with_halo(hp_ref, hc_ref, hn_ref, g_ref)
    gb_ref[...] = jax.nn.gelu(_dot(xn[HALO:HALO + tm], w_ref[:, :r_w]) + b_ref[:, :r_w]).astype(BF16)
    u = _dot(xn, w_ref[:, r_w:]) + b_ref[:, r_w:]
    _store_zero_padded(u_s, u, tiles_per_seq, tm)
    left = LRU_CONV // 2
    y = cb_ref[...]
    for j in range(LRU_CONV):
        y = y + cw_ref[j:j + 1, :] * u_s[pl.ds(HALO + j - left, tm), :]
    xb_ref[...] = y


def _lru_in(h, g, w, b, cw, cb, seq_len, tm=TOKEN_TILE):
    t, d = h.shape
    r_w = cw.shape[1]
    tile = pl.BlockSpec((tm, r_w), lambda i: (i, 0))
    return pl.pallas_call(
        functools.partial(_lru_in_body, tiles_per_seq=seq_len // tm, tm=tm),
        grid=(t // tm,),
        in_specs=_halo_specs(tm, d, t) + [_resident((1, d)), _resident(w.shape), _resident((1, 2 * r_w)),
                                          _resident(cw.shape), _resident((1, r_w))],
        out_specs=[tile, tile],
        out_shape=[jax.ShapeDtypeStruct((t, r_w), BF16), jax.ShapeDtypeStruct((t, r_w), F32)],
        scratch_shapes=[pltpu.VMEM((tm + 2 * HALO, r_w), F32)],
        compiler_params=_params("parallel"), name="lru_in")(h, h, h, _row(g), w, _row(b), cw, _row(cb))


def _block_diag(x, w_ref):
    nb, bw, _ = w_ref.shape
    return jnp.concatenate([_dot(x[:, n * bw:(n + 1) * bw], w_ref[n]) for n in range(nb)], axis=1)


def _lru_scan_tile(a, b, a_s, b_s, h_s, carry_ref, reverse):
    tm, width = a.shape
    n_groups = tm // SUBLANES
    a = a.reshape(n_groups, SUBLANES, width)
    b = b.reshape(n_groups, SUBLANES, width)
    row = lax.broadcasted_iota(jnp.int32, (1, SUBLANES, 1), 1)
    for dist in (1, 2, 4):
        if reverse:
            shift, ok = SUBLANES - dist, row < SUBLANES - dist
        else:
            shift, ok = dist, row >= dist
        a_prev = jnp.where(ok, pltpu.roll(a, shift, 1), 1.0)
        b_prev = jnp.where(ok, pltpu.roll(b, shift, 1), 0.0)
        b = a * b_prev + b
        a = a * a_prev
    a_s[...] = a.reshape(tm, width)
    b_s[...] = b.reshape(tm, width)
    edge = 0 if reverse else SUBLANES - 1

    def group(k, carry):
        gi = n_groups - 1 - k if reverse else k
        r0 = pl.multiple_of(gi * SUBLANES, SUBLANES)
        hg = a_s[pl.ds(r0, SUBLANES), :] * carry + b_s[pl.ds(r0, SUBLANES), :]
        h_s[pl.ds(r0, SUBLANES), :] = hg
        return hg[edge:edge + 1, :]

    carry_ref[...] = lax.fori_loop(0, n_groups, group, carry_ref[...], unroll=True)


def _lru_gates(xb, wa_ref, ba_ref, wx_ref, bx_ref, c8_ref, first_row):
    xb16 = xb.astype(BF16)
    gate_x = _sigmoid(_block_diag(xb16, wx_ref) + bx_ref[...])
    gate_a = _sigmoid(_block_diag(xb16, wa_ref) + ba_ref[...])
    log_a = gate_a * c8_ref[...]
    a = jnp.exp(log_a)
    y = 1.0 - a * a
    mult = jnp.where(y > 0.0, y * lax.rsqrt(y), 0.0)
    r = lax.broadcasted_iota(jnp.int32, (xb.shape[0], 1), 0)
    mult = jnp.where(r == first_row, 1.0, mult)
    return a, mult * gate_x * xb


def _lru_fwd_body(xb_ref, wa_ref, ba_ref, wx_ref, bx_ref, c8_ref, h0_ref, a_s, b_s, carry):
    ti = pl.program_id(1)

    @pl.when(ti == 0)
    def _():
        carry[...] = jnp.zeros_like(carry)

    first_row = jnp.where(ti == 0, 0, -1)
    a, b = _lru_gates(xb_ref[...], wa_ref, ba_ref, wx_ref, bx_ref, c8_ref, first_row)
    _lru_scan_tile(a, b, a_s, b_s, h0_ref, carry, reverse=False)


def _lru_bwd_body(xb_ref, wa_ref, ba_ref, wx_ref, bx_ref, c8_ref, h0_ref, gb_ref, h_ref, wo_ref, bo_ref,
                  o_ref, a_s, b_s, h1_s, carry, *, tm):
    ti = pl.program_id(1)

    @pl.when(ti == 0)
    def _():
        carry[...] = jnp.zeros_like(carry)

    first_row = jnp.where(ti == 0, tm - 1, -1)
    a, b = _lru_gates(xb_ref[...], wa_ref, ba_ref, wx_ref, bx_ref, c8_ref, first_row)
    _lru_scan_tile(a, b, a_s, b_s, h1_s, carry, reverse=True)
    y = h0_ref[...] + h1_s[...]
    z = (gb_ref[...].astype(F32) * y).astype(BF16)
    o_ref[...] = h_ref[...] + _dot(z, wo_ref[...]) + bo_ref[...]


def _lru_mixer(h, g, prm, seq_len, tm=TOKEN_TILE):
    t, d = h.shape
    bsz = t // seq_len
    nt = seq_len // tm
    gb, xb = _lru_in(h, g, prm['in_w'], prm['in_b'], prm['conv_w'], prm['conv_b'], seq_len, tm)
    r_w = xb.shape[1]
    bw = r_w // LRU_BLOCKS
    gate_specs = lambda dr: [_resident((LRU_BLOCKS, bw, bw)), _resident((1, r_w)),
                             _resident((LRU_BLOCKS, bw, bw)), _resident((1, r_w)), _resident((1, r_w))]
    gate_args = lambda dr: [prm['wa'][dr], _row(prm['ba'][dr]), prm['wx'][dr], _row(prm['bx'][dr]),
                            _row(prm['c8'][dr])]
    fwd = lambda b, i: (b * nt + i, 0)
    bwd = lambda b, i: (b * nt + nt - 1 - i, 0)
    scan_scratch = [pltpu.VMEM((tm, r_w), F32), pltpu.VMEM((tm, r_w), F32)]
    h0 = pl.pallas_call(
        _lru_fwd_body, grid=(bsz, nt),
        in_specs=[pl.BlockSpec((tm, r_w), fwd)] + gate_specs(0),
        out_specs=pl.BlockSpec((tm, r_w), fwd), out_shape=jax.ShapeDtypeStruct((t, r_w), F32),
        scratch_shapes=scan_scratch + [pltpu.VMEM((1, r_w), F32)],
        compiler_params=_params("arbitrary", "arbitrary"), name="lru_fwd")(xb, *gate_args(0))
    out = pl.pallas_call(
        functools.partial(_lru_bwd_body, tm=tm), grid=(bsz, nt),
        in_specs=[pl.BlockSpec((tm, r_w), bwd)] + gate_specs(1)
        + [pl.BlockSpec((tm, r_w), bwd), pl.BlockSpec((tm, r_w), bwd), pl.BlockSpec((tm, d), bwd),
           _resident((r_w, d)), _resident((1, d))],
        out_specs=pl.BlockSpec((tm, d), bwd), out_shape=jax.ShapeDtypeStruct((t, d), F32),
        scratch_shapes=scan_scratch + [pltpu.VMEM((tm, r_w), F32), pltpu.VMEM((1, r_w), F32)],
        compiler_params=_params("arbitrary", "arbitrary"), name="lru_bwd")(
            xb, *gate_args(1), h0, gb, h, prm['out_w'], _row(prm['out_b']))
    return out, None


NA_PAIR = 2 * GRID_W
NA_WIN = 5
NA_PB = 16


def _na_qkv_body(h_ref, g_ref, wq_ref, wkt_ref, wv_ref, bq_ref, bk_ref, bv_ref, q_ref, kt_ref, v_ref, *, scale):
    xn = _rms(h_ref[...], g_ref[...]).astype(BF16)
    q_ref[...] = ((_dot(xn, wq_ref[...]) + bq_ref[...]) * scale).astype(BF16)
    kt = lax.dot_general(wkt_ref[...], xn, (((1,), (1,)), ((), ())), preferred_element_type=F32)
    kt_ref[...] = (kt + bk_ref[...]).astype(BF16)
    v_ref[...] = (_dot(xn, wv_ref[...]) + bv_ref[...]).astype(BF16)


def _na_qkv(h, g, wq, wkt, wv, bq, bk, bv, seq_len, scale, tm=TOKEN_TILE):
    t, d = h.shape
    bsz, nt = t // seq_len, seq_len // tm
    tile = pl.BlockSpec((tm, d), lambda i: (i, 0))
    return pl.pallas_call(
        functools.partial(_na_qkv_body, scale=scale), grid=(t // tm,),
        in_specs=[tile, _resident((1, d)), _resident((d, d)), _resident((d, d)), _resident((d, d)),
                  _resident((1, d)), _resident((d, 1)), _resident((1, d))],
        out_specs=[tile, pl.BlockSpec((None, d, tm), lambda i: (i // nt, 0, i % nt)), tile],
        out_shape=[jax.ShapeDtypeStruct((t, d), BF16), jax.ShapeDtypeStruct((bsz, d, seq_len), BF16),
                   jax.ShapeDtypeStruct((t, d), BF16)],
        compiler_params=_params("parallel"), name="na_qkv")(
            h, _row(g), wq, wkt, wv, _row(bq), bk.reshape(d, 1).astype(F32), _row(bv))


def _na_attn_body(q_ref, kp_ref, kc_ref, kn_ref, vp_ref, vc_ref, vn_ref, sb_ref, o_ref, kwin, vwin,
                  *, n_pairs, rows):
    pb = pl.program_id(2)
    blk = NA_PB * NA_PAIR
    for bi, (k_ref, v_ref) in enumerate(((kp_ref, vp_ref), (kc_ref, vc_ref), (kn_ref, vn_ref))):
        for t in range(NA_PB):
            kwin[bi * NA_PB + t] = k_ref[:, t * NA_PAIR:(t + 1) * NA_PAIR]
        vwin[bi * blk:(bi + 1) * blk, :] = v_ref[...]
    n_hg = sb_ref.shape[0]
    dh = LANES // n_hg
    pad = 2 * SUBLANES
    lane = lax.broadcasted_iota(jnp.int32, (1, LANES), 1)
    in_head = [(lane // dh) == j for j in range(n_hg)]
    q_row_onehot = (lax.broadcasted_iota(jnp.int32, (NA_PAIR, pad), 0) // GRID_W
                    == lax.broadcasted_iota(jnp.int32, (NA_PAIR, pad), 1)).astype(BF16)
    mask_row = lax.broadcasted_iota(jnp.int32, (pad, 1), 0)
    k_row_in_win = lax.broadcasted_iota(jnp.int32, (1, NA_WIN * NA_PAIR), 1) // GRID_W

    def pair(i, carry):
        p = pb * NA_PB + i
        kp0 = jnp.clip(p - 2, 0, n_pairs - NA_WIN)
        off = kp0 - (pb - 1) * NA_PB
        m0 = kp0 - p + (NA_WIN - 1)
        qi = q_ref[pl.ds(pl.multiple_of(i * NA_PAIR, NA_PAIR), NA_PAIR), :]
        ksl = jnp.concatenate([kwin[off + jp] for jp in range(NA_WIN)], axis=1)
        vsl = vwin[pl.ds(pl.multiple_of(off * NA_PAIR, NA_PAIR), NA_WIN * NA_PAIR), :]
        r_start = jnp.clip(2 * p + mask_row - NA_KH // 2, 0, rows - NA_KH)
        k_row = 2 * kp0 + k_row_in_win
        in_window = (k_row >= r_start) & (k_row < r_start + NA_KH)
        row_mask = jnp.where(in_window | (mask_row >= 2), 0.0, NEG_INF).astype(BF16)
        q_heads = jnp.concatenate(
            [jnp.concatenate([jnp.where(m, qi, jnp.zeros_like(qi)), q_row_onehot], axis=1) for m in in_head], axis=0)
        bias = jnp.concatenate(
            [jnp.concatenate([sb_ref[j, m0 + jp] for jp in range(NA_WIN)], axis=1) for j in range(n_hg)], axis=0)
        s = _dot(q_heads, jnp.concatenate([ksl, row_mask], axis=0)) + bias
        e = jnp.exp((s - jnp.max(s, axis=-1, keepdims=True)).astype(BF16))
        denom = jnp.sum(e, axis=-1, keepdims=True, dtype=F32)
        pv = _dot(e, vsl) / denom
        acc = jnp.zeros((NA_PAIR, LANES), F32)
        for j, m in enumerate(in_head):
            acc = acc + jnp.where(m, pv[j * NA_PAIR:(j + 1) * NA_PAIR], 0.0)
        o_ref[pl.ds(pl.multiple_of(i * NA_PAIR, NA_PAIR), NA_PAIR), :] = acc.astype(BF16)
        return carry

    lax.fori_loop(0, NA_PB, pair, 0, unroll=True)


def _na_bias_table(rpb, head_dim):
    n_h, n_dr, n_dc = rpb.shape
    n_hg = LANES // head_dim
    col = np.arange(GRID_W)
    q_start = np.clip(col - NA_KW // 2, 0, GRID_W - NA_KW)
    col_ok = (col[None, :] >= q_start[:, None]) & (col[None, :] < q_start[:, None] + NA_KW)
    dc_idx = col[None, :] - col[:, None] + NA_KW - 1
    onehot = ((np.arange(n_dc)[:, None, None] == dc_idx[None]) & col_ok[None]).reshape(n_dc, -1)
    tab = jnp.dot(rpb.astype(F32).reshape(n_h * n_dr, n_dc), jnp.asarray(onehot, F32),
                  precision=lax.Precision.HIGHEST)
    tab = tab.reshape(n_h, n_dr, GRID_W, GRID_W) + jnp.where(col_ok, 0.0, NEG_INF).astype(F32)
    masked = jnp.full((n_h, GRID_W, GRID_W), NEG_INF, F32)

    def quadrant(dr):
        return tab[:, dr + NA_KH - 1] if abs(dr) <= NA_KH - 1 else masked

    tiles = [jnp.concatenate([jnp.concatenate([quadrant(2 * (m - (NA_WIN - 1)) + c - a) for c in range(2)], axis=2)
                              for a in range(2)], axis=1) for m in range(2 * (NA_WIN - 1) + 1)]
    sb = jnp.stack(tiles, axis=1)
    return sb.reshape(n_h // n_hg, n_hg, *sb.shape[1:])


def _na_mixer(h, g, prm, seq_len):
    t, d = h.shape
    bsz = t // seq_len
    rows = seq_len // GRID_W
    n_pairs = rows // 2
    nblk = n_pairs // NA_PB
    assert seq_len == 2 * nblk * NA_PB * GRID_W and rows >= NA_KH and LANES % (d // NA_HEADS) == 0
    blk = NA_PB * NA_PAIR
    q, kt, v = _na_qkv(h, g, prm['wq'], prm['wkt'], prm['wv'], prm['bq'], prm['bk'], prm['bv'], seq_len,
                       (d // NA_HEADS) ** -0.5)
    q = q.reshape(bsz, seq_len, d)
    v = v.reshape(bsz, seq_len, d)
    prev = lambda pb: jnp.maximum(pb - 1, 0)
    nxt = lambda pb: jnp.minimum(pb + 1, nblk - 1)
    k_spec = lambda f: pl.BlockSpec((None, LANES, blk), lambda gi, b, pb: (b, gi, f(pb)))
    v_spec = lambda f: pl.BlockSpec((None, blk, LANES), lambda gi, b, pb: (b, f(pb), gi))
    same = lambda pb: pb
    sb = prm['bias']
    o = pl.pallas_call(
        functools.partial(_na_attn_body, n_pairs=n_pairs, rows=rows),
        grid=(d // LANES, bsz, nblk),
        in_specs=[v_spec(same), k_spec(prev), k_spec(same), k_spec(nxt), v_spec(prev), v_spec(same), v_spec(nxt),
                  pl.BlockSpec((None,) + sb.shape[1:], lambda gi, b, pb: (gi, 0, 0, 0, 0))],
        out_specs=v_spec(same), out_shape=jax.ShapeDtypeStruct((bsz, seq_len, d), BF16),
        scratch_shapes=[pltpu.VMEM((3 * NA_PB, LANES, NA_PAIR), BF16), pltpu.VMEM((3 * blk, LANES), BF16)],
        compiler_params=_params("arbitrary", "arbitrary", "arbitrary"), name="na_attn")(
            q, kt, kt, kt, v, v, v, sb)
    return h, ('plain', [o.reshape(t, d)], [prm['out_w'], prm['out_b']])


def _hy_in_body(hp_ref, hc_ref, hn_ref, g_ref, w_ref, b_ref, cw_ref, cb_ref, x0_ref, vv_ref, u_s,
                *, tiles_per_seq, tm):
    d = x0_ref.shape[1]
    xn = _normed_with_halo(hp_ref, hc_ref, hn_ref, g_ref)
    left = (HY_SHORT - 1) // 2

    def conv(c0):
        u = _dot(xn, w_ref[:, c0:c0 + d]) + b_ref[:, c0:c0 + d]
        _store_zero_padded(u_s, u, tiles_per_seq, tm)
        y = cb_ref[:, c0:c0 + d]
        for j in range(HY_SHORT):
            y = y + cw_ref[j:j + 1, c0:c0 + d] * u_s[pl.ds(HALO + j - left, tm), :]
        return y

    x0_ref[...] = conv(0).astype(x0_ref.dtype)
    x1 = conv(d)
    vv_ref[...] = conv(2 * d) * x1


def _hy_in(h, g, w, b, cw, cb, seq_len, tm=TOKEN_TILE):
    t, d = h.shape
    tile = pl.BlockSpec((tm, d), lambda i: (i, 0))
    return pl.pallas_call(
        functools.partial(_hy_in_body, tiles_per_seq=seq_len // tm, tm=tm), grid=(t // tm,),
        in_specs=_halo_specs(tm, d, t) + [_resident((1, d)), _resident(w.shape), _resident((1, 3 * d)),
                                          _resident(cw.shape), _resident((1, 3 * d))],
        out_specs=[tile, tile],
        out_shape=[jax.ShapeDtypeStruct((t, d), BF16), jax.ShapeDtypeStruct((t, d), F32)],
        scratch_shapes=[pltpu.VMEM((tm + 2 * HALO, d), F32)],
        compiler_params=_params("parallel"), name="hy_in")(h, h, h, _row(g), w, _row(b), cw, _row(cb))


def _hy_filter_body(z_ref, w1_ref, b1_ref, w2_ref, b2_ref, w3_ref, b3_ref, fr_ref, wo_ref, dl_ref,
                    k_ref):
    hp = lax.Precision.HIGHEST
    half, d = z_ref.shape[0], k_ref.shape[2]
    z = z_ref[...]
    fr = fr_ref[...]
    a = jnp.sin(fr * (jnp.dot(z, w1_ref[...], precision=hp, preferred_element_type=F32) + b1_ref[...]))
    a = jnp.sin(fr * (jnp.dot(a, w2_ref[...], precision=hp, preferred_element_type=F32) + b2_ref[...]))
    a = jnp.sin(fr * (jnp.dot(a, w3_ref[...], precision=hp, preferred_element_type=F32) + b3_ref[...]))
    a = a.astype(BF16)
    r = lax.broadcasted_iota(jnp.int32, (half, 1), 0)
    for part in range(2):
        k = _dot(a, wo_ref[part])
        window = jnp.exp(-z[:, part * HY_PACK:part * HY_PACK + 1] * dl_ref[...])
        rows = slice(part * half, (part + 1) * half)
        k_ref[0, rows, :] = k[:, :d] * window
        kb = k[:, d:] * window
        if part == 0:
            kb = jnp.where((r == 0) & (pl.program_id(0) == 0), 0.0, kb)
        k_ref[1, rows, :] = kb


def _block_diag2(x):
    zero = jnp.zeros_like(x)
    return jnp.concatenate([jnp.concatenate([x, zero], axis=-1), jnp.concatenate([zero, x], axis=-1)], axis=-2)


def _pad_to(x, shape):
    return jnp.pad(x.astype(F32), [(0, s - n) for s, n in zip(shape, x.shape)])


@functools.lru_cache(maxsize=None)
def _hy_positions(length, d, tl):
    bands = (HY_EMB - 1) // 2
    t = np.linspace(0.0, 1.0, length)[:, None]
    ang = (2.0 * math.pi / length) * np.arange(length)[:, None]
    fb = np.linspace(1e-4, bands - 1, bands)[None, :]
    z = np.concatenate([t, np.cos(fb * ang), -np.sin(fb * ang)], axis=-1)
    z = np.pad(z, ((0, 0), (0, HY_PACK - z.shape[1]))).reshape(length // tl, 2, tl // 2, HY_PACK)
    z = z.transpose(0, 2, 1, 3).reshape(length // 2, 2 * HY_PACK).astype(np.float32)
    decay_min = math.log(HY_DECAY_TARGET) / HY_DECAY_PCT_LO
    decay_max = math.log(HY_DECAY_TARGET) / HY_DECAY_PCT_HI
    deltas = np.abs(np.linspace(decay_min, decay_max, d))[None, :].astype(np.float32)
    return z, deltas


def _hy_filters(length, prm, d, tl=TOKEN_TILE):
    assert max(prm['f_w1'].shape + prm['f_w2'].shape) <= HY_PACK and length % tl == 0
    z, deltas = _hy_positions(length, d, tl)
    sq = (HY_PACK, HY_PACK)
    twice = lambda v: jnp.tile(_pad_to(v[None], (1, HY_PACK)), (1, 2))
    w_out = _pad_to(prm['f_out'], (HY_PACK, 2 * d))
    w_out = jnp.stack([jnp.concatenate([w_out, jnp.zeros_like(w_out)]),
                       jnp.concatenate([jnp.zeros_like(w_out), w_out])]).astype(BF16)
    args = [z, _block_diag2(_pad_to(prm['f_w1'], sq)), twice(prm['f_b1']),
            _block_diag2(_pad_to(prm['f_w2'], sq)), twice(prm['f_b2']),
            _block_diag2(_pad_to(prm['f_w3'], sq)), twice(prm['f_b3']), twice(prm['f_freq']), w_out, deltas]
    return pl.pallas_call(
        _hy_filter_body, grid=(length // tl,),
        in_specs=[pl.BlockSpec((tl // 2, 2 * HY_PACK), lambda i: (i, 0))] + [_resident(a.shape) for a in args[1:]],
        out_specs=pl.BlockSpec((2, tl, d), lambda i: (0, i, 0)),
        out_shape=jax.ShapeDtypeStruct((2, length, d), F32),
        compiler_params=_params("parallel"), name="hy_filter")(*args)


def _dft_plan(length):
    n = 2 * length
    n1 = min(128, n // 16)
    return n, n1, n // n1, n1 // 2 + 1


@functools.lru_cache(maxsize=None)
def _dft_tables(length):
    n, n1, n2, n1e = _dft_plan(length)
    n1h, sub = n1 // 2, SUBLANES
    groups = n2 // sub
    k1 = np.arange(n1e)
    t_idx = (n2 * np.arange(n1h)[None, None, None, :] + sub * np.arange(groups)[:, None, None, None]
             + np.arange(sub)[None, None, :, None])
    th = 2.0 * np.pi * ((k1[None, :, None, None] * t_idx) % n) / n

    def expand(m):
        return np.einsum('gksi,st->gksit', m, np.eye(sub)).reshape(groups, n1e * sub, n1h * sub)

    fwd1 = np.concatenate([expand(np.cos(th)), expand(-np.sin(th))], axis=1)
    weight = np.where((k1 == 0) | (k1 == n1 // 2), 1.0, 2.0)[None, :, None, None] / n
    inv1 = np.concatenate([expand(weight * np.cos(th)), expand(-weight * np.sin(th))], axis=1).transpose(0, 2, 1)
    inv1 = np.pad(inv1, ((0, 0), (0, 0), (0, -inv1.shape[2] % LANES)))
    idx = np.arange(n2, dtype=np.int64)
    th2 = 2.0 * np.pi * ((idx[:, None] * idx[None, :]) % n2) / n2
    c2, s2 = np.cos(th2), np.sin(th2)
    fwd2 = np.block([[c2, s2], [-s2, c2]])
    inv2 = np.block([[c2, -s2], [s2, c2]])
    return {name: tab.astype(np.float32).astype(BF16)
            for name, tab in (('fwd1', fwd1), ('inv1', inv1), ('fwd2', fwd2), ('inv2', inv2))}


DFT_ROWS = 2 * SUBLANES


def _dft1_body(x_ref, t_ref, o_ref):
    n1h, _, d = x_ref.shape
    x = x_ref[...]
    halves = [_dot(t_ref[g], x[:, g * SUBLANES:(g + 1) * SUBLANES, :].reshape(n1h * SUBLANES, d).astype(BF16))
              .reshape(o_ref.shape[:2] + (SUBLANES, d)) for g in range(DFT_ROWS // SUBLANES)]
    o_ref[...] = jnp.concatenate(halves, axis=2).astype(o_ref.dtype)


def _dft1(x, length, tabs):
    bx, _, d = x.shape
    n, n1, n2, n1e = _dft_plan(length)
    tab = tabs['fwd1']
    a = pl.pallas_call(
        _dft1_body, grid=(n2 // DFT_ROWS, bx),
        in_specs=[pl.BlockSpec((None, n1 // 2, None, DFT_ROWS, d), lambda j, b: (b, 0, j, 0, 0)),
                  pl.BlockSpec((DFT_ROWS // SUBLANES,) + tab.shape[1:], lambda j, b: (j, 0, 0))],
        out_specs=pl.BlockSpec((None, 2, n1e, None, DFT_ROWS, d), lambda j, b: (b, 0, 0, j, 0, 0)),
        out_shape=jax.ShapeDtypeStruct((bx, 2, n1e, n2 // DFT_ROWS, DFT_ROWS, d), BF16),
        compiler_params=_params("parallel", "arbitrary"), name="hy_dft1")(
            x.reshape(bx, n1 // 2, n2 // DFT_ROWS, DFT_ROWS, d), tab)
    return a.reshape(bx, 2, n1e, n2, d)


def _filter_spectrum_body(a_ref, f_ref, o_ref):
    n2 = o_ref.shape[1]
    d = o_ref.shape[2]
    xf = _dot(f_ref[...], a_ref[0].reshape(2 * n2, d))
    xb = _dot(f_ref[...], a_ref[1].reshape(2 * n2, d))
    o_ref[0] = (xf[:n2] + xb[:n2]).astype(o_ref.dtype)
    o_ref[1] = (xf[n2:] - xb[n2:]).astype(o_ref.dtype)


def _filter_spectrum(k, length, tabs):
    d = k.shape[2]
    n, n1, n2, n1e = _dft_plan(length)
    a = _dft1(k, length, tabs)
    return pl.pallas_call(
        _filter_spectrum_body, grid=(n1e,),
        in_specs=[pl.BlockSpec((2, 2, None, n2, d), lambda k: (0, 0, k, 0, 0)), _resident(tabs['fwd2'].shape)],
        out_specs=pl.BlockSpec((2, None, n2, d), lambda k: (0, k, 0, 0)),
        out_shape=jax.ShapeDtypeStruct((2, n1e, n2, d), BF16),
        compiler_params=_params("parallel"), name="hy_filter_spectrum")(a, tabs['fwd2'])


def _spectral_body(a_ref, k_ref, f_ref, i_ref, o_ref):
    bb, _, n2, d = o_ref.shape
    a = jnp.concatenate([a_ref[b].reshape(2 * n2, d) for b in range(bb)], axis=1)
    x = _dot(f_ref[...], a)
    xr, xi = x[:n2], x[n2:]
    kr = jnp.concatenate([k_ref[0]] * bb, axis=1)
    ki = jnp.concatenate([k_ref[1]] * bb, axis=1)
    y = jnp.concatenate([xr * kr - xi * ki, xr * ki + xi * kr], axis=0).astype(BF16)
    z = _dot(i_ref[...], y)
    for b in range(bb):
        o_ref[b] = z[:, b * d:(b + 1) * d].reshape(2, n2, d).astype(o_ref.dtype)


def _spectral_multiply(a, kc, length, tabs):
    bx = a.shape[0]
    _, n1e, n2, d = kc.shape
    bb = max(b for b in range(1, bx + 1) if bx % b == 0 and b * n2 <= 2 * MXU_DIM)
    blk = pl.BlockSpec((bb, 2, None, n2, d), lambda k, b: (b, 0, k, 0, 0))
    return pl.pallas_call(
        _spectral_body, grid=(n1e, bx // bb),
        in_specs=[blk, pl.BlockSpec((2, None, n2, d), lambda k, b: (0, k, 0, 0)),
                  _resident(tabs['fwd2'].shape), _resident(tabs['inv2'].shape)],
        out_specs=blk, out_shape=jax.ShapeDtypeStruct((bx, 2, n1e, n2, d), BF16),
        compiler_params=_params("parallel", "arbitrary"), name="hy_spectral")(
            a, kc, tabs['fwd2'], tabs['inv2'])


def _idft1_body(z_ref, t_ref, o_ref):
    _, n1e, _, d = z_ref.shape
    rows = 2 * n1e * SUBLANES
    z = z_ref[...].astype(F32)
    halves = []
    for g in range(DFT_ROWS // SUBLANES):
        zg = z[:, :, g * SUBLANES:(g + 1) * SUBLANES, :].reshape(rows, d).astype(BF16)
        zg = jnp.concatenate([zg, jnp.zeros((t_ref.shape[2] - rows, d), BF16)], axis=0)
        halves.append(_dot(t_ref[g], zg).reshape(o_ref.shape[0], SUBLANES, d))
    o_ref[...] = jnp.concatenate(halves, axis=1).astype(o_ref.dtype)


def _idft1(z, length, tabs):
    bx, _, n1e, n2, d = z.shape
    n, n1, _, _ = _dft_plan(length)
    tab = tabs['inv1']
    y = pl.pallas_call(
        _idft1_body, grid=(n2 // DFT_ROWS, bx),
        in_specs=[pl.BlockSpec((None, 2, n1e, None, DFT_ROWS, d), lambda j, b: (b, 0, 0, j, 0, 0)),
                  pl.BlockSpec((DFT_ROWS // SUBLANES,) + tab.shape[1:], lambda j, b: (j, 0, 0))],
        out_specs=pl.BlockSpec((None, n1 // 2, None, DFT_ROWS, d), lambda j, b: (b, 0, j, 0, 0)),
        out_shape=jax.ShapeDtypeStruct((bx, n1 // 2, n2 // DFT_ROWS, DFT_ROWS, d), BF16),
        compiler_params=_params("parallel", "arbitrary"), name="hy_idft1")(
            z.reshape(bx, 2, n1e, n2 // DFT_ROWS, DFT_ROWS, d), tab)
    return y.reshape(bx, length, d)


def _hyena_mixer(h, g, prm, seq_len):
    t, d = h.shape
    bsz = t // seq_len
    tabs = _dft_tables(seq_len)
    x0, vv = _hy_in(h, g, prm['in_w'], prm['in_b'], prm['conv_w'], prm['conv_b'], seq_len)
    kc = _filter_spectrum(_hy_filters(seq_len, prm, d), seq_len, tabs)
    a = _dft1(vv.reshape(bsz, seq_len, d), seq_len, tabs)
    z = _spectral_multiply(a, kc, seq_len, tabs)
    yc = _idft1(z, seq_len, tabs).reshape(t, d)
    return h, ('gated', [yc, vv, x0], [prm['skip'], prm['out_w'], prm['out_b']])


def _trunk(x, p, w, depth):
    bsz, seq_len, d = x.shape
    assert seq_len % FFN_TILE == 0 and d % LANES == 0
    h = x.reshape(bsz * seq_len, d)
    p = p.reshape(depth, bsz * seq_len, -1)
    for i in range(depth):
        kind, j = i % N_MIXERS, i // N_MIXERS
        h = _ffn(h, i, w['ln_ffn1'], w['ffn1_wg'], w['ffn1_wu'], w['ffn1_wd'])
        mix = (_hyena_mixer, _na_mixer, _lru_mixer)[kind]
        h, pending = mix(h, w['ln_mix'][i], w[('hy', 'na', 'lru')[kind]][j], seq_len)
        h = _ffn(h, i, w['ln_ffn2'], w['ffn2_wg'], w['ffn2_wu'], w['ffn2_wd'], pre=pending,
                 ple=(w['ln_ple'], w['ple_gate'], p, w['ple_proj']),
                 final=w['ln_final'] if i == depth - 1 else None)
    return h.reshape(bsz, seq_len, d)


def kernel(x_prompt, x_sample, p_prompt, p_sample, ln_ffn1, ffn1_wg, ffn1_wu, ffn1_wd, ln_mix, ln_ffn2, ffn2_wg, ffn2_wu, ffn2_wd, ln_ple, ple_gate, ple_proj, ln_final, hy_in_w, hy_in_b, hy_conv_w, hy_conv_b, hy_f_w1, hy_f_b1, hy_f_w2, hy_f_b2, hy_f_w3, hy_f_b3, hy_f_freq, hy_f_out, hy_skip, hy_out_w, hy_out_b, na_qkv_w, na_qkv_b, na_rpb, na_out_w, na_out_b, lru_in_w, lru_in_b, lru_conv_w, lru_conv_b, lru_wa, lru_ba, lru_wx, lru_bx, lru_lambda, lru_out_w, lru_out_b):
    depth = ln_ffn1.shape[0]
    d = x_prompt.shape[-1]
    b16 = lambda a: a.astype(BF16)
    w = {
        'ln_ffn1': ln_ffn1, 'ffn1_wg': b16(ffn1_wg), 'ffn1_wu': b16(ffn1_wu), 'ffn1_wd': b16(ffn1_wd),
        'ln_mix': ln_mix, 'ln_ffn2': ln_ffn2, 'ffn2_wg': b16(ffn2_wg), 'ffn2_wu': b16(ffn2_wu),
        'ffn2_wd': b16(ffn2_wd), 'ln_ple': ln_ple, 'ple_gate': b16(ple_gate), 'ple_proj': b16(ple_proj),
        'ln_final': ln_final,
        'hy': [dict(in_w=b16(hy_in_w[j]), in_b=hy_in_b[j], conv_w=hy_conv_w[j], conv_b=hy_conv_b[j],
                    f_w1=hy_f_w1[j], f_b1=hy_f_b1[j], f_w2=hy_f_w2[j], f_b2=hy_f_b2[j], f_w3=hy_f_w3[j],
                    f_b3=hy_f_b3[j], f_freq=hy_f_freq[j], f_out=hy_f_out[j], skip=hy_skip[j],
                    out_w=b16(hy_out_w[j]), out_b=hy_out_b[j]) for j in range(hy_in_w.shape[0])],
        'na': [dict(wq=b16(na_qkv_w[j][:, :d]), wkt=b16(na_qkv_w[j][:, d:2 * d].T), wv=b16(na_qkv_w[j][:, 2 * d:]),
                    bq=na_qkv_b[j][:d], bk=na_qkv_b[j][d:2 * d], bv=na_qkv_b[j][2 * d:],
                    bias=_na_bias_table(na_rpb[j], d // NA_HEADS), out_w=b16(na_out_w[j]), out_b=na_out_b[j])
               for j in range(na_qkv_w.shape[0])],
        'lru': [dict(in_w=b16(lru_in_w[j]), in_b=lru_in_b[j], conv_w=lru_conv_w[j], conv_b=lru_conv_b[j],
                     wa=b16(lru_wa[j]), ba=lru_ba[j], wx=b16(lru_wx[j]), bx=lru_bx[j],
                     c8=-LRU_C * jax.nn.softplus(-lru_lambda[j].astype(F32)),
                     out_w=b16(lru_out_w[j]), out_b=lru_out_b[j]) for j in range(lru_in_w.shape[0])],
    }
    return _trunk(x_prompt, p_prompt, w, depth), _trunk(x_sample, p_sample, w, depth)
```

```python
import functools
import math

import numpy as np
import jax
import jax.numpy as jnp
from jax import lax
from jax.experimental import pallas as pl
from jax.experimental.pallas import tpu as pltpu

F32 = jnp.float32
BF16 = jnp.bfloat16

RMS_EPS = 1e-6
NEG_INF = -1e30
LANES = 128
SUBLANES = 8
MXU_DIM = 256
VMEM_LIMIT = 56 * 1024 * 1024

GRID_W = 64
NA_HEADS = 32
NA_KH = 8
NA_KW = 16
LRU_BLOCKS = 4
LRU_CONV = 4
LRU_C = 8.0
HY_EMB = 33
HY_SHORT = 3
HY_PACK = LANES // 2
HY_DECAY_TARGET = 1e-2
HY_DECAY_PCT_HI = 0.3
HY_DECAY_PCT_LO = 1.5
N_MIXERS = 3

TOKEN_TILE = 512
FFN_TILE = 1024
HALO = SUBLANES


def _params(*sem):
    return pltpu.CompilerParams(dimension_semantics=sem, vmem_limit_bytes=VMEM_LIMIT)


def _resident(shape):
    nd = len(shape)
    return pl.BlockSpec(shape, lambda *_: (0,) * nd, pipeline_mode=pl.Buffered(1))


def _resident_layer(stacked, layer):
    nd = stacked.ndim - 1
    return pl.BlockSpec((None,) + stacked.shape[1:], lambda *_: (layer,) + (0,) * nd, pipeline_mode=pl.Buffered(1))


def _rms(x, g):
    ms = jnp.mean(x * x, axis=-1, keepdims=True)
    return x * lax.rsqrt(ms + RMS_EPS) * g


def _dot(a, b):
    return jnp.dot(a, b, preferred_element_type=F32)


def _sigmoid(x):
    return 0.5 * jnp.tanh(0.5 * x) + 0.5


def _row(v):
    return v.reshape(1, -1).astype(F32)


def _ffn_body(*refs, pre, ple, final, n_chunks):
    h_ref, g_ref, wg_ref, wu_ref, wd_ref = refs[:5]
    o_ref = refs[-1]
    h = h_ref[...]
    k = 5
    if pre == 'gated':
        y_ref, v_ref, x0_ref, sk_ref, w_ref, b_ref = refs[k:k + 6]
        k += 6
        y = (y_ref[...] + v_ref[...] * sk_ref[...]) * x0_ref[...]
        h = h + _dot(y.astype(BF16), w_ref[...]) + b_ref[...]
    elif pre == 'plain':
        x_ref, w_ref, b_ref = refs[k:k + 3]
        k += 3
        h = h + _dot(x_ref[...], w_ref[...]) + b_ref[...]
    xn = _rms(h, g_ref[...]).astype(BF16)
    n_tiles = pl.cdiv(wg_ref.shape[1], MXU_DIM)
    bounds = [min(-(-n_tiles * c // n_chunks) * MXU_DIM, wg_ref.shape[1]) for c in range(n_chunks + 1)]
    acc = jnp.zeros_like(h)
    for c0, c1 in zip(bounds[:-1], bounds[1:]):
        gt = _dot(xn, wg_ref[:, c0:c1])
        up = _dot(xn, wu_ref[:, c0:c1])
        act = (gt * jax.nn.sigmoid(gt) * up).astype(BF16)
        acc = acc + _dot(act, wd_ref[c0:c1, :])
    h = h + 0.5 * acc
    if ple:
        gp_ref, wpg_ref, p_ref, wpp_ref = refs[k:k + 4]
        k += 4
        gate = jax.nn.sigmoid(_dot(_rms(h, gp_ref[...]).astype(BF16), wpg_ref[...]))
        h = h + gate * _dot(p_ref[...].astype(BF16), wpp_ref[...])
    if final:
        h = _rms(h, refs[k][...])
    o_ref[...] = h


def _ffn(h, layer, g, wg, wu, wd, pre=None, ple=None, final=None):
    t, d = h.shape
    tm = FFN_TILE if pre is None else TOKEN_TILE
    tile = lambda width: pl.BlockSpec((tm, width), lambda i: (i, 0))
    args = [h, _row(g[layer]), wg, wu, wd]
    specs = [tile(d), _resident((1, d))] + [_resident_layer(w, layer) for w in (wg, wu, wd)]
    kind = None
    if pre is not None:
        kind, per_token, shared = pre
        args += list(per_token) + [op if op.ndim == 2 else _row(op) for op in shared]
        specs += [tile(op.shape[1]) for op in per_token] + [_resident(op.shape) for op in args[-len(shared):]]
    if ple is not None:
        gp, wpg, p, wpp = ple
        args += [_row(gp[layer]), wpg, p, wpp]
        specs += [_resident((1, d)), _resident_layer(wpg, layer),
                  pl.BlockSpec((None, tm, p.shape[2]), lambda i: (layer, i, 0)), _resident_layer(wpp, layer)]
    if final is not None:
        args.append(_row(final))
        specs.append(_resident((1, d)))
    return pl.pallas_call(
        functools.partial(_ffn_body, pre=kind, ple=ple is not None, final=final is not None, n_chunks=2),
        grid=(t // tm,), in_specs=specs, out_specs=tile(d),
        out_shape=jax.ShapeDtypeStruct((t, d), F32),
        compiler_params=_params("parallel"), name="ffn")(*args)


def _halo_specs(tm, d, n_rows):
    per = tm // HALO
    last = n_rows // HALO - 1
    return [pl.BlockSpec((HALO, d), lambda i: (jnp.maximum(i * per - 1, 0), 0)),
            pl.BlockSpec((tm, d), lambda i: (i, 0)),
            pl.BlockSpec((HALO, d), lambda i: (jnp.minimum((i + 1) * per, last), 0))]


def _normed_with_halo(hp_ref, hc_ref, hn_ref, g_ref):
    x = jnp.concatenate([hp_ref[...], hc_ref[...], hn_ref[...]], axis=0)
    return _rms(x, g_ref[...]).astype(BF16)


def _store_zero_padded(u_s, u, tiles_per_seq, tm):
    i = pl.program_id(0) % tiles_per_seq
    u_s[HALO:tm + HALO, :] = u[HALO:tm + HALO]
    u_s[0:HALO, :] = jnp.where(i == 0, 0.0, u[0:HALO])
    u_s[tm + HALO:tm + 2 * HALO, :] = jnp.where(i == tiles_per_seq - 1, 0.0, u[tm + HALO:tm + 2 * HALO])


def _lru_in_body(hp_ref, hc_ref, hn_ref, g_ref, w_ref, b_ref, cw_ref, cb_ref, gb_ref, xb_ref, u_s,
                 *, tiles_per_seq, tm):
    r_w = gb_ref.shape[1]
    xn = _normed_with_halo(hp_ref, hc_ref, hn_ref, g_ref)
    gb_ref[...] = jax.nn.gelu(_dot(xn[HALO:HALO + tm], w_ref[:, :r_w]) + b_ref[:, :r_w]).astype(BF16)
    u = _dot(xn, w_ref[:, r_w:]) + b_ref[:, r_w:]
    _store_zero_padded(u_s, u, tiles_per_seq, tm)
    left = LRU_CONV // 2
    y = cb_ref[...]
    for j in range(LRU_CONV):
        y = y + cw_ref[j:j + 1, :] * u_s[pl.ds(HALO + j - left, tm), :]
    xb_ref[...] = y


def _lru_in(h, g, w, b, cw, cb, seq_len, tm=TOKEN_TILE):
    t, d = h.shape
    r_w = cw.shape[1]
    tile = pl.BlockSpec((tm, r_w), lambda i: (i, 0))
    return pl.pallas_call(
        functools.partial(_lru_in_body, tiles_per_seq=seq_len // tm, tm=tm),
        grid=(t // tm,),
        in_specs=_halo_specs(tm, d, t) + [_resident((1, d)), _resident(w.shape), _resident((1, 2 * r_w)),
                                          _resident(cw.shape), _resident((1, r_w))],
        out_specs=[tile, tile],
        out_shape=[jax.ShapeDtypeStruct((t, r_w), BF16), jax.ShapeDtypeStruct((t, r_w), F32)],
        scratch_shapes=[pltpu.VMEM((tm + 2 * HALO, r_w), F32)],
        compiler_params=_params("parallel"), name="lru_in")(h, h, h, _row(g), w, _row(b), cw, _row(cb))


def _block_diag(x, w_ref):
    nb, bw, _ = w_ref.shape
    return jnp.concatenate([_dot(x[:, n * bw:(n + 1) * bw], w_ref[n]) for n in range(nb)], axis=1)


def _lru_scan_tile(a, b, a_s, b_s, h_s, carry_ref, reverse):
    tm, width = a.shape
    n_groups = tm // SUBLANES
    a = a.reshape(n_groups, SUBLANES, width)
    b = b.reshape(n_groups, SUBLANES, width)
    row = lax.broadcasted_iota(jnp.int32, (1, SUBLANES, 1), 1)
    for dist in (1, 2, 4):
        if reverse:
            shift, ok = SUBLANES - dist, row < SUBLANES - dist
        else:
            shift, ok = dist, row >= dist
        a_prev = jnp.where(ok, pltpu.roll(a, shift, 1), 1.0)
        b_prev = jnp.where(ok, pltpu.roll(b, shift, 1), 0.0)
        b = a * b_prev + b
        a = a * a_prev
    a_s[...] = a.reshape(tm, width)
    b_s[...] = b.reshape(tm, width)
    edge = 0 if reverse else SUBLANES - 1

    def group(k, carry):
        gi = n_groups - 1 - k if reverse else k
        r0 = pl.multiple_of(gi * SUBLANES, SUBLANES)
        hg = a_s[pl.ds(r0, SUBLANES), :] * carry + b_s[pl.ds(r0, SUBLANES), :]
        h_s[pl.ds(r0, SUBLANES), :] = hg
        return hg[edge:edge + 1, :]

    carry_ref[...] = lax.fori_loop(0, n_groups, group, carry_ref[...], unroll=True)


def _lru_gates(xb, wa_ref, ba_ref, wx_ref, bx_ref, c8_ref, first_row):
    xb16 = xb.astype(BF16)
    gate_x = _sigmoid(_block_diag(xb16, wx_ref) + bx_ref[...])
    gate_a = _sigmoid(_block_diag(xb16, wa_ref) + ba_ref[...])
    log_a = gate_a * c8_ref[...]
    a = jnp.exp(log_a)
    y = 1.0 - a * a
    mult = jnp.where(y > 0.0, y * lax.rsqrt(y), 0.0)
    r = lax.broadcasted_iota(jnp.int32, (xb.shape[0], 1), 0)
    mult = jnp.where(r == first_row, 1.0, mult)
    return a, mult * gate_x * xb


def _lru_fwd_body(xb_ref, wa_ref, ba_ref, wx_ref, bx_ref, c8_ref, h0_ref, a_s, b_s, carry):
    ti = pl.program_id(1)

    @pl.when(ti == 0)
    def _():
        carry[...] = jnp.zeros_like(carry)

    first_row = jnp.where(ti == 0, 0, -1)
    a, b = _lru_gates(xb_ref[...], wa_ref, ba_ref, wx_ref, bx_ref, c8_ref, first_row)
    _lru_scan_tile(a, b, a_s, b_s, h0_ref, carry, reverse=False)


def _lru_bwd_body(xb_ref, wa_ref, ba_ref, wx_ref, bx_ref, c8_ref, h0_ref, gb_ref, h_ref, wo_ref, bo_ref,
                  o_ref, a_s, b_s, h1_s, carry, *, tm):
    ti = pl.program_id(1)

    @pl.when(ti == 0)
    def _():
        carry[...] = jnp.zeros_like(carry)

    first_row = jnp.where(ti == 0, tm - 1, -1)
    a, b = _lru_gates(xb_ref[...], wa_ref, ba_ref, wx_ref, bx_ref, c8_ref, first_row)
    _lru_scan_tile(a, b, a_s, b_s, h1_s, carry, reverse=True)
    y = h0_ref[...] + h1_s[...]
    z = (gb_ref[...].astype(F32) * y).astype(BF16)
    o_ref[...] = h_ref[...] + _dot(z, wo_ref[...]) + bo_ref[...]


def _lru_mixer(h, g, prm, seq_len, tm=TOKEN_TILE):
    t, d = h.shape
    bsz = t // seq_len
    nt = seq_len // tm
    gb, xb = _lru_in(h, g, prm['in_w'], prm['in_b'], prm['conv_w'], prm['conv_b'], seq_len, tm)
    r_w = xb.shape[1]
    bw = r_w // LRU_BLOCKS
    gate_specs = lambda dr: [_resident((LRU_BLOCKS, bw, bw)), _resident((1, r_w)),
                             _resident((LRU_BLOCKS, bw, bw)), _resident((1, r_w)), _resident((1, r_w))]
    gate_args = lambda dr: [prm['wa'][dr], _row(prm['ba'][dr]), prm['wx'][dr], _row(prm['bx'][dr]),
                            _row(prm['c8'][dr])]
    fwd = lambda b, i: (b * nt + i, 0)
    bwd = lambda b, i: (b * nt + nt - 1 - i, 0)
    scan_scratch = [pltpu.VMEM((tm, r_w), F32), pltpu.VMEM((tm, r_w), F32)]
    h0 = pl.pallas_call(
        _lru_fwd_body, grid=(bsz, nt),
        in_specs=[pl.BlockSpec((tm, r_w), fwd)] + gate_specs(0),
        out_specs=pl.BlockSpec((tm, r_w), fwd), out_shape=jax.ShapeDtypeStruct((t, r_w), F32),
        scratch_shapes=scan_scratch + [pltpu.VMEM((1, r_w), F32)],
        compiler_params=_params("arbitrary", "arbitrary"), name="lru_fwd")(xb, *gate_args(0))
    out = pl.pallas_call(
        functools.partial(_lru_bwd_body, tm=tm), grid=(bsz, nt),
        in_specs=[pl.BlockSpec((tm, r_w), bwd)] + gate_specs(1)
        + [pl.BlockSpec((tm, r_w), bwd), pl.BlockSpec((tm, r_w), bwd), pl.BlockSpec((tm, d), bwd),
           _resident((r_w, d)), _resident((1, d))],
        out_specs=pl.BlockSpec((tm, d), bwd), out_shape=jax.ShapeDtypeStruct((t, d), F32),
        scratch_shapes=scan_scratch + [pltpu.VMEM((tm, r_w), F32), pltpu.VMEM((1, r_w), F32)],
        compiler_params=_params("arbitrary", "arbitrary"), name="lru_bwd")(
            xb, *gate_args(1), h0, gb, h, prm['out_w'], _row(prm['out_b']))
    return out, None


NA_PAIR = 2 * GRID_W
NA_WIN = 5
NA_PB = 16


def _na_qkv_body(h_ref, g_ref, wq_ref, wkt_ref, wv_ref, bq_ref, bk_ref, bv_ref, q_ref, kt_ref, v_ref, *, scale):
    xn = _rms(h_ref[...], g_ref[...]).astype(BF16)
    q_ref[...] = ((_dot(xn, wq_ref[...]) + bq_ref[...]) * scale).astype(BF16)
    kt = lax.dot_general(wkt_ref[...], xn, (((1,), (1,)), ((), ())), preferred_element_type=F32)
    kt_ref[...] = (kt + bk_ref[...]).astype(BF16)
    v_ref[...] = (_dot(xn, wv_ref[...]) + bv_ref[...]).astype(BF16)


def _na_qkv(h, g, wq, wkt, wv, bq, bk, bv, seq_len, scale, tm=TOKEN_TILE):
    t, d = h.shape
    bsz, nt = t // seq_len, seq_len // tm
    tile = pl.BlockSpec((tm, d), lambda i: (i, 0))
    return pl.pallas_call(
        functools.partial(_na_qkv_body, scale=scale), grid=(t // tm,),
        in_specs=[tile, _resident((1, d)), _resident((d, d)), _resident((d, d)), _resident((d, d)),
                  _resident((1, d)), _resident((d, 1)), _resident((1, d))],
        out_specs=[tile, pl.BlockSpec((None, d, tm), lambda i: (i // nt, 0, i % nt)), tile],
        out_shape=[jax.ShapeDtypeStruct((t, d), BF16), jax.ShapeDtypeStruct((bsz, d, seq_len), BF16),
                   jax.ShapeDtypeStruct((t, d), BF16)],
        compiler_params=_params("parallel"), name="na_qkv")(
            h, _row(g), wq, wkt, wv, _row(bq), bk.reshape(d, 1).astype(F32), _row(bv))


def _na_attn_body(q_ref, kp_ref, kc_ref, kn_ref, vp_ref, vc_ref, vn_ref, sb_ref, o_ref, kwin, vwin,
                  *, n_pairs, rows):
    pb = pl.program_id(2)
    blk = NA_PB * NA_PAIR
    for bi, (k_ref, v_ref) in enumerate(((kp_ref, vp_ref), (kc_ref, vc_ref), (kn_ref, vn_ref))):
        for t in range(NA_PB):
            kwin[bi * NA_PB + t] = k_ref[:, t * NA_PAIR:(t + 1) * NA_PAIR]
        vwin[bi * blk:(bi + 1) * blk, :] = v_ref[...]
    n_hg = sb_ref.shape[0]
    dh = LANES // n_hg
    pad = 2 * SUBLANES
    lane = lax.broadcasted_iota(jnp.int32, (1, LANES), 1)
    in_head = [(lane // dh) == j for j in range(n_hg)]
    q_row_onehot = (lax.broadcasted_iota(jnp.int32, (NA_PAIR, pad), 0) // GRID_W
                    == lax.broadcasted_iota(jnp.int32, (NA_PAIR, pad), 1)).astype(BF16)
    mask_row = lax.broadcasted_iota(jnp.int32, (pad, 1), 0)
    k_row_in_win = lax.broadcasted_iota(jnp.int32, (1, NA_WIN * NA_PAIR), 1) // GRID_W

    def pair(i, carry):
        p = pb * NA_PB + i
        kp0 = jnp.clip(p - 2, 0, n_pairs - NA_WIN)
        off = kp0 - (pb - 1) * NA_PB
        m0 = kp0 - p + (NA_WIN - 1)
        qi = q_ref[pl.ds(pl.multiple_of(i * NA_PAIR, NA_PAIR), NA_PAIR), :]
        ksl = jnp.concatenate([kwin[off + jp] for jp in range(NA_WIN)], axis=1)
        vsl = vwin[pl.ds(pl.multiple_of(off * NA_PAIR, NA_PAIR), NA_WIN * NA_PAIR), :]
        r_start = jnp.clip(2 * p + mask_row - NA_KH // 2, 0, rows - NA_KH)
        k_row = 2 * kp0 + k_row_in_win
        in_window = (k_row >= r_start) & (k_row < r_start + NA_KH)
        row_mask = jnp.where(in_window | (mask_row >= 2), 0.0, NEG_INF).astype(BF16)
        q_heads = jnp.concatenate(
            [jnp.concatenate([jnp.where(m, qi, jnp.zeros_like(qi)), q_row_onehot], axis=1) for m in in_head], axis=0)
        bias = jnp.concatenate(
            [jnp.concatenate([sb_ref[j, m0 + jp] for jp in range(NA_WIN)], axis=1) for j in range(n_hg)], axis=0)
        s = _dot(q_heads, jnp.concatenate([ksl, row_mask], axis=0)) + bias
        e = jnp.exp((s - jnp.max(s, axis=-1, keepdims=True)).astype(BF16))
        denom = jnp.sum(e, axis=-1, keepdims=True, dtype=F32)
        pv = _dot(e, vsl) / denom
        acc = jnp.zeros((NA_PAIR, LANES), F32)
        for j, m in enumerate(in_head):
            acc = acc + jnp.where(m, pv[j * NA_PAIR:(j + 1) * NA_PAIR], 0.0)
        o_ref[pl.ds(pl.multiple_of(i * NA_PAIR, NA_PAIR), NA_PAIR), :] = acc.astype(BF16)
        return carry

    lax.fori_loop(0, NA_PB, pair, 0, unroll=True)


def _na_bias_table(rpb, head_dim):
    n_h, n_dr, n_dc = rpb.shape
    n_hg = LANES // head_dim
    col = np.arange(GRID_W)
    q_start = np.clip(col - NA_KW // 2, 0, GRID_W - NA_KW)
    col_ok = (col[None, :] >= q_start[:, None]) & (col[None, :] < q_start[:, None] + NA_KW)
    dc_idx = col[None, :] - col[:, None] + NA_KW - 1
    onehot = ((np.arange(n_dc)[:, None, None] == dc_idx[None]) & col_ok[None]).reshape(n_dc, -1)
    tab = jnp.dot(rpb.astype(F32).reshape(n_h * n_dr, n_dc), jnp.asarray(onehot, F32),
                  precision=lax.Precision.HIGHEST)
    tab = tab.reshape(n_h, n_dr, GRID_W, GRID_W) + jnp.where(col_ok, 0.0, NEG_INF).astype(F32)
    masked = jnp.full((n_h, GRID_W, GRID_W), NEG_INF, F32)

    def quadrant(dr):
        return tab[:, dr + NA_KH - 1] if abs(dr) <= NA_KH - 1 else masked

    tiles = [jnp.concatenate([jnp.concatenate([quadrant(2 * (m - (NA_WIN - 1)) + c - a) for c in range(2)], axis=2)
                              for a in range(2)], axis=1) for m in range(2 * (NA_WIN - 1) + 1)]
    sb = jnp.stack(tiles, axis=1)
    return sb.reshape(n_h // n_hg, n_hg, *sb.shape[1:])


def _na_mixer(h, g, prm, seq_len):
    t, d = h.shape
    bsz = t // seq_len
    rows = seq_len // GRID_W
    n_pairs = rows // 2
    nblk = n_pairs // NA_PB
    assert seq_len == 2 * nblk * NA_PB * GRID_W and rows >= NA_KH and LANES % (d // NA_HEADS) == 0
    blk = NA_PB * NA_PAIR
    q, kt, v = _na_qkv(h, g, prm['wq'], prm['wkt'], prm['wv'], prm['bq'], prm['bk'], prm['bv'], seq_len,
                       (d // NA_HEADS) ** -0.5)
    q = q.reshape(bsz, seq_len, d)
    v = v.reshape(bsz, seq_len, d)
    prev = lambda pb: jnp.maximum(pb - 1, 0)
    nxt = lambda pb: jnp.minimum(pb + 1, nblk - 1)
    k_spec = lambda f: pl.BlockSpec((None, LANES, blk), lambda gi, b, pb: (b, gi, f(pb)))
    v_spec = lambda f: pl.BlockSpec((None, blk, LANES), lambda gi, b, pb: (b, f(pb), gi))
    same = lambda pb: pb
    sb = prm['bias']
    o = pl.pallas_call(
        functools.partial(_na_attn_body, n_pairs=n_pairs, rows=rows),
        grid=(d // LANES, bsz, nblk),
        in_specs=[v_spec(same), k_spec(prev), k_spec(same), k_spec(nxt), v_spec(prev), v_spec(same), v_spec(nxt),
                  pl.BlockSpec((None,) + sb.shape[1:], lambda gi, b, pb: (gi, 0, 0, 0, 0))],
        out_specs=v_spec(same), out_shape=jax.ShapeDtypeStruct((bsz, seq_len, d), BF16),
        scratch_shapes=[pltpu.VMEM((3 * NA_PB, LANES, NA_PAIR), BF16), pltpu.VMEM((3 * blk, LANES), BF16)],
        compiler_params=_params("arbitrary", "arbitrary", "arbitrary"), name="na_attn")(
            q, kt, kt, kt, v, v, v, sb)
    return h, ('plain', [o.reshape(t, d)], [prm['out_w'], prm['out_b']])


def _hy_in_body(hp_ref, hc_ref, hn_ref, g_ref, w_ref, b_ref, cw_ref, cb_ref, x0_ref, vv_ref, u_s,
                *, tiles_per_seq, tm):
    d = x0_ref.shape[1]
    xn = _normed_with_halo(hp_ref, hc_ref, hn_ref, g_ref)
    left = (HY_SHORT - 1) // 2

    def conv(c0):
        u = _dot(xn, w_ref[:, c0:c0 + d]) + b_ref[:, c0:c0 + d]
        _store_zero_padded(u_s, u, tiles_per_seq, tm)
        y = cb_ref[:, c0:c0 + d]
        for j in range(HY_SHORT):
            y = y + cw_ref[j:j + 1, c0:c0 + d] * u_s[pl.ds(HALO + j - left, tm), :]
        return y

    x0_ref[...] = conv(0).astype(x0_ref.dtype)
    x1 = conv(d)
    vv_ref[...] = conv(2 * d) * x1


def _hy_in(h, g, w, b, cw, cb, seq_len, tm=TOKEN_TILE):
    t, d = h.shape
    tile = pl.BlockSpec((tm, d), lambda i: (i, 0))
    return pl.pallas_call(
        functools.partial(_hy_in_body, tiles_per_seq=seq_len // tm, tm=tm), grid=(t // tm,),
        in_specs=_halo_specs(tm, d, t) + [_resident((1, d)), _resident(w.shape), _resident((1, 3 * d)),
                                          _resident(cw.shape), _resident((1, 3 * d))],
        out_specs=[tile, tile],
        out_shape=[jax.ShapeDtypeStruct((t, d), BF16), jax.ShapeDtypeStruct((t, d), F32)],
        scratch_shapes=[pltpu.VMEM((tm + 2 * HALO, d), F32)],
        compiler_params=_params("parallel"), name="hy_in")(h, h, h, _row(g), w, _row(b), cw, _row(cb))


def _hy_filter_body(z_ref, w1_ref, b1_ref, w2_ref, b2_ref, w3_ref, b3_ref, fr_ref, wo_ref, dl_ref,
                    k_ref):
    hp = lax.Precision.HIGHEST
    half, d = z_ref.shape[0], k_ref.shape[2]
    z = z_ref[...]
    fr = fr_ref[...]
    a = jnp.sin(fr * (jnp.dot(z, w1_ref[...], precision=hp, preferred_element_type=F32) + b1_ref[...]))
    a = jnp.sin(fr * (jnp.dot(a, w2_ref[...], precision=hp, preferred_element_type=F32) + b2_ref[...]))
    a = jnp.sin(fr * (jnp.dot(a, w3_ref[...], precision=hp, preferred_element_type=F32) + b3_ref[...]))
    a = a.astype(BF16)
    r = lax.broadcasted_iota(jnp.int32, (half, 1), 0)
    for part in range(2):
        k = _dot(a, wo_ref[part])
        window = jnp.exp(-z[:, part * HY_PACK:part * HY_PACK + 1] * dl_ref[...])
        rows = slice(part * half, (part + 1) * half)
        k_ref[0, rows, :] = k[:, :d] * window
        kb = k[:, d:] * window
        if part == 0:
            kb = jnp.where((r == 0) & (pl.program_id(0) == 0), 0.0, kb)
        k_ref[1, rows, :] = kb


def _block_diag2(x):
    zero = jnp.zeros_like(x)
    return jnp.concatenate([jnp.concatenate([x, zero], axis=-1), jnp.concatenate([zero, x], axis=-1)], axis=-2)


def _pad_to(x, shape):
    return jnp.pad(x.astype(F32), [(0, s - n) for s, n in zip(shape, x.shape)])


@functools.lru_cache(maxsize=None)
def _hy_positions(length, d, tl):
    bands = (HY_EMB - 1) // 2
    t = np.linspace(0.0, 1.0, length)[:, None]
    ang = (2.0 * math.pi / length) * np.arange(length)[:, None]
    fb = np.linspace(1e-4, bands - 1, bands)[None, :]
    z = np.concatenate([t, np.cos(fb * ang), -np.sin(fb * ang)], axis=-1)
    z = np.pad(z, ((0, 0), (0, HY_PACK - z.shape[1]))).reshape(length // tl, 2, tl // 2, HY_PACK)
    z = z.transpose(0, 2, 1, 3).reshape(length // 2, 2 * HY_PACK).astype(np.float32)
    decay_min = math.log(HY_DECAY_TARGET) / HY_DECAY_PCT_LO
    decay_max = math.log(HY_DECAY_TARGET) / HY_DECAY_PCT_HI
    deltas = np.abs(np.linspace(decay_min, decay_max, d))[None, :].astype(np.float32)
    return z, deltas


def _hy_filters(length, prm, d, tl=TOKEN_TILE):
    assert max(prm['f_w1'].shape + prm['f_w2'].shape) <= HY_PACK and length % tl == 0
    z, deltas = _hy_positions(length, d, tl)
    sq = (HY_PACK, HY_PACK)
    twice = lambda v: jnp.tile(_pad_to(v[None], (1, HY_PACK)), (1, 2))
    w_out = _pad_to(prm['f_out'], (HY_PACK, 2 * d))
    w_out = jnp.stack([jnp.concatenate([w_out, jnp.zeros_like(w_out)]),
                       jnp.concatenate([jnp.zeros_like(w_out), w_out])]).astype(BF16)
    args = [z, _block_diag2(_pad_to(prm['f_w1'], sq)), twice(prm['f_b1']),
            _block_diag2(_pad_to(prm['f_w2'], sq)), twice(prm['f_b2']),
            _block_diag2(_pad_to(prm['f_w3'], sq)), twice(prm['f_b3']), twice(prm['f_freq']), w_out, deltas]
    return pl.pallas_call(
        _hy_filter_body, grid=(length // tl,),
        in_specs=[pl.BlockSpec((tl // 2, 2 * HY_PACK), lambda i: (i, 0))] + [_resident(a.shape) for a in args[1:]],
        out_specs=pl.BlockSpec((2, tl, d), lambda i: (0, i, 0)),
        out_shape=jax.ShapeDtypeStruct((2, length, d), F32),
        compiler_params=_params("parallel"), name="hy_filter")(*args)


def _dft_plan(length):
    n = 2 * length
    n1 = min(128, n // 16)
    return n, n1, n // n1, n1 // 2 + 1


@functools.lru_cache(maxsize=None)
def _dft_tables(length):
    n, n1, n2, n1e = _dft_plan(length)
    n1h, sub = n1 // 2, SUBLANES
    groups = n2 // sub
    k1 = np.arange(n1e)
    t_idx = (n2 * np.arange(n1h)[None, None, None, :] + sub * np.arange(groups)[:, None, None, None]
             + np.arange(sub)[None, None, :, None])
    th = 2.0 * np.pi * ((k1[None, :, None, None] * t_idx) % n) / n

    def expand(m):
        return np.einsum('gksi,st->gksit', m, np.eye(sub)).reshape(groups, n1e * sub, n1h * sub)

    fwd1 = np.concatenate([expand(np.cos(th)), expand(-np.sin(th))], axis=1)
    weight = np.where((k1 == 0) | (k1 == n1 // 2), 1.0, 2.0)[None, :, None, None] / n
    inv1 = np.concatenate([expand(weight * np.cos(th)), expand(-weight * np.sin(th))], axis=1).transpose(0, 2, 1)
    inv1 = np.pad(inv1, ((0, 0), (0, 0), (0, -inv1.shape[2] % LANES)))
    idx = np.arange(n2, dtype=np.int64)
    th2 = 2.0 * np.pi * ((idx[:, None] * idx[None, :]) % n2) / n2
    c2, s2 = np.cos(th2), np.sin(th2)
    fwd2 = np.block([[c2, s2], [-s2, c2]])
    inv2 = np.block([[c2, -s2], [s2, c2]])
    return {name: tab.astype(np.float32).astype(BF16)
            for name, tab in (('fwd1', fwd1), ('inv1', inv1), ('fwd2', fwd2), ('inv2', inv2))}


DFT_ROWS = 2 * SUBLANES


def _dft1_body(x_ref, t_ref, o_ref):
    n1h, _, d = x_ref.shape
    x = x_ref[...]
    halves = [_dot(t_ref[g], x[:, g * SUBLANES:(g + 1) * SUBLANES, :].reshape(n1h * SUBLANES, d).astype(BF16))
              .reshape(o_ref.shape[:2] + (SUBLANES, d)) for g in range(DFT_ROWS // SUBLANES)]
    o_ref[...] = jnp.concatenate(halves, axis=2).astype(o_ref.dtype)


def _dft1(x, length, tabs):
    bx, _, d = x.shape
    n, n1, n2, n1e = _dft_plan(length)
    tab = tabs['fwd1']
    a = pl.pallas_call(
        _dft1_body, grid=(n2 // DFT_ROWS, bx),
        in_specs=[pl.BlockSpec((None, n1 // 2, None, DFT_ROWS, d), lambda j, b: (b, 0, j, 0, 0)),
                  pl.BlockSpec((DFT_ROWS // SUBLANES,) + tab.shape[1:], lambda j, b: (j, 0, 0))],
        out_specs=pl.BlockSpec((None, 2, n1e, None, DFT_ROWS, d), lambda j, b: (b, 0, 0, j, 0, 0)),
        out_shape=jax.ShapeDtypeStruct((bx, 2, n1e, n2 // DFT_ROWS, DFT_ROWS, d), BF16),
        compiler_params=_params("parallel", "arbitrary"), name="hy_dft1")(
            x.reshape(bx, n1 // 2, n2 // DFT_ROWS, DFT_ROWS, d), tab)
    return a.reshape(bx, 2, n1e, n2, d)


def _filter_spectrum_body(a_ref, f_ref, o_ref):
    n2 = o_ref.shape[1]
    d = o_ref.shape[2]
    xf = _dot(f_ref[...], a_ref[0].reshape(2 * n2, d))
    xb = _dot(f_ref[...], a_ref[1].reshape(2 * n2, d))
    o_ref[0] = (xf[:n2] + xb[:n2]).astype(o_ref.dtype)
    o_ref[1] = (xf[n2:] - xb[n2:]).astype(o_ref.dtype)


def _filter_spectrum(k, length, tabs):
    d = k.shape[2]
    n, n1, n2, n1e = _dft_plan(length)
    a = _dft1(k, length, tabs)
    return pl.pallas_call(
        _filter_spectrum_body, grid=(n1e,),
        in_specs=[pl.BlockSpec((2, 2, None, n2, d), lambda k: (0, 0, k, 0, 0)), _resident(tabs['fwd2'].shape)],
        out_specs=pl.BlockSpec((2, None, n2, d), lambda k: (0, k, 0, 0)),
        out_shape=jax.ShapeDtypeStruct((2, n1e, n2, d), BF16),
        compiler_params=_params("parallel"), name="hy_filter_spectrum")(a, tabs['fwd2'])


SPECTRAL_BUFS = 3


def _spectral_body(a_hbm, k_ref, f_ref, i_ref, o_ref, a_buf, sem, *, nb, n_steps):
    bb, _, n2, d = o_ref.shape
    step = pl.program_id(0) * nb + pl.program_id(1)

    def fetch(s):
        slot = s % SPECTRAL_BUFS
        return pltpu.make_async_copy(a_hbm.at[pl.ds((s % nb) * bb, bb), :, s // nb], a_buf.at[slot], sem.at[slot])

    @pl.when(step == 0)
    def _():
        for s in range(SPECTRAL_BUFS - 1):
            fetch(s).start()

    @pl.when(step + SPECTRAL_BUFS - 1 < n_steps)
    def _():
        fetch(step + SPECTRAL_BUFS - 1).start()

    fetch(step).wait()
    a_ref = a_buf.at[step % SPECTRAL_BUFS]
    a = jnp.concatenate([a_ref[b].reshape(2 * n2, d) for b in range(bb)], axis=1)
    x = _dot(f_ref[...], a)
    xr, xi = x[:n2], x[n2:]
    kr = jnp.concatenate([k_ref[0]] * bb, axis=1)
    ki = jnp.concatenate([k_ref[1]] * bb, axis=1)
    y = jnp.concatenate([xr * kr - xi * ki, xr * ki + xi * kr], axis=0).astype(BF16)
    z = _dot(i_ref[...], y)
    for b in range(bb):
        o_ref[b] = z[:, b * d:(b + 1) * d].reshape(2, n2, d).astype(o_ref.dtype)


def _spectral_multiply(a, kc, length, tabs):
    bx = a.shape[0]
    _, n1e, n2, d = kc.shape
    bb = max(b for b in range(1, bx + 1) if bx % b == 0 and b * n2 <= 2 * MXU_DIM)
    nb = bx // bb
    assert n1e * nb >= SPECTRAL_BUFS - 1
    blk = pl.BlockSpec((bb, 2, None, n2, d), lambda k, b: (b, 0, k, 0, 0))
    return pl.pallas_call(
        functools.partial(_spectral_body, nb=nb, n_steps=n1e * nb), grid=(n1e, nb),
        in_specs=[pl.BlockSpec(memory_space=pl.ANY), pl.BlockSpec((2, None, n2, d), lambda k, b: (0, k, 0, 0)),
                  _resident(tabs['fwd2'].shape), _resident(tabs['inv2'].shape)],
        out_specs=blk, out_shape=jax.ShapeDtypeStruct((bx, 2, n1e, n2, d), BF16),
        scratch_shapes=[pltpu.VMEM((SPECTRAL_BUFS, bb, 2, n2, d), BF16), pltpu.SemaphoreType.DMA((SPECTRAL_BUFS,))],
        compiler_params=_params("arbitrary", "arbitrary"), name="hy_spectral")(
            a, kc, tabs['fwd2'], tabs['inv2'])


def _idft1_body(z_ref, t_ref, o_ref):
    _, n1e, _, d = z_ref.shape
    rows = 2 * n1e * SUBLANES
    z = z_ref[...].astype(F32)
    halves = []
    for g in range(DFT_ROWS // SUBLANES):
        zg = z[:, :, g * SUBLANES:(g + 1) * SUBLANES, :].reshape(rows, d).astype(BF16)
        zg = jnp.concatenate([zg, jnp.zeros((t_ref.shape[2] - rows, d), BF16)], axis=0)
        halves.append(_dot(t_ref[g], zg).reshape(o_ref.shape[0], SUBLANES, d))
    o_ref[...] = jnp.concatenate(halves, axis=1).astype(o_ref.dtype)


def _idft1(z, length, tabs):
    bx, _, n1e, n2, d = z.shape
    n, n1, _, _ = _dft_plan(length)
    tab = tabs['inv1']
    y = pl.pallas_call(
        _idft1_body, grid=(n2 // DFT_ROWS, bx),
        in_specs=[pl.BlockSpec((None, 2, n1e, None, DFT_ROWS, d), lambda j, b: (b, 0, 0, j, 0, 0)),
                  pl.BlockSpec((DFT_ROWS // SUBLANES,) + tab.shape[1:], lambda j, b: (j, 0, 0))],
        out_specs=pl.BlockSpec((None, n1 // 2, None, DFT_ROWS, d), lambda j, b: (b, 0, j, 0, 0)),
        out_shape=jax.ShapeDtypeStruct((bx, n1 // 2, n2 // DFT_ROWS, DFT_ROWS, d), BF16),
        compiler_params=_params("parallel", "arbitrary"), name="hy_idft1")(
            z.reshape(bx, 2, n1e, n2 // DFT_ROWS, DFT_ROWS, d), tab)
    return y.reshape(bx, length, d)


def _hyena_mixer(h, g, prm, seq_len):
    t, d = h.shape
    bsz = t // seq_len
    tabs = _dft_tables(seq_len)
    x0, vv = _hy_in(h, g, prm['in_w'], prm['in_b'], prm['conv_w'], prm['conv_b'], seq_len)
    kc = _filter_spectrum(_hy_filters(seq_len, prm, d), seq_len, tabs)
    a = _dft1(vv.reshape(bsz, seq_len, d), seq_len, tabs)
    z = _spectral_multiply(a, kc, seq_len, tabs)
    yc = _idft1(z, seq_len, tabs).reshape(t, d)
    return h, ('gated', [yc, vv, x0], [prm['skip'], prm['out_w'], prm['out_b']])


def _trunk(x, p, w, depth):
    bsz, seq_len, d = x.shape
    assert seq_len % FFN_TILE == 0 and d % LANES == 0
    h = x.reshape(bsz * seq_len, d)
    p = p.reshape(depth, bsz * seq_len, -1)
    for i in range(depth):
        kind, j = i % N_MIXERS, i // N_MIXERS
        h = _ffn(h, i, w['ln_ffn1'], w['ffn1_wg'], w['ffn1_wu'], w['ffn1_wd'])
        mix = (_hyena_mixer, _na_mixer, _lru_mixer)[kind]
        h, pending = mix(h, w['ln_mix'][i], w[('hy', 'na', 'lru')[kind]][j], seq_len)
        h = _ffn(h, i, w['ln_ffn2'], w['ffn2_wg'], w['ffn2_wu'], w['ffn2_wd'], pre=pending,
                 ple=(w['ln_ple'], w['ple_gate'], p, w['ple_proj']),
                 final=w['ln_final'] if i == depth - 1 else None)
    return h.reshape(bsz, seq_len, d)


def kernel(x_prompt, x_sample, p_prompt, p_sample, ln_ffn1, ffn1_wg, ffn1_wu, ffn1_wd, ln_mix, ln_ffn2, ffn2_wg, ffn2_wu, ffn2_wd, ln_ple, ple_gate, ple_proj, ln_final, hy_in_w, hy_in_b, hy_conv_w, hy_conv_b, hy_f_w1, hy_f_b1, hy_f_w2, hy_f_b2, hy_f_w3, hy_f_b3, hy_f_freq, hy_f_out, hy_skip, hy_out_w, hy_out_b, na_qkv_w, na_qkv_b, na_rpb, na_out_w, na_out_b, lru_in_w, lru_in_b, lru_conv_w, lru_conv_b, lru_wa, lru_ba, lru_wx, lru_bx, lru_lambda, lru_out_w, lru_out_b):
    depth = ln_ffn1.shape[0]
    d = x_prompt.shape[-1]
    b16 = lambda a: a.astype(BF16)
    w = {
        'ln_ffn1': ln_ffn1, 'ffn1_wg': b16(ffn1_wg), 'ffn1_wu': b16(ffn1_wu), 'ffn1_wd': b16(ffn1_wd),
        'ln_mix': ln_mix, 'ln_ffn2': ln_ffn2, 'ffn2_wg': b16(ffn2_wg), 'ffn2_wu': b16(ffn2_wu),
        'ffn2_wd': b16(ffn2_wd), 'ln_ple': ln_ple, 'ple_gate': b16(ple_gate), 'ple_proj': b16(ple_proj),
        'ln_final': ln_final,
        'hy': [dict(in_w=b16(hy_in_w[j]), in_b=hy_in_b[j], conv_w=hy_conv_w[j], conv_b=hy_conv_b[j],
                    f_w1=hy_f_w1[j], f_b1=hy_f_b1[j], f_w2=hy_f_w2[j], f_b2=hy_f_b2[j], f_w3=hy_f_w3[j],
                    f_b3=hy_f_b3[j], f_freq=hy_f_freq[j], f_out=hy_f_out[j], skip=hy_skip[j],
                    out_w=b16(hy_out_w[j]), out_b=hy_out_b[j]) for j in range(hy_in_w.shape[0])],
        'na': [dict(wq=b16(na_qkv_w[j][:, :d]), wkt=b16(na_qkv_w[j][:, d:2 * d].T), wv=b16(na_qkv_w[j][:, 2 * d:]),
                    bq=na_qkv_b[j][:d], bk=na_qkv_b[j][d:2 * d], bv=na_qkv_b[j][2 * d:],
                    bias=_na_bias_table(na_rpb[j], d // NA_HEADS), out_w=b16(na_out_w[j]), out_b=na_out_b[j])
               for j in range(na_qkv_w.shape[0])],
        'lru': [dict(in_w=b16(lru_in_w[j]), in_b=lru_in_b[j], conv_w=lru_conv_w[j], conv_b=lru_conv_b[j],
                     wa=b16(lru_wa[j]), ba=lru_ba[j], wx=b16(lru_wx[j]), bx=lru_bx[j],
                     c8=-LRU_C * jax.nn.softplus(-lru_lambda[j].astype(F32)),
                     out_w=b16(lru_out_w[j]), out_b=lru_out_b[j]) for j in range(lru_in_w.shape[0])],
    }
    return _trunk(x_prompt, p_prompt, w, depth), _trunk(x_sample, p_sample, w, depth)
```
